```python
import math
import jax, jax.numpy as jnp
from jax import lax
import numpy as np

D_MODEL = 1024
BATCH = 8
SEQ = 4096
DEPTH = 2

GRID_W = 64
CTX_LEN = 256
EPS = 1e-6
ROPE_BASE = 10000.0
MIX_WIDTH = D_MODEL
HALF_MIX = MIX_WIDTH // 2
SHORT_CONV_W = 5

A_HEAD_DIM = 64
A_HEADS = HALF_MIX // A_HEAD_DIM
A_KV_HEADS = A_HEADS // 4
A_GROUP = A_HEADS // A_KV_HEADS
A_WINDOW = 128
A_BLOCK = 128

B_HEADS = 4
B_V_DIM = HALF_MIX // B_HEADS
B_QK_DIM = B_V_DIM // 2
B_CHUNK = 64

C_HEAD_DIM = 64
C_HEADS = HALF_MIX // (2 * C_HEAD_DIM)
C_BLOCK = 128

D_HEAD_DIM = 64
D_HEADS = HALF_MIX // D_HEAD_DIM
D_GROUPS = 2
D_STATE = 128
D_CHUNK = 128

FFN_HIDDEN = ((8 * D_MODEL // 3 + 255) // 256) * 256
FFN_CONV_W = 3

A_Q = A_HEADS * A_HEAD_DIM
A_KV = A_KV_HEADS * A_HEAD_DIM
B_QK = 2 * B_HEADS * B_QK_DIM
B_V = B_HEADS * B_V_DIM
B_GATES = 4 * B_HEADS
C_QK = C_HEADS * 2 * C_HEAD_DIM
C_V = C_HEADS * 2 * C_HEAD_DIM
D_INNER = D_HEADS * D_HEAD_DIM
D_XBC = D_INNER + 2 * D_GROUPS * D_STATE
EVEN_SIZES = (A_Q, A_KV, A_KV, B_QK, B_V, B_V, B_GATES)
ODD_SIZES = (C_QK, C_QK, C_V, D_INNER, D_XBC, 2 * D_HEADS)
EVEN_IN = sum(EVEN_SIZES)
ODD_IN = sum(ODD_SIZES)

kernel_name = 'hybrid_diffusion_gqa_mlstm_diffattn_ssd'


def rmsnorm(x, g):
    xf = x.astype(jnp.float32)
    y = xf * lax.rsqrt(jnp.mean(xf * xf, axis=-1, keepdims=True) + EPS)
    return (y * g.astype(jnp.float32)).astype(x.dtype)


def modulate(u, shift, scale):
    return u * (1.0 + scale) + shift


def split_cols(y, sizes):
    return jnp.split(y, [int(s) for s in np.cumsum(sizes)[:-1]], axis=-1)


def dwconv(x, w):
    k = w.shape[0]
    pad = k // 2
    t = x.shape[1]
    xp = jnp.pad(x, ((0, 0), (pad, pad), (0, 0)))
    y = xp[:, 0:t] * w[0]
    for j in range(1, k):
        y = y + xp[:, j:j + t] * w[j]
    return y


def axial_rope_tables(rows, head_dim):
    pos = jnp.arange(rows * GRID_W)
    row = (pos // GRID_W).astype(jnp.float32)
    col = (pos % GRID_W).astype(jnp.float32)
    nf = head_dim // 4
    inv = ROPE_BASE ** (-jnp.arange(nf, dtype=jnp.float32) / nf)
    ar = row[:, None] * inv
    ac = col[:, None] * inv
    return (jnp.cos(ar), jnp.sin(ar), jnp.cos(ac), jnp.sin(ac))


def _rot_half(x, cos, sin):
    x1, x2 = jnp.split(x, 2, axis=-1)
    return jnp.concatenate([x1 * cos - x2 * sin, x2 * cos + x1 * sin], axis=-1)


def rope_2d(x, tabs):
    def shape(t):
        return t.reshape(t.shape[:1] + (1,) * (x.ndim - 3) + t.shape[1:]).astype(x.dtype)
    cr, sr, cc, sc = (shape(t) for t in tabs)
    xr, xc = jnp.split(x, 2, axis=-1)
    return jnp.concatenate([_rot_half(xr, cr, sr), _rot_half(xc, cc, sc)], axis=-1)


def _flip_time(arrays):
    return tuple(jnp.flip(a, axis=1) for a in arrays)


def bidir(run, ctx_dirs, lat_dirs, zero_state, need_ctx):
    y_ctx, y_lat = None, None
    for d in range(2):
        ci, li = ctx_dirs[d], lat_dirs[d]
        if d == 1:
            ci, li = _flip_time(ci), _flip_time(li)
        yc, state = run(*ci, zero_state, need_ctx)
        yl, _ = run(*li, state, True)
        if d == 1:
            yl = jnp.flip(yl, axis=1)
            yc = jnp.flip(yc, axis=1) if need_ctx else None
        y_lat = yl if y_lat is None else y_lat + yl
        if need_ctx:
            y_ctx = yc if y_ctx is None else y_ctx + yc
    return y_ctx, y_lat


def mlstm_chunked(q, k, v, log_i, log_f, state, want_out):
    bsz, t, nh, _ = q.shape
    L = B_CHUNK
    nc = t // L

    def ch(a):
        return a.reshape((bsz, nc, L) + a.shape[2:])
    q, k, v = ch(q), ch(k), ch(v)
    li = jnp.swapaxes(ch(log_i), 2, 3)
    b = jnp.cumsum(jnp.swapaxes(ch(log_f), 2, 3), axis=-1)
    b_last = b[..., -1]
    w = b_last[..., None] - b + li
    m_loc = jnp.max(w, axis=-1)
    e = jnp.exp(w - m_loc[..., None])
    c_loc = jnp.einsum('bchl,bclhv,bclhk->bchvk', e, v, k)
    n_loc = jnp.einsum('bchl,bclhk->bchk', e, k)

    def step(carry, inp):
        c_prev, n_prev, m_prev = carry
        bl, cl, nl, ml = inp
        m_new = jnp.maximum(bl + m_prev, ml)
        a = jnp.exp(bl + m_prev - m_new)
        g = jnp.exp(ml - m_new)
        c_new = a[..., None, None] * c_prev + g[..., None, None] * cl
        n_new = a[..., None] * n_prev + g[..., None] * nl
        return (c_new, n_new, m_new), (c_prev, n_prev, m_prev)

    lead = lambda a: jnp.moveaxis(a, 1, 0)
    final, (c0, n0, m0) = lax.scan(step, state, (lead(b_last), lead(c_loc), lead(n_loc), lead(m_loc)))
    if not want_out:
        return None, final
    c0, n0, m0 = jnp.moveaxis(c0, 0, 1), jnp.moveaxis(n0, 0, 1), jnp.moveaxis(m0, 0, 1)
    tri = jnp.tril(jnp.ones((L, L), dtype=bool))
    log_d = jnp.where(tri, b[..., :, None] - b[..., None, :] + li[..., None, :], -jnp.inf)
    g_in = b + m0[..., None]
    m_t = jnp.maximum(jnp.max(log_d, axis=-1), g_in)
    s = jnp.einsum('bcthk,bcshk->bchts', q, k) * jnp.exp(log_d - m_t[..., None])
    a_in = jnp.exp(g_in - m_t)
    num = jnp.einsum('bchts,bcshv->bcthv', s, v) + jnp.einsum('bcht,bchvk,bcthk->bcthv', a_in, c0, q)
    den = jnp.sum(s, axis=-1) + a_in * jnp.einsum('bchk,bcthk->bcht', n0, q)
    den = jnp.maximum(jnp.abs(den), jnp.exp(-m_t))
    h = num / jnp.swapaxes(den, 2, 3)[..., None]
    return h.reshape(bsz, t, nh, -1), final


def ssd_chunked(x, dt, a, bm, cm, state, want_out):
    bsz, t = x.shape[:2]
    L = D_CHUNK
    nc = t // L

    def ch(z):
        return z.reshape((bsz, nc, L) + z.shape[2:])
    x, dt, a, bm, cm = ch(x), ch(dt), ch(a), ch(bm), ch(cm)
    acum = jnp.cumsum(a, axis=2)
    a_last = acum[:, :, -1]
    wst = jnp.exp(a_last[:, :, None] - acum) * dt
    s_loc = jnp.einsum('bclgh,bclgn,bclghp->bcghpn', wst, bm, x)

    def step(h, inp):
        da, sl = inp
        return jnp.exp(da)[..., None, None] * h + sl, h

    final, h0 = lax.scan(step, state, (jnp.moveaxis(a_last, 1, 0), jnp.moveaxis(s_loc, 1, 0)))
    if not want_out:
        return None, final
    h0 = jnp.moveaxis(h0, 0, 1)
    ac = jnp.moveaxis(acum, 2, -1)
    tri = jnp.tril(jnp.ones((L, L), dtype=bool))
    seg = jnp.where(tri, ac[..., :, None] - ac[..., None, :], -jnp.inf)
    cb = jnp.einsum('bctgn,bcsgn->bcgts', cm, bm)
    mix = cb[:, :, :, None] * jnp.exp(seg) * jnp.moveaxis(dt, 2, -1)[..., None, :]
    y = jnp.einsum('bcghts,bcsghp->bctghp', mix, x)
    y = y + jnp.einsum('bctgn,bcghpn->bctghp', cm, h0) * jnp.exp(acum)[..., None]
    return y.reshape((bsz, t) + y.shape[3:]), final


def windowed_gqa_latent(q, k, v, k_ctx, v_ctx, sink):
    bsz, t = q.shape[:2]
    w = A_BLOCK
    nb = t // w
    scale = A_HEAD_DIM ** -0.5

    def windows(z):
        zp = jnp.pad(z, ((0, 0), (w, w), (0, 0), (0, 0))).reshape(bsz, nb + 2, w, A_KV_HEADS, A_HEAD_DIM)
        return jnp.concatenate([zp[:, :-2], zp[:, 1:-1], zp[:, 2:]], axis=2)
    kw, vw = windows(k), windows(v)
    qb = q.reshape(bsz, nb, w, A_KV_HEADS, A_GROUP, A_HEAD_DIM)
    s_loc = jnp.einsum('bnqhgd,bnkhd->bnhgqk', qb, kw).astype(jnp.float32) * scale
    qi = jnp.arange(w)[:, None]
    kj = jnp.arange(3 * w)[None, :]
    kpos = (jnp.arange(nb) * w)[:, None, None] - w + kj[None]
    mask = (jnp.abs(kj - w - qi) <= A_WINDOW)[None] & (kpos >= 0) & (kpos < t)
    s_loc = jnp.where(mask[None, :, None, None], s_loc, -jnp.inf)
    s_ctx = jnp.einsum('bnqhgd,bchd->bnhgqc', qb, k_ctx).astype(jnp.float32) * scale
    s_sink = jnp.broadcast_to(sink.astype(jnp.float32).reshape(A_KV_HEADS, A_GROUP, 1, 1), s_ctx.shape[:-1] + (1,))
    p = jax.nn.softmax(jnp.concatenate([s_loc, s_ctx, s_sink], axis=-1), axis=-1).astype(v.dtype)
    o = (jnp.einsum('bnhgqk,bnkhd->bnqhgd', p[..., :3 * w], vw)
         + jnp.einsum('bnhgqc,bchd->bnqhgd', p[..., 3 * w:-1], v_ctx))
    return o.reshape(bsz, t, A_Q)


def gqa_context(q, k, v, sink):
    bsz, tc = q.shape[:2]
    s = jnp.einsum('bqhgd,bkhd->bhgqk', q, k).astype(jnp.float32) * (A_HEAD_DIM ** -0.5)
    s_sink = jnp.broadcast_to(sink.astype(jnp.float32).reshape(A_KV_HEADS, A_GROUP, 1, 1), s.shape[:-1] + (1,))
    p = jax.nn.softmax(jnp.concatenate([s, s_sink], axis=-1), axis=-1)[..., :-1].astype(v.dtype)
    return jnp.einsum('bhgqk,bkhd->bqhgd', p, v).reshape(bsz, tc, A_Q)


def even_mixer(uc, ux, w_in, w_out, sink, conv_w, gate_b, norm_gain, rope, need_ctx):
    pc = split_cols(uc @ w_in, EVEN_SIZES)
    px = split_cols(ux @ w_in, EVEN_SIZES)
    bsz = ux.shape[0]

    def attn_heads(p):
        b_, t_ = p[0].shape[:2]
        return (p[0].reshape(b_, t_, A_KV_HEADS, A_GROUP, A_HEAD_DIM),
                p[1].reshape(b_, t_, A_KV_HEADS, A_HEAD_DIM),
                p[2].reshape(b_, t_, A_KV_HEADS, A_HEAD_DIM))
    qc, kc, vc = attn_heads(pc)
    qx, kx, vx = attn_heads(px)
    ya_x = windowed_gqa_latent(rope_2d(qx, rope), rope_2d(kx, rope), vx, kc, vc, sink)

    def mlstm_inputs(p):
        b_, t_ = p[3].shape[:2]
        q, k = jnp.split(jax.nn.silu(dwconv(p[3], conv_w)), 2, axis=-1)
        q = q.reshape(b_, t_, B_HEADS, B_QK_DIM).astype(jnp.float32) * (B_QK_DIM ** -0.5)
        k = k.reshape(b_, t_, B_HEADS, B_QK_DIM).astype(jnp.float32)
        v = p[4].reshape(b_, t_, B_HEADS, B_V_DIM).astype(jnp.float32)
        g = (p[6].astype(jnp.float32) + gate_b.astype(jnp.float32)).reshape(b_, t_, 4, B_HEADS)
        return tuple((q, k, v, g[:, :, d], jax.nn.log_sigmoid(g[:, :, 2 + d])) for d in range(2))

    zero = (jnp.zeros((bsz, B_HEADS, B_V_DIM, B_QK_DIM), jnp.float32),
            jnp.zeros((bsz, B_HEADS, B_QK_DIM), jnp.float32),
            jnp.zeros((bsz, B_HEADS), jnp.float32))
    hb_c, hb_x = bidir(mlstm_chunked, mlstm_inputs(pc), mlstm_inputs(px), zero, need_ctx)

    def mlstm_out(h, p):
        b_, t_ = p[5].shape[:2]
        o = jax.nn.sigmoid(p[5].reshape(b_, t_, B_HEADS, B_V_DIM))
        return (rmsnorm(h, norm_gain.reshape(B_HEADS, B_V_DIM)).astype(o.dtype) * o).reshape(b_, t_, B_V)

    yx = jnp.concatenate([ya_x, mlstm_out(hb_x, px)], axis=-1) @ w_out
    yc = None
    if need_ctx:
        ya_c = gqa_context(qc, kc, vc, sink)
        yc = jnp.concatenate([ya_c, mlstm_out(hb_c, pc)], axis=-1) @ w_out
    return yc, yx


def diff_attn(q, k, v, lam, scale):
    s = jnp.einsum('bqhmd,bkhmd->bhmqk', q, k).astype(jnp.float32) * scale
    p = jax.nn.softmax(s, axis=-1)
    a = (p[:, :, 0] - lam * p[:, :, 1]).astype(v.dtype)
    return jnp.einsum('bhqk,bkhv->bqhv', a, v)


def odd_mixer(uc, ux, w_in, w_out, lam_vecs, c_gain, conv_w, conv_b, dt_bias, a_log, skip, d_gain,
              rope, lam_init, need_ctx):
    pc = split_cols(uc @ w_in, ODD_SIZES)
    px = split_cols(ux @ w_in, ODD_SIZES)
    bsz, t = ux.shape[:2]

    def diff_heads(p):
        b_, t_ = p[0].shape[:2]
        return (p[0].reshape(b_, t_, C_HEADS, 2, C_HEAD_DIM),
                p[1].reshape(b_, t_, C_HEADS, 2, C_HEAD_DIM),
                p[2].reshape(b_, t_, C_HEADS, 2 * C_HEAD_DIM))
    qc, kc, vc = diff_heads(pc)
    qx, kx, vx = diff_heads(px)
    lv = lam_vecs.astype(jnp.float32)
    lam = jnp.exp(jnp.sum(lv[0] * lv[1])) - jnp.exp(jnp.sum(lv[2] * lv[3])) + lam_init
    scale = C_HEAD_DIM ** -0.5
    k_all = jnp.concatenate([rope_2d(kx, rope), kc], axis=1)
    v_all = jnp.concatenate([vx, vc], axis=1)
    nb = t // C_BLOCK
    qb = jnp.moveaxis(rope_2d(qx, rope).reshape(bsz, nb, C_BLOCK, C_HEADS, 2, C_HEAD_DIM), 1, 0)
    o_x = lax.map(lambda qblk: diff_attn(qblk, k_all, v_all, lam, scale), qb)
    o_x = jnp.moveaxis(o_x, 0, 1).reshape(bsz, t, C_HEADS, 2 * C_HEAD_DIM)

    def diff_out(o):
        return (rmsnorm(o, c_gain.reshape(C_HEADS, 2 * C_HEAD_DIM)) * (1.0 - lam_init)).reshape(
            o.shape[0], o.shape[1], C_V)

    hg = D_HEADS // D_GROUPS

    def ssd_inputs(p):
        b_, t_ = p[4].shape[:2]
        xbc = jax.nn.silu(dwconv(p[4], conv_w) + conv_b).astype(jnp.float32)
        xs, bm, cm = jnp.split(xbc, [D_INNER, D_INNER + D_GROUPS * D_STATE], axis=-1)
        xs = xs.reshape(b_, t_, D_GROUPS, hg, D_HEAD_DIM)
        bm = bm.reshape(b_, t_, D_GROUPS, D_STATE)
        cm = cm.reshape(b_, t_, D_GROUPS, D_STATE)
        dtr = p[5].astype(jnp.float32).reshape(b_, t_, 2, D_GROUPS, hg)
        dirs = []
        for d in range(2):
            dt = jax.nn.softplus(dtr[:, :, d] + dt_bias[d].astype(jnp.float32).reshape(D_GROUPS, hg))
            a = dt * (-jnp.exp(a_log[d].astype(jnp.float32))).reshape(D_GROUPS, hg)
            dirs.append((xs, dt, a, bm, cm))
        return xs, tuple(dirs)

    xs_c, dirs_c = ssd_inputs(pc)
    xs_x, dirs_x = ssd_inputs(px)
    zero = jnp.zeros((bsz, D_GROUPS, hg, D_HEAD_DIM, D_STATE), jnp.float32)
    ys_c, ys_x = bidir(ssd_chunked, dirs_c, dirs_x, zero, need_ctx)

    def ssd_out(y, xs, p):
        b_, t_ = p[3].shape[:2]
        y = y + skip.astype(jnp.float32).reshape(D_GROUPS, hg, 1) * xs
        z = jax.nn.silu(p[3].astype(jnp.float32))
        yz = y.reshape(b_, t_, D_GROUPS, hg * D_HEAD_DIM) * z.reshape(b_, t_, D_GROUPS, hg * D_HEAD_DIM)
        return rmsnorm(yz, d_gain.reshape(D_GROUPS, hg * D_HEAD_DIM)).reshape(b_, t_, D_INNER).astype(p[3].dtype)

    yx = jnp.concatenate([diff_out(o_x), ssd_out(ys_x, xs_x, px)], axis=-1) @ w_out
    yc = None
    if need_ctx:
        o_c = diff_attn(qc, kc, vc, lam, scale)
        yc = jnp.concatenate([diff_out(o_c), ssd_out(ys_c, xs_c, pc)], axis=-1) @ w_out
    return yc, yx


def conv_ffn(u, w_gate, w_up, conv_w, w_down):
    g = dwconv(u @ w_gate, conv_w)
    return (jax.nn.silu(g) * (u @ w_up)) @ w_down


def setup_inputs(seed: int = 0) -> dict:
    key = jax.random.key(seed)
    ks = jax.random.split(key, 32)
    f32 = jnp.float32
    D = D_MODEL
    ne, no = (DEPTH + 1) // 2, DEPTH // 2

    def nrm(i, shape, s):
        return jax.random.normal(ks[i], shape, f32) * s

    x = nrm(0, (BATCH, SEQ, D), 1.0)
    c = nrm(1, (BATCH, D), 1.0)
    ctx = nrm(2, (BATCH, CTX_LEN, D), 1.0)
    c_ctx = nrm(3, (D,), 1.0)
    mod_w = nrm(4, (DEPTH, D, 6 * D), 0.5 * D ** -0.5)
    mod_b = nrm(5, (DEPTH, 6 * D), 0.02)
    norm_g = 1.0 + nrm(6, (DEPTH, 4, D), 0.02)
    ffn_w_gate = nrm(7, (DEPTH, D, FFN_HIDDEN), D ** -0.5)
    ffn_w_up = nrm(8, (DEPTH, D, FFN_HIDDEN), D ** -0.5)
    ffn_conv = nrm(9, (DEPTH, FFN_CONV_W, FFN_HIDDEN), FFN_CONV_W ** -0.5)
    ffn_w_down = nrm(10, (DEPTH, FFN_HIDDEN, D), FFN_HIDDEN ** -0.5)
    ev_w_in = nrm(11, (ne, D, EVEN_IN), D ** -0.5)
    ev_w_out = nrm(12, (ne, MIX_WIDTH, D), MIX_WIDTH ** -0.5)
    a_sink = nrm(13, (ne, A_HEADS), 0.5)
    b_conv = nrm(14, (ne, SHORT_CONV_W, B_QK), SHORT_CONV_W ** -0.5)
    f_bias = jnp.broadcast_to(jnp.tile(jnp.linspace(3.0, 6.0, B_HEADS), 2), (ne, 2 * B_HEADS))
    b_gate_b = jnp.concatenate([nrm(15, (ne, 2 * B_HEADS), 0.1), f_bias + nrm(16, (ne, 2 * B_HEADS), 0.1)], axis=-1)
    b_norm_g = 1.0 + nrm(17, (ne, B_V), 0.02)
    od_w_in = nrm(18, (no, D, ODD_IN), D ** -0.5)
    od_w_out = nrm(19, (no, MIX_WIDTH, D), MIX_WIDTH ** -0.5)
    c_lambda = nrm(20, (no, 4, C_HEAD_DIM), 0.1)
    c_norm_g = 1.0 + nrm(21, (no, C_V), 0.02)
    d_conv = nrm(22, (no, SHORT_CONV_W, D_XBC), SHORT_CONV_W ** -0.5)
    d_conv_b = nrm(23, (no, D_XBC), 0.02)
    dt0 = jnp.exp(jax.random.uniform(ks[24], (no, 2, D_HEADS), f32, math.log(1e-3), math.log(1e-1)))
    d_dt_bias = dt0 + jnp.log(-jnp.expm1(-dt0))
    d_a_log = jnp.log(jax.random.uniform(ks[25], (no, 2, D_HEADS), f32, 1.0, 16.0))
    d_skip = 1.0 + nrm(26, (no, D_HEADS), 0.02)
    d_norm_g = 1.0 + nrm(27, (no, D_INNER), 0.02)
    return {'x': x, 'c': c, 'ctx': ctx, 'c_ctx': c_ctx, 'mod_w': mod_w, 'mod_b': mod_b, 'norm_g': norm_g,
            'ffn_w_gate': ffn_w_gate, 'ffn_w_up': ffn_w_up, 'ffn_conv': ffn_conv, 'ffn_w_down': ffn_w_down,
            'ev_w_in': ev_w_in, 'ev_w_out': ev_w_out, 'a_sink': a_sink, 'b_conv': b_conv, 'b_gate_b': b_gate_b,
            'b_norm_g': b_norm_g, 'od_w_in': od_w_in, 'od_w_out': od_w_out, 'c_lambda': c_lambda,
            'c_norm_g': c_norm_g, 'd_conv': d_conv, 'd_conv_b': d_conv_b, 'd_dt_bias': d_dt_bias,
            'd_a_log': d_a_log, 'd_skip': d_skip, 'd_norm_g': d_norm_g}


def reference(x, c, ctx, c_ctx, mod_w, mod_b, norm_g, ffn_w_gate, ffn_w_up, ffn_conv, ffn_w_down,
              ev_w_in, ev_w_out, a_sink, b_conv, b_gate_b, b_norm_g, od_w_in, od_w_out, c_lambda,
              c_norm_g, d_conv, d_conv_b, d_dt_bias, d_a_log, d_skip, d_norm_g):
    rows = x.shape[1] // GRID_W
    rope_a = axial_rope_tables(rows, A_HEAD_DIM)
    rope_c = axial_rope_tables(rows, C_HEAD_DIM)
    hx, hc = x, ctx
    for l in range(DEPTH):
        need_ctx = l < DEPTH - 1
        j = l // 2
        mx = jnp.split((jax.nn.silu(c) @ mod_w[l] + mod_b[l])[:, None, :], 6, axis=-1)
        mc = jnp.split(jax.nn.silu(c_ctx) @ mod_w[l] + mod_b[l], 6, axis=-1)
        g = norm_g[l]
        ux = modulate(rmsnorm(hx, g[0]), mx[0], mx[1])
        uc = modulate(rmsnorm(hc, g[0]), mc[0], mc[1])
        if l % 2 == 0:
            yc, yx = even_mixer(uc, ux, ev_w_in[j], ev_w_out[j], a_sink[j], b_conv[j], b_gate_b[j],
                                b_norm_g[j], rope_a, need_ctx)
        else:
            lam_init = 0.8 - 0.6 * math.exp(-0.3 * l)
            yc, yx = odd_mixer(uc, ux, od_w_in[j], od_w_out[j], c_lambda[j], c_norm_g[j], d_conv[j],
                               d_conv_b[j], d_dt_bias[j], d_a_log[j], d_skip[j], d_norm_g[j], rope_c,
                               lam_init, need_ctx)
        hx = hx + mx[2] * rmsnorm(yx, g[1])
        ux = modulate(rmsnorm(hx, g[2]), mx[3], mx[4])
        hx = hx + mx[5] * rmsnorm(conv_ffn(ux, ffn_w_gate[l], ffn_w_up[l], ffn_conv[l], ffn_w_down[l]), g[3])
        if need_ctx:
            hc = hc + mc[2] * rmsnorm(yc, g[1])
            uc = modulate(rmsnorm(hc, g[2]), mc[3], mc[4])
            hc = hc + mc[5] * rmsnorm(conv_ffn(uc, ffn_w_gate[l], ffn_w_up[l], ffn_conv[l], ffn_w_down[l]), g[3])
    return hx
```

```python
import functools
import math

import jax
import jax.numpy as jnp
from jax import lax
from jax.experimental import pallas as pl
from jax.experimental.pallas import tpu as pltpu

F32 = jnp.float32
BF16 = jnp.bfloat16

EPS = 1e-6
ROPE_BASE = 10000.0
GRID_W = 64
HEAD_DIM = 64
A_KV_HEADS = 2
A_GROUP = 4
A_WINDOW = 128
B_HEADS = 4
D_GROUPS = 2
LAM_DEPTH_RATE = 0.3

LANES = 128
HALO = 8
TM = 256
CHUNK = 128
FFN_CHUNK = 256
TQ = 256
NEG = -1e30
VMEM_LIMIT = 56 * 1024 * 1024


def _cparams(*sem):
    return pltpu.CompilerParams(dimension_semantics=sem, vmem_limit_bytes=VMEM_LIMIT)


def _silu(x):
    return x * jax.nn.sigmoid(x)


def _softplus(x):
    return jnp.maximum(x, 0.0) + jnp.log1p(jnp.exp(-jnp.abs(x)))


def _log_sigmoid(x):
    return -_softplus(-x)


def _rms(x):
    return x * lax.rsqrt(jnp.mean(x * x, axis=-1, keepdims=True) + EPS)


def _dot(a, b):
    return jnp.dot(a, b, preferred_element_type=F32)


def _dot_nt(a, b):
    return lax.dot_general(a, b, (((1,), (1,)), ((), ())), preferred_element_type=F32)


def _dot_tn(a, b):
    return lax.dot_general(a, b, (((0,), (0,)), ((), ())), preferred_element_type=F32)


def _split3(x):
    x1 = x.astype(BF16)
    r1 = x - x1.astype(F32)
    x2 = r1.astype(BF16)
    x3 = (r1 - x2.astype(F32)).astype(BF16)
    return x1, x2, x3


def _cumsum_cols(tri, x):
    x1, x2, x3 = _split3(x)
    return _dot(tri, x1) + _dot(tri, x2) + _dot(tri, x3)


def _cumsum_rows(x, tri):
    x1, x2, x3 = _split3(x)
    return _dot_nt(x1, tri) + _dot_nt(x2, tri) + _dot_nt(x3, tri)


def _mod_body(c_ref, w_ref, b_ref, o_ref):
    a = _silu(c_ref[...]).astype(BF16)
    o_ref[...] = _dot(a, w_ref[...].astype(BF16)) + b_ref[...]


def _mod_call(cc, mod_w, mod_b):
    depth, d, n6 = mod_w.shape
    rows = cc.shape[0]
    tn = 1024
    return pl.pallas_call(
        _mod_body,
        grid=(depth, n6 // tn),
        in_specs=[pl.BlockSpec((rows, d), lambda l, j: (0, 0)),
                  pl.BlockSpec((None, d, tn), lambda l, j: (l, 0, j)),
                  pl.BlockSpec((None, 1, tn), lambda l, j: (l, 0, j))],
        out_specs=pl.BlockSpec((None, rows, tn), lambda l, j: (l, 0, j)),
        out_shape=jax.ShapeDtypeStruct((depth, rows, n6), F32),
        compiler_params=_cparams("arbitrary", "arbitrary"),
        name="modulation",
    )(cc, mod_w, mod_b.reshape(depth, 1, n6))


def _rope_block(y, cos, sin, lo):
    partner = jnp.where(lo, pltpu.roll(y, LANES - 16, 1), pltpu.roll(y, 16, 1))
    return y * cos + partner * sin


def _inproj_body(h_ref, g_ref, sh_ref, sc_ref, cos_ref, sin_ref, w_ref, *out_refs, groups):
    u = _rms(h_ref[...]) * g_ref[...]
    u = u * (1.0 + sc_ref[...]) + sh_ref[...]
    ub = u.astype(BF16)
    cos = cos_ref[...]
    sin = sin_ref[...]
    lane = lax.broadcasted_iota(jnp.int32, cos.shape, 1)
    lo = (lane & 31) < 16
    for (start, width, rope, qscale), o_ref in zip(groups, out_refs):
        step = LANES if rope else min(width, 512)
        for a in range(0, width, step):
            y = _dot(ub, w_ref[:, start + a:start + a + step])
            if rope:
                y = _rope_block(y, cos, sin, lo)
            if qscale != 1.0:
                y = y * qscale
            o_ref[:, a:a + step] = y.astype(o_ref.dtype)


def _seg_map(ctx_tiles, off):
    return lambda b, t: (b, jnp.where(t + off >= ctx_tiles, 1, 0), 0, 0)


def _inproj_call(h, g, shift, scale, cos, sin, w, groups, dtypes, ctx_tiles):
    bsz, s, d = h.shape
    nt = s // TM
    seg = _seg_map(ctx_tiles, 0)
    in_specs = [pl.BlockSpec((None, TM, d), lambda b, t: (b, t, 0)),
                pl.BlockSpec((1, d), lambda b, t: (0, 0)),
                pl.BlockSpec((None, None, 1, d), seg),
                pl.BlockSpec((None, None, 1, d), seg),
                pl.BlockSpec((TM, LANES), lambda b, t: (t, 0)),
                pl.BlockSpec((TM, LANES), lambda b, t: (t, 0)),
                pl.BlockSpec(w.shape, lambda b, t: (0, 0))]
    out_specs = [pl.BlockSpec((None, TM, gr[1]), lambda b, t: (b, t, 0)) for gr in groups]
    out_shape = [jax.ShapeDtypeStruct((bsz, s, gr[1]), dt) for gr, dt in zip(groups, dtypes)]
    return pl.pallas_call(
        functools.partial(_inproj_body, groups=groups),
        grid=(bsz, nt), in_specs=in_specs, out_specs=out_specs, out_shape=out_shape,
        compiler_params=_cparams("parallel", "arbitrary"),
        name="in_proj",
    )(h, g, shift, scale, cos, sin, w)


def _halo_flags(t, nt, seg_starts):
    prev_ok = t >= 0
    nxt_ok = (t + 1) < nt
    for s0 in seg_starts:
        prev_ok = jnp.logical_and(prev_ok, t != s0)
        nxt_ok = jnp.logical_and(nxt_ok, (t + 1) != s0)
    return prev_ok, nxt_ok


def _conv_body(xm_ref, xp_ref, xn_ref, w_ref, b_ref, o_ref, scr, *, nt, seg_starts, taps):
    t = pl.program_id(1)
    prev_ok, nxt_ok = _halo_flags(t, nt, seg_starts)
    scr[0:HALO, :] = jnp.where(prev_ok, xp_ref[...], 0.0)
    scr[HALO:HALO + TM, :] = xm_ref[...]
    scr[HALO + TM:HALO + TM + HALO, :] = jnp.where(nxt_ok, xn_ref[...], 0.0)
    pad = taps // 2
    y = scr[pl.ds(HALO - pad, TM), :] * w_ref[0:1, :]
    for j in range(1, taps):
        y = y + scr[pl.ds(HALO - pad + j, TM), :] * w_ref[j:j + 1, :]
    o_ref[...] = _silu(y + b_ref[...]).astype(o_ref.dtype)


def _halo_specs(width, s, t_off):
    per = TM // HALO
    last = s // HALO - 1
    return [pl.BlockSpec((None, TM, width), lambda b, t: (b, t + t_off, 0)),
            pl.BlockSpec((None, HALO, width), lambda b, t: (b, jnp.maximum((t + t_off) * per - 1, 0), 0)),
            pl.BlockSpec((None, HALO, width), lambda b, t: (b, jnp.minimum((t + t_off + 1) * per, last), 0))]


def _conv_call(x, w, bias, seg_starts):
    bsz, s, c = x.shape
    nt = s // TM
    taps = w.shape[0]
    return pl.pallas_call(
        functools.partial(_conv_body, nt=nt, seg_starts=seg_starts, taps=taps),
        grid=(bsz, nt),
        in_specs=_halo_specs(c, s, 0) + [pl.BlockSpec((taps, c), lambda b, t: (0, 0)),
                                         pl.BlockSpec((1, c), lambda b, t: (0, 0))],
        out_specs=pl.BlockSpec((None, TM, c), lambda b, t: (b, t, 0)),
        out_shape=jax.ShapeDtypeStruct((bsz, s, c), F32),
        scratch_shapes=[pltpu.VMEM((TM + 2 * HALO, c), F32)],
        compiler_params=_cparams("parallel", "arbitrary"),
        name="short_conv",
    )(x, x, x, w, bias)


def _win_attn_body(sink_ref, q_ref, kp_ref, kc_ref, kn_ref, kx_ref, vp_ref, vc_ref, vn_ref, vx_ref,
                   o_ref, *, ctx_blocks, lat_blocks):
    i = pl.program_id(1)
    blk = CHUNK
    nwin = 3 * blk
    kcat = jnp.concatenate([kp_ref[...], kc_ref[...], kn_ref[...], kx_ref[...]], axis=0)
    vcat = jnp.concatenate([vp_ref[...], vc_ref[...], vn_ref[...], vx_ref[...]], axis=0)
    nkeys = kcat.shape[0]
    rows = A_GROUP * blk
    qi = lax.broadcasted_iota(jnp.int32, (rows, nkeys), 0) & (blk - 1)
    kj = lax.broadcasted_iota(jnp.int32, (rows, nkeys), 1)
    kpos = (i - ctx_blocks - 1) * blk + kj
    in_win = (kj < nwin) & (jnp.abs(kj - blk - qi) <= A_WINDOW) & (kpos >= 0) & (kpos < lat_blocks * blk)
    valid = (in_win & (i >= ctx_blocks)) | (kj >= nwin)
    lane = lax.broadcasted_iota(jnp.int32, (blk, LANES), 1)
    rowg = lax.broadcasted_iota(jnp.int32, (rows, 1), 0) // blk
    q = q_ref[...]
    outs = []
    for h in range(A_KV_HEADS):
        half = (lane >= h * HEAD_DIM) & (lane < (h + 1) * HEAD_DIM)
        qh = jnp.concatenate(
            [jnp.where(half, q[:, g * LANES:(g + 1) * LANES], jnp.zeros((), BF16)) for g in range(A_GROUP)], axis=0)
        s = jnp.where(valid, _dot_nt(qh, kcat), NEG)
        sink = jnp.zeros((rows, 1), F32)
        for g in range(A_GROUP):
            sink = jnp.where(rowg == g, sink_ref[h * A_GROUP + g], sink)
        m = jnp.maximum(jnp.max(s, axis=1, keepdims=True), sink)
        p = jnp.exp(s - m)
        den = jnp.sum(p, axis=1, keepdims=True) + jnp.exp(sink - m)
        outs.append(_dot(p.astype(BF16), vcat) / den)
    for g in range(A_GROUP):
        o_ref[:, g * LANES:(g + 1) * LANES] = jnp.where(
            lane < HEAD_DIM, outs[0][g * blk:(g + 1) * blk], outs[1][g * blk:(g + 1) * blk]).astype(o_ref.dtype)


def _win_attn_call(q, k, v, sink, ctx_len):
    bsz, s, qw = q.shape
    kw = k.shape[-1]
    blk = CHUNK
    nblk = s // blk
    cb = ctx_len // blk
    lb = nblk - cb
    clamp = lambda i: jnp.clip(i, cb, nblk - 1)
    kv_specs = [pl.BlockSpec((None, blk, kw), lambda b, i: (b, clamp(i - 1), 0)),
                pl.BlockSpec((None, blk, kw), lambda b, i: (b, clamp(i), 0)),
                pl.BlockSpec((None, blk, kw), lambda b, i: (b, clamp(i + 1), 0)),
                pl.BlockSpec((None, ctx_len, kw), lambda b, i: (b, 0, 0))]
    return pl.pallas_call(
        functools.partial(_win_attn_body, ctx_blocks=cb, lat_blocks=lb),
        grid=(bsz, nblk),
        in_specs=[pl.BlockSpec(memory_space=pltpu.SMEM),
                  pl.BlockSpec((None, blk, qw), lambda b, i: (b, i, 0))] + kv_specs + kv_specs,
        out_specs=pl.BlockSpec((None, blk, qw), lambda b, i: (b, i, 0)),
        out_shape=jax.ShapeDtypeStruct((bsz, s, qw), BF16),
        compiler_params=_cparams("parallel", "arbitrary"),
        name="window_attention",
    )(sink, q, k, k, k, k, v, v, v, v)


def _bwd_chunk(j, ctx_chunks, n_chunks):
    return jnp.where(j < ctx_chunks, ctx_chunks - 1 - j, n_chunks + ctx_chunks - 1 - j)


def _tri_masks(d):
    row = lax.broadcasted_iota(jnp.int32, (CHUNK, CHUNK), 0)
    col = lax.broadcasted_iota(jnp.int32, (CHUNK, CHUNK), 1)
    keep = (col <= row) if d == 0 else (col >= row)
    return keep, jnp.where(keep, 1.0, 0.0).astype(BF16)


def _mlstm_body(qkf_ref, vf_ref, gf_ref, qkb_ref, vb_ref, gb_ref, bias_ref, hf_ref, hb_ref, c_scr, m_scr):
    L = CHUNK
    dk = HEAD_DIM
    dv = LANES

    @pl.when(pl.program_id(1) == 0)
    def _():
        c_scr[...] = jnp.zeros_like(c_scr)
        m_scr[...] = jnp.zeros_like(m_scr)

    lane = lax.broadcasted_iota(jnp.int32, (L, LANES), 1)
    srow = lax.broadcasted_iota(jnp.int32, (LANES, 2 * dv), 0)
    ones_v = jnp.ones((L, dv), BF16)
    dirs = ((qkf_ref, vf_ref, gf_ref, hf_ref), (qkb_ref, vb_ref, gb_ref, hb_ref))
    for d, (qk_ref, v_ref, g_ref, o_ref) in enumerate(dirs):
        keep, tri = _tri_masks(d)
        gates = g_ref[...] + bias_ref[...]
        gates_t = gates.T[0:16, :]
        b_col = _cumsum_cols(tri, _log_sigmoid(gates))
        b_row = _cumsum_rows(_log_sigmoid(gates_t), tri)
        for h in range(B_HEADS):
            ci = d * B_HEADS + h
            cf = 2 * B_HEADS + ci
            r = ci
            li_c = gates[:, ci:ci + 1]
            li_r = gates_t[ci:ci + 1, :]
            bc = b_col[:, cf:cf + 1]
            br = b_row[cf:cf + 1, :]
            b_last = bc[L - 1:L, :] if d == 0 else bc[0:1, :]
            m_prev = m_scr[r:r + 1, 0:1]
            pair = (h // 2) * LANES
            half = (lane >= (h % 2) * dk) & (lane < (h % 2 + 1) * dk)
            q = jnp.where(half, qk_ref[:, pair:pair + LANES] * (dk ** -0.5), 0.0).astype(BF16)
            k = qk_ref[:, B_HEADS * dk + pair:B_HEADS * dk + pair + LANES].astype(BF16)
            v = v_ref[:, h * dv:(h + 1) * dv]
            dm = jnp.where(keep, bc - br + li_r, NEG)
            g_in = bc + m_prev
            m_t = jnp.maximum(jnp.max(dm, axis=1, keepdims=True), g_in)
            sm = (_dot_nt(q, k) * jnp.exp(dm - m_t)).astype(BF16)
            vaug = jnp.concatenate([v, ones_v], axis=1)
            c_prev = c_scr[r]
            nd = _dot(sm, vaug) + jnp.exp(g_in - m_t) * _dot(q, c_prev.astype(BF16))
            den = jnp.maximum(jnp.abs(nd[:, dv:]), jnp.exp(-m_t))
            o_ref[:, h * dv:(h + 1) * dv] = nd[:, :dv] / den
            w_c = b_last - bc + li_c
            m_loc = jnp.max(w_c, axis=0, keepdims=True)
            e = jnp.exp(w_c - m_loc)
            ev = jnp.concatenate([e * v.astype(F32), jnp.broadcast_to(e, (L, dv))], axis=1).astype(BF16)
            c_loc = _dot_tn(k, ev)
            own = (srow >= (h % 2) * dk) & (srow < (h % 2 + 1) * dk)
            m_new = jnp.maximum(b_last + m_prev, m_loc)
            c_scr[r] = jnp.exp(b_last + m_prev - m_new) * c_prev + jnp.where(own, jnp.exp(m_loc - m_new) * c_loc, 0.0)
            m_scr[r:r + 1, :] = jnp.broadcast_to(m_new, (1, LANES))


def _scan_specs(width, ctx_chunks, n_chunks):
    fwd = pl.BlockSpec((None, CHUNK, width), lambda b, j: (b, j, 0))
    bwd = pl.BlockSpec((None, CHUNK, width), lambda b, j: (b, _bwd_chunk(j, ctx_chunks, n_chunks), 0))
    return fwd, bwd


def _mlstm_call(qk, v, gates, bias, ctx_len):
    bsz, s, w = v.shape
    nc = s // CHUNK
    cc = ctx_len // CHUNK
    qf, qb = _scan_specs(qk.shape[-1], cc, nc)
    vf, vb = _scan_specs(w, cc, nc)
    gf, gb = _scan_specs(LANES, cc, nc)
    return pl.pallas_call(
        _mlstm_body,
        grid=(bsz, nc),
        in_specs=[qf, vf, gf, qb, vb, gb, pl.BlockSpec((1, LANES), lambda b, j: (0, 0))],
        out_specs=[vf, vb],
        out_shape=[jax.ShapeDtypeStruct((bsz, s, w), F32)] * 2,
        scratch_shapes=[pltpu.VMEM((2 * B_HEADS, LANES, 2 * LANES), F32),
                        pltpu.VMEM((2 * B_HEADS, LANES), F32)],
        compiler_params=_cparams("parallel", "arbitrary"),
        name="mlstm_scan",
    )(qk, v, gates, qk, v, gates, bias)


def _ssd_body(xf_ref, dtf_ref, xb_ref, dtb_ref, dtbias_ref, nega_ref, skip_ref, yf_ref, yb_ref, s_scr):
    L = CHUNK
    hd = HEAD_DIM
    inner = 4 * D_GROUPS * hd
    hpg = 4

    @pl.when(pl.program_id(1) == 0)
    def _():
        s_scr[...] = jnp.zeros_like(s_scr)

    lane = lax.broadcasted_iota(jnp.int32, (L, LANES), 1)
    lo = lane < hd
    lo1 = lo[0:1, :]
    dirs = ((xf_ref, dtf_ref, yf_ref), (xb_ref, dtb_ref, yb_ref))
    for d, (x_ref, dt_ref, y_ref) in enumerate(dirs):
        keep, tri = _tri_masks(d)
        dt = _softplus(dt_ref[...] + dtbias_ref[...])
        a = dt * nega_ref[...]
        dt_t = dt.T[0:16, :]
        a_t = a.T[0:16, :]
        ac_col = _cumsum_cols(tri, a)
        ac_row = _cumsum_rows(a_t, tri)
        for g in range(D_GROUPS):
            bg = x_ref[:, inner + g * LANES:inner + (g + 1) * LANES].astype(BF16)
            cg = x_ref[:, inner + (D_GROUPS + g) * LANES:inner + (D_GROUPS + g + 1) * LANES].astype(BF16)
            cb = _dot_nt(cg, bg)
            for pr in range(2):
                c0 = g * hpg * hd + pr * LANES
                xp = x_ref[:, c0:c0 + LANES]
                xpb = xp.astype(BF16)
                ys, eas, wsts, als = [], [], [], []
                for hh in range(2):
                    col = d * D_GROUPS * hpg + g * hpg + pr * 2 + hh
                    acc = ac_col[:, col:col + 1]
                    acr = ac_row[col:col + 1, :]
                    seg = jnp.where(keep, acc - acr, NEG)
                    mix = (cb * jnp.exp(seg) * dt_t[col:col + 1, :]).astype(BF16)
                    ys.append(_dot(mix, xpb))
                    a_last = acc[L - 1:L, :] if d == 0 else acc[0:1, :]
                    eas.append(jnp.exp(acc))
                    wsts.append(jnp.exp(a_last - acc) * dt[:, col:col + 1])
                    als.append(jnp.exp(a_last))
                sidx = d * 2 * D_GROUPS + g * 2 + pr
                st = s_scr[sidx]
                y = jnp.where(lo, ys[0], ys[1]) + _dot(cg, st.astype(BF16)) * jnp.where(lo, eas[0], eas[1])
                if d == 0:
                    y = y + skip_ref[:, c0:c0 + LANES] * xp
                y_ref[:, c0:c0 + LANES] = y
                xw = (xp * jnp.where(lo, wsts[0], wsts[1])).astype(BF16)
                s_scr[sidx] = jnp.where(lo1, als[0], als[1]) * st + _dot_tn(bg, xw)


def _ssd_call(xbc, dt, dt_bias, neg_a, skip, ctx_len):
    bsz, s, w = xbc.shape
    nc = s // CHUNK
    cc = ctx_len // CHUNK
    inner = skip.shape[-1]
    xf, xb = _scan_specs(w, cc, nc)
    df, db = _scan_specs(LANES, cc, nc)
    yf, yb = _scan_specs(inner, cc, nc)
    vec = lambda n: pl.BlockSpec((1, n), lambda b, j: (0, 0))
    return pl.pallas_call(
        _ssd_body,
        grid=(bsz, nc),
        in_specs=[xf, df, xb, db, vec(LANES), vec(LANES), vec(inner)],
        out_specs=[yf, yb],
        out_shape=[jax.ShapeDtypeStruct((bsz, s, inner), F32)] * 2,
        scratch_shapes=[pltpu.VMEM((4 * D_GROUPS, LANES, LANES), F32)],
        compiler_params=_cparams("parallel", "arbitrary"),
        name="ssd_scan",
    )(xbc, dt, xbc, dt, dt_bias, neg_a, skip)


def _diff_attn_body(lam_ref, q_ref, k_ref, v_ref, o_ref, *, lam_init):
    lv = lam_ref[...]
    lam = (jnp.exp(jnp.sum(lv[0:1] * lv[1:2], axis=1, keepdims=True))
           - jnp.exp(jnp.sum(lv[2:3] * lv[3:4], axis=1, keepdims=True)) + lam_init)
    tq = q_ref.shape[0]
    lane = lax.broadcasted_iota(jnp.int32, (tq, LANES), 1)
    nheads = q_ref.shape[1] // LANES
    for h in range(nheads):
        q = q_ref[:, h * LANES:(h + 1) * LANES]
        k = k_ref[:, h * LANES:(h + 1) * LANES]
        probs = []
        for m in range(2):
            qm = jnp.where((lane >= m * HEAD_DIM) & (lane < (m + 1) * HEAD_DIM), q, jnp.zeros((), BF16))
            s = _dot_nt(qm, k)
            p = jnp.exp(s - jnp.max(s, axis=1, keepdims=True))
            probs.append(p / jnp.sum(p, axis=1, keepdims=True))
        a = (probs[0] - lam * probs[1]).astype(BF16)
        o_ref[:, h * LANES:(h + 1) * LANES] = _dot(a, v_ref[:, h * LANES:(h + 1) * LANES])


def _diff_attn_call(q, k, v, lam_vecs, lam_init, ctx_len):
    bsz, s, w = q.shape
    t = s - ctx_len
    off = ctx_len // TQ
    return pl.pallas_call(
        functools.partial(_diff_attn_body, lam_init=lam_init),
        grid=(bsz, t // TQ),
        in_specs=[pl.BlockSpec(lam_vecs.shape, lambda b, i: (0, 0)),
                  pl.BlockSpec((None, TQ, w), lambda b, i: (b, i + off, 0)),
                  pl.BlockSpec((None, s, w), lambda b, i: (b, 0, 0)),
                  pl.BlockSpec((None, s, w), lambda b, i: (b, 0, 0))],
        out_specs=pl.BlockSpec((None, TQ, w), lambda b, i: (b, i, 0)),
        out_shape=jax.ShapeDtypeStruct((bsz, t, w), F32),
        compiler_params=_cparams("parallel", "arbitrary"),
        name="diff_attention",
    )(lam_vecs, q, k, v)


def _group_rms(x, width):
    return jnp.concatenate([_rms(x[:, a:a + width]) for a in range(0, x.shape[1], width)], axis=1)


def _residual_out(y, h_ref, g_ref, gate_ref, out_ref):
    out_ref[...] = h_ref[...] + gate_ref[...] * (_rms(y) * g_ref[...])


def _outproj_even_body(ya_ref, hf_ref, hb_ref, og_ref, ng_ref, w_ref, h_ref, g_ref, gate_ref, out_ref):
    half = ya_ref.shape[1]
    hn = _group_rms(hf_ref[...] + hb_ref[...], LANES) * ng_ref[...] * jax.nn.sigmoid(og_ref[...])
    y = _dot(ya_ref[...], w_ref[0:half, :]) + _dot(hn.astype(BF16), w_ref[half:, :])
    _residual_out(y, h_ref, g_ref, gate_ref, out_ref)


def _outproj_odd_body(oa_ref, yf_ref, yb_ref, z_ref, cg_ref, dg_ref, w_ref, h_ref, g_ref, gate_ref, out_ref,
                      *, out_scale):
    half = oa_ref.shape[1]
    on = _group_rms(oa_ref[...], LANES) * cg_ref[...] * out_scale
    yz = (yf_ref[...] + yb_ref[...]) * _silu(z_ref[...])
    sn = _group_rms(yz, half // D_GROUPS) * dg_ref[...]
    y = _dot(on.astype(BF16), w_ref[0:half, :]) + _dot(sn.astype(BF16), w_ref[half:, :])
    _residual_out(y, h_ref, g_ref, gate_ref, out_ref)


def _outproj_call(body, acts, act_offs, vecs, w, h, g, gate, ctx_tiles, t_off, n_t):
    bsz, _, d = h.shape
    seg = _seg_map(ctx_tiles, t_off)
    out_rows = n_t * TM

    def act_spec(arr, off):
        return pl.BlockSpec((None, TM, arr.shape[-1]), lambda b, t: (b, t + off, 0))

    in_specs = ([act_spec(a_, o_) for a_, o_ in zip(acts, act_offs)]
                + [pl.BlockSpec(v_.shape, lambda b, t: (0, 0)) for v_ in vecs]
                + [pl.BlockSpec(w.shape, lambda b, t: (0, 0)),
                   pl.BlockSpec((None, TM, d), lambda b, t: (b, t + t_off, 0)),
                   pl.BlockSpec((1, d), lambda b, t: (0, 0)),
                   pl.BlockSpec((None, None, 1, d), seg)])
    return pl.pallas_call(
        body, grid=(bsz, n_t), in_specs=in_specs,
        out_specs=pl.BlockSpec((None, TM, d), lambda b, t: (b, t, 0)),
        out_shape=jax.ShapeDtypeStruct((bsz, out_rows, d), F32),
        compiler_params=_cparams("parallel", "arbitrary"),
        name="out_proj",
    )(*acts, *vecs, w, h, g, gate)


def _ffn_body(hm_ref, hp_ref, hn_ref, g2_ref, sh_ref, sc_ref, wg_ref, wu_ref, cw_ref, wd_ref, g3_ref, gate_ref,
              out_ref, gscr, acc, *, nt, seg_starts):
    t = pl.program_id(1)
    prev_ok, nxt_ok = _halo_flags(t, nt, seg_starts)

    def normmod(h):
        return (_rms(h) * g2_ref[...]) * (1.0 + sc_ref[...]) + sh_ref[...]

    um = normmod(hm_ref[...])
    up = jnp.where(prev_ok, normmod(hp_ref[...]), 0.0)
    un = jnp.where(nxt_ok, normmod(hn_ref[...]), 0.0)
    ucat = jnp.concatenate([up, um, un], axis=0).astype(BF16)
    umb = um.astype(BF16)
    acc[...] = jnp.zeros_like(acc)

    def step(c, carry):
        gscr[...] = _dot(ucat, wg_ref[c])
        cw = cw_ref[c]
        gc = (gscr[pl.ds(HALO - 1, TM), :] * cw[0:1, :] + gscr[pl.ds(HALO, TM), :] * cw[1:2, :]
              + gscr[pl.ds(HALO + 1, TM), :] * cw[2:3, :])
        hid = (_silu(gc) * _dot(umb, wu_ref[c])).astype(BF16)
        acc[...] += _dot(hid, wd_ref[c])
        return carry

    lax.fori_loop(0, wg_ref.shape[0], step, 0)
    _residual_out(acc[...], hm_ref, g3_ref, gate_ref, out_ref)


def _ffn_call(h, g2, shift, scale, wg, wu, cw, wd, g3, gate, ctx_tiles, seg_off, seg_starts):
    bsz, s, d = h.shape
    n_t = s // TM
    const3 = lambda arr: pl.BlockSpec(arr.shape, lambda b, t: (0, 0, 0))
    vec = pl.BlockSpec((1, d), lambda b, t: (0, 0))
    mod = pl.BlockSpec((None, None, 1, d), _seg_map(ctx_tiles, seg_off))
    return pl.pallas_call(
        functools.partial(_ffn_body, nt=n_t, seg_starts=seg_starts),
        grid=(bsz, n_t),
        in_specs=_halo_specs(d, s, 0) + [vec, mod, mod, const3(wg), const3(wu), const3(cw), const3(wd), vec, mod],
        out_specs=pl.BlockSpec((None, TM, d), lambda b, t: (b, t, 0)),
        out_shape=jax.ShapeDtypeStruct((bsz, s, d), F32),
        scratch_shapes=[pltpu.VMEM((TM + 2 * HALO, FFN_CHUNK), F32), pltpu.VMEM((TM, d), F32)],
        compiler_params=_cparams("parallel", "arbitrary"),
        name="conv_ffn",
    )(h, h, h, g2, shift, scale, wg, wu, cw, wd, g3, gate)


def _rope_tables(ctx_len, t):
    pos = jnp.arange(t)
    row = (pos // GRID_W).astype(F32)
    col = (pos % GRID_W).astype(F32)
    nf = HEAD_DIM // 4
    inv = ROPE_BASE ** (-jnp.arange(nf, dtype=F32) / nf)
    ar = row[:, None] * inv
    ac = col[:, None] * inv
    cos = jnp.concatenate([jnp.cos(ar), jnp.cos(ar), jnp.cos(ac), jnp.cos(ac)], axis=1)
    sin = jnp.concatenate([-jnp.sin(ar), jnp.sin(ar), -jnp.sin(ac), jnp.sin(ac)], axis=1)
    cos = jnp.concatenate([jnp.ones((ctx_len, HEAD_DIM), F32), cos], axis=0)
    sin = jnp.concatenate([jnp.zeros((ctx_len, HEAD_DIM), F32), sin], axis=0)
    return jnp.tile(cos, (1, LANES // HEAD_DIM)), jnp.tile(sin, (1, LANES // HEAD_DIM))


def _pad_cols(w, n):
    return jnp.pad(w, ((0, 0), (0, n - w.shape[1])))


def _gqa_perm():
    idx = [(h * A_GROUP + g) * HEAD_DIM + dd
           for g in range(A_GROUP) for h in range(A_KV_HEADS) for dd in range(HEAD_DIM)]
    return jnp.array(idx, dtype=jnp.int32)


def _ffn_weights(w_gate, w_up, conv, w_down):
    d, hid = w_gate.shape
    nchunk = hid // FFN_CHUNK
    wg = w_gate.astype(BF16).reshape(d, nchunk, FFN_CHUNK).transpose(1, 0, 2)
    wu = w_up.astype(BF16).reshape(d, nchunk, FFN_CHUNK).transpose(1, 0, 2)
    cw = conv.reshape(conv.shape[0], nchunk, FFN_CHUNK).transpose(1, 0, 2)
    wd = w_down.astype(BF16).reshape(nchunk, FFN_CHUNK, d)
    return wg, wu, cw, wd


def kernel(x, c, ctx, c_ctx, mod_w, mod_b, norm_g, ffn_w_gate, ffn_w_up, ffn_conv, ffn_w_down, ev_w_in, ev_w_out, a_sink, b_conv, b_gate_b, b_norm_g, od_w_in, od_w_out, c_lambda, c_norm_g, d_conv, d_conv_b, d_dt_bias, d_a_log, d_skip, d_norm_g):
    bsz, t, d = x.shape
    ctx_len = ctx.shape[1]
    depth = mod_w.shape[0]
    s = ctx_len + t
    half = d // 2
    assert ctx_len % TM == 0 and t % TM == 0 and t % GRID_W == 0 and ffn_w_gate.shape[-1] % FFN_CHUNK == 0
    n_t = s // TM
    ctx_tiles = ctx_len // TM

    mod_rows = -(-(bsz + 1) // HALO) * HALO
    cc = jnp.concatenate([c, c_ctx[None, :], jnp.zeros((mod_rows - bsz - 1, d), F32)], axis=0)
    mods = _mod_call(cc, mod_w, mod_b).reshape(depth, mod_rows, 6, d)

    def layer_mods(l):
        lat = mods[l, :bsz]
        cx = jnp.broadcast_to(mods[l, bsz][None], (bsz, 6, d))
        m = jnp.stack([cx, lat], axis=1)
        return [m[:, :, i][:, :, None, :] for i in range(6)]

    cos, sin = _rope_tables(ctx_len, t)
    h = jnp.concatenate([ctx, x], axis=1)

    for l in range(depth):
        last = l == depth - 1
        j = l // 2
        m = layer_mods(l)
        g = norm_g[l]
        wg, wu, cw, wd = _ffn_weights(ffn_w_gate[l], ffn_w_up[l], ffn_conv[l], ffn_w_down[l])
        if l % 2 == 0:
            w_in = ev_w_in[j]
            perm = _gqa_perm()
            akv = A_KV_HEADS * HEAD_DIM
            o1 = half + 2 * akv
            w_cat = jnp.concatenate([w_in[:, :half][:, perm], w_in[:, half:o1 + 3 * half],
                                     _pad_cols(w_in[:, o1 + 3 * half:], LANES)], axis=1).astype(BF16)
            groups = ((0, half, True, HEAD_DIM ** -0.5), (half, akv, True, 1.0), (half + akv, akv, False, 1.0),
                      (o1, half, False, 1.0), (o1 + half, half, False, 1.0), (o1 + 2 * half, half, False, 1.0),
                      (o1 + 3 * half, LANES, False, 1.0))
            q, k, v, mqk, mv, mo, gates = _inproj_call(
                h, g[0:1], m[0], m[1], cos, sin, w_cat, groups, (BF16, BF16, BF16, F32, BF16, F32, F32), ctx_tiles)
            ya = _win_attn_call(q, k, v, a_sink[j], ctx_len)
            qkc = _conv_call(mqk, b_conv[j], jnp.zeros((1, half), F32), (0, ctx_tiles))
            hf, hb = _mlstm_call(qkc, mv, gates, _pad_cols(b_gate_b[j][None, :], LANES), ctx_len)
            w_out = ev_w_out[j]
            w_out = jnp.concatenate([w_out[:half][perm], w_out[half:]], axis=0).astype(BF16)
            acts, vecs, body = (ya, hf, hb, mo), (b_norm_g[j][None, :],), _outproj_even_body
        else:
            w_in = od_w_in[j]
            xbc_w = 2 * half
            w_cat = jnp.concatenate([w_in[:, :4 * half + xbc_w],
                                     _pad_cols(w_in[:, 4 * half + xbc_w:], LANES)], axis=1).astype(BF16)
            groups = ((0, half, True, HEAD_DIM ** -0.5), (half, half, True, 1.0), (2 * half, half, False, 1.0),
                      (3 * half, half, False, 1.0), (4 * half, xbc_w, False, 1.0), (4 * half + xbc_w, LANES, False, 1.0))
            q, k, v, z, xbc, dt = _inproj_call(
                h, g[0:1], m[0], m[1], cos, sin, w_cat, groups, (BF16, BF16, BF16, F32, F32, F32), ctx_tiles)
            lam_init = 0.8 - 0.6 * math.exp(-LAM_DEPTH_RATE * l)
            oa = _diff_attn_call(q, k, v, c_lambda[j], lam_init, ctx_len)
            xbcc = _conv_call(xbc, d_conv[j], d_conv_b[j][None, :], (0, ctx_tiles))
            dt_bias = _pad_cols(d_dt_bias[j].reshape(1, -1), LANES)
            neg_a = _pad_cols(-jnp.exp(d_a_log[j].astype(F32)).reshape(1, -1), LANES)
            skip = jnp.repeat(d_skip[j].astype(F32), HEAD_DIM)[None, :]
            yf, yb = _ssd_call(xbcc, dt, dt_bias, neg_a, skip, ctx_len)
            w_out = od_w_out[j].astype(BF16)
            acts, vecs = (oa, yf, yb, z), (c_norm_g[j][None, :], d_norm_g[j][None, :])
            body = functools.partial(_outproj_odd_body, out_scale=1.0 - lam_init)
        t_off = ctx_tiles if last else 0
        if l % 2 == 0:
            act_offs = (t_off,) * 4
        else:
            assert last, "differential attention is only computed for latent queries"
            act_offs = (0,) + (t_off,) * 3
        h_mid = _outproj_call(body, acts, act_offs, vecs, w_out, h, g[1:2], m[2], ctx_tiles, t_off, n_t - t_off)
        seg_starts = (0,) if last else (0, ctx_tiles)
        h = _ffn_call(h_mid, g[2:3], m[3], m[4], wg, wu, cw, wd, g[3:4], m[5], ctx_tiles, t_off, seg_starts)
    return h
```

```python
import functools
import math

import jax
import jax.numpy as jnp
from jax import lax
from jax.experimental import pallas as pl
from jax.experimental.pallas import tpu as pltpu

F32 = jnp.float32
BF16 = jnp.bfloat16

EPS = 1e-6
ROPE_BASE = 10000.0
GRID_W = 64
HEAD_DIM = 64
A_KV_HEADS = 2
A_GROUP = 4
A_WINDOW = 128
B_HEADS = 4
D_GROUPS = 2
LAM_DEPTH_RATE = 0.3

LANES = 128
HALO = 8
TM = 256
CHUNK = 128
FFN_CHUNK = 256
TQ = 256
NEG = -1e30
LOG2E = math.log2(math.e)
VMEM_LIMIT = 56 * 1024 * 1024


def _cparams(*sem):
    return pltpu.CompilerParams(dimension_semantics=sem, vmem_limit_bytes=VMEM_LIMIT)


def _silu(x):
    return x * jax.nn.sigmoid(x)


def _softplus(x):
    return jnp.maximum(x, 0.0) + jnp.log1p(jnp.exp(-jnp.abs(x)))


def _log_sigmoid(x):
    return -_softplus(-x)


def _rms(x):
    return x * lax.rsqrt(jnp.mean(x * x, axis=-1, keepdims=True) + EPS)


def _dot(a, b):
    return jnp.dot(a, b, preferred_element_type=F32)


def _dot_nt(a, b):
    return lax.dot_general(a, b, (((1,), (1,)), ((), ())), preferred_element_type=F32)


def _dot_tn(a, b):
    return lax.dot_general(a, b, (((0,), (0,)), ((), ())), preferred_element_type=F32)


def _split3(x):
    x1 = x.astype(BF16)
    r1 = x - x1.astype(F32)
    x2 = r1.astype(BF16)
    x3 = (r1 - x2.astype(F32)).astype(BF16)
    return x1, x2, x3


def _cumsum_cols(tri, x):
    x1, x2, x3 = _split3(x)
    return _dot(tri, x1) + _dot(tri, x2) + _dot(tri, x3)


def _cumsum_rows(x, tri):
    x1, x2, x3 = _split3(x)
    return _dot_nt(x1, tri) + _dot_nt(x2, tri) + _dot_nt(x3, tri)


def _mod_body(c_ref, w_ref, b_ref, o_ref):
    a = _silu(c_ref[...]).astype(BF16)
    o_ref[...] = _dot(a, w_ref[...].astype(BF16)) + b_ref[...]


def _mod_call(cc, mod_w, mod_b):
    depth, d, n6 = mod_w.shape
    rows = cc.shape[0]
    tn = 1024
    return pl.pallas_call(
        _mod_body,
        grid=(depth, n6 // tn),
        in_specs=[pl.BlockSpec((rows, d), lambda l, j: (0, 0)),
                  pl.BlockSpec((None, d, tn), lambda l, j: (l, 0, j)),
                  pl.BlockSpec((None, 1, tn), lambda l, j: (l, 0, j))],
        out_specs=pl.BlockSpec((None, rows, tn), lambda l, j: (l, 0, j)),
        out_shape=jax.ShapeDtypeStruct((depth, rows, n6), F32),
        compiler_params=_cparams("arbitrary", "arbitrary"),
        name="modulation",
    )(cc, mod_w, mod_b.reshape(depth, 1, n6))


def _rope_block(y, cos, sin, lo):
    partner = jnp.where(lo, pltpu.roll(y, LANES - 16, 1), pltpu.roll(y, 16, 1))
    return y * cos + partner * sin


def _inproj_body(h_ref, g_ref, sh_ref, sc_ref, cos_ref, sin_ref, w_ref, *out_refs, groups):
    u = _rms(h_ref[...]) * g_ref[...]
    u = u * (1.0 + sc_ref[...]) + sh_ref[...]
    ub = u.astype(BF16)
    cos = cos_ref[...]
    sin = sin_ref[...]
    lane = lax.broadcasted_iota(jnp.int32, cos.shape, 1)
    lo = (lane & 31) < 16
    for (start, width, rope, qscale), o_ref in zip(groups, out_refs):
        step = LANES if rope else min(width, 512)
        for a in range(0, width, step):
            y = _dot(ub, w_ref[:, start + a:start + a + step])
            if rope:
                y = _rope_block(y, cos, sin, lo)
            if qscale != 1.0:
                y = y * qscale
            o_ref[:, a:a + step] = y.astype(o_ref.dtype)


def _seg_map(ctx_tiles, off):
    return lambda b, t: (b, jnp.where(t + off >= ctx_tiles, 1, 0), 0, 0)


def _inproj_call(h, g, shift, scale, cos, sin, w, groups, dtypes, ctx_tiles):
    bsz, s, d = h.shape
    nt = s // TM
    seg = _seg_map(ctx_tiles, 0)
    in_specs = [pl.BlockSpec((None, TM, d), lambda b, t: (b, t, 0)),
                pl.BlockSpec((1, d), lambda b, t: (0, 0)),
                pl.BlockSpec((None, None, 1, d), seg),
                pl.BlockSpec((None, None, 1, d), seg),
                pl.BlockSpec((TM, LANES), lambda b, t: (t, 0)),
                pl.BlockSpec((TM, LANES), lambda b, t: (t, 0)),
                pl.BlockSpec(w.shape, lambda b, t: (0, 0))]
    out_specs = [pl.BlockSpec((None, TM, gr[1]), lambda b, t: (b, t, 0)) for gr in groups]
    out_shape = [jax.ShapeDtypeStruct((bsz, s, gr[1]), dt) for gr, dt in zip(groups, dtypes)]
    return pl.pallas_call(
        functools.partial(_inproj_body, groups=groups),
        grid=(bsz, nt), in_specs=in_specs, out_specs=out_specs, out_shape=out_shape,
        compiler_params=_cparams("parallel", "arbitrary"),
        name="in_proj",
    )(h, g, shift, scale, cos, sin, w)


def _halo_flags(t, nt, seg_starts):
    prev_ok = t >= 0
    nxt_ok = (t + 1) < nt
    for s0 in seg_starts:
        prev_ok = jnp.logical_and(prev_ok, t != s0)
        nxt_ok = jnp.logical_and(nxt_ok, (t + 1) != s0)
    return prev_ok, nxt_ok


def _conv_body(xm_ref, xp_ref, xn_ref, w_ref, b_ref, o_ref, scr, *, nt, seg_starts, taps):
    t = pl.program_id(1)
    prev_ok, nxt_ok = _halo_flags(t, nt, seg_starts)
    scr[0:HALO, :] = jnp.where(prev_ok, xp_ref[...], 0.0)
    scr[HALO:HALO + TM, :] = xm_ref[...]
    scr[HALO + TM:HALO + TM + HALO, :] = jnp.where(nxt_ok, xn_ref[...], 0.0)
    pad = taps // 2
    y = scr[pl.ds(HALO - pad, TM), :] * w_ref[0:1, :]
    for j in range(1, taps):
        y = y + scr[pl.ds(HALO - pad + j, TM), :] * w_ref[j:j + 1, :]
    o_ref[...] = _silu(y + b_ref[...]).astype(o_ref.dtype)


def _halo_specs(width, s, t_off):
    per = TM // HALO
    last = s // HALO - 1
    return [pl.BlockSpec((None, TM, width), lambda b, t: (b, t + t_off, 0)),
            pl.BlockSpec((None, HALO, width), lambda b, t: (b, jnp.maximum((t + t_off) * per - 1, 0), 0)),
            pl.BlockSpec((None, HALO, width), lambda b, t: (b, jnp.minimum((t + t_off + 1) * per, last), 0))]


def _conv_call(x, w, bias, seg_starts):
    bsz, s, c = x.shape
    nt = s // TM
    taps = w.shape[0]
    return pl.pallas_call(
        functools.partial(_conv_body, nt=nt, seg_starts=seg_starts, taps=taps),
        grid=(bsz, nt),
        in_specs=_halo_specs(c, s, 0) + [pl.BlockSpec((taps, c), lambda b, t: (0, 0)),
                                         pl.BlockSpec((1, c), lambda b, t: (0, 0))],
        out_specs=pl.BlockSpec((None, TM, c), lambda b, t: (b, t, 0)),
        out_shape=jax.ShapeDtypeStruct((bsz, s, c), F32),
        scratch_shapes=[pltpu.VMEM((TM + 2 * HALO, c), F32)],
        compiler_params=_cparams("parallel", "arbitrary"),
        name="short_conv",
    )(x, x, x, w, bias)


def _win_attn_body(sink_ref, q_ref, kp_ref, kc_ref, kn_ref, kx_ref, vp_ref, vc_ref, vn_ref, vx_ref,
                   o_ref, *, ctx_blocks, lat_blocks):
    i = pl.program_id(1)
    blk = CHUNK
    nwin = 3 * blk
    kcat = jnp.concatenate([kp_ref[...], kc_ref[...], kn_ref[...], kx_ref[...]], axis=0)
    vcat = jnp.concatenate([vp_ref[...], vc_ref[...], vn_ref[...], vx_ref[...]], axis=0)
    nkeys = kcat.shape[0]
    rows = A_GROUP * blk
    qi = lax.broadcasted_iota(jnp.int32, (rows, nkeys), 0) & (blk - 1)
    kj = lax.broadcasted_iota(jnp.int32, (rows, nkeys), 1)
    kpos = (i - ctx_blocks - 1) * blk + kj
    in_win = (kj < nwin) & (jnp.abs(kj - blk - qi) <= A_WINDOW) & (kpos >= 0) & (kpos < lat_blocks * blk)
    valid = (in_win & (i >= ctx_blocks)) | (kj >= nwin)
    lane = lax.broadcasted_iota(jnp.int32, (blk, LANES), 1)
    rowg = lax.broadcasted_iota(jnp.int32, (rows, 1), 0) // blk
    q = q_ref[...]
    outs = []
    for h in range(A_KV_HEADS):
        half = (lane >= h * HEAD_DIM) & (lane < (h + 1) * HEAD_DIM)
        qh = jnp.concatenate(
            [jnp.where(half, q[:, g * LANES:(g + 1) * LANES], jnp.zeros((), BF16)) for g in range(A_GROUP)], axis=0)
        s = jnp.where(valid, _dot_nt(qh, kcat), NEG)
        sink = jnp.zeros((rows, 1), F32)
        for g in range(A_GROUP):
            sink = jnp.where(rowg == g, sink_ref[h * A_GROUP + g], sink)
        m = jnp.maximum(jnp.max(s, axis=1, keepdims=True), sink)
        p = jnp.exp(s - m)
        den = jnp.sum(p, axis=1, keepdims=True) + jnp.exp(sink - m)
        outs.append(_dot(p.astype(BF16), vcat) / den)
    for g in range(A_GROUP):
        o_ref[:, g * LANES:(g + 1) * LANES] = jnp.where(
            lane < HEAD_DIM, outs[0][g * blk:(g + 1) * blk], outs[1][g * blk:(g + 1) * blk]).astype(o_ref.dtype)


def _win_attn_call(q, k, v, sink, ctx_len):
    bsz, s, qw = q.shape
    kw = k.shape[-1]
    blk = CHUNK
    nblk = s // blk
    cb = ctx_len // blk
    lb = nblk - cb
    clamp = lambda i: jnp.clip(i, cb, nblk - 1)
    kv_specs = [pl.BlockSpec((None, blk, kw), lambda b, i: (b, clamp(i - 1), 0)),
                pl.BlockSpec((None, blk, kw), lambda b, i: (b, clamp(i), 0)),
                pl.BlockSpec((None, blk, kw), lambda b, i: (b, clamp(i + 1), 0)),
                pl.BlockSpec((None, ctx_len, kw), lambda b, i: (b, 0, 0))]
    return pl.pallas_call(
        functools.partial(_win_attn_body, ctx_blocks=cb, lat_blocks=lb),
        grid=(bsz, nblk),
        in_specs=[pl.BlockSpec(memory_space=pltpu.SMEM),
                  pl.BlockSpec((None, blk, qw), lambda b, i: (b, i, 0))] + kv_specs + kv_specs,
        out_specs=pl.BlockSpec((None, blk, qw), lambda b, i: (b, i, 0)),
        out_shape=jax.ShapeDtypeStruct((bsz, s, qw), BF16),
        compiler_params=_cparams("parallel", "arbitrary"),
        name="window_attention",
    )(sink, q, k, k, k, k, v, v, v, v)


def _bwd_chunk(j, ctx_chunks, n_chunks):
    return jnp.where(j < ctx_chunks, ctx_chunks - 1 - j, n_chunks + ctx_chunks - 1 - j)


def _tri_masks(d):
    row = lax.broadcasted_iota(jnp.int32, (CHUNK, CHUNK), 0)
    col = lax.broadcasted_iota(jnp.int32, (CHUNK, CHUNK), 1)
    keep = (col <= row) if d == 0 else (col >= row)
    return keep, jnp.where(keep, 1.0, 0.0).astype(BF16)


def _mlstm_body(qkf_ref, vf_ref, gf_ref, qkb_ref, vb_ref, gb_ref, bias_ref, hf_ref, hb_ref, c_scr, m_scr):
    L = CHUNK
    dk = HEAD_DIM
    dv = LANES

    @pl.when(pl.program_id(1) == 0)
    def _():
        c_scr[...] = jnp.zeros_like(c_scr)
        m_scr[...] = jnp.zeros_like(m_scr)

    lane = lax.broadcasted_iota(jnp.int32, (L, LANES), 1)
    srow = lax.broadcasted_iota(jnp.int32, (LANES, 2 * dv), 0)
    ones_v = jnp.ones((L, dv), BF16)
    dirs = ((qkf_ref, vf_ref, gf_ref, hf_ref), (qkb_ref, vb_ref, gb_ref, hb_ref))
    for d, (qk_ref, v_ref, g_ref, o_ref) in enumerate(dirs):
        keep, tri = _tri_masks(d)
        gates = g_ref[...] + bias_ref[...]
        gates_t = gates.T[0:16, :]
        b_col = _cumsum_cols(tri, _log_sigmoid(gates))
        b_row = _cumsum_rows(_log_sigmoid(gates_t), tri)
        for h in range(B_HEADS):
            ci = d * B_HEADS + h
            cf = 2 * B_HEADS + ci
            r = ci
            li_c = gates[:, ci:ci + 1]
            li_r = gates_t[ci:ci + 1, :]
            bc = b_col[:, cf:cf + 1]
            br = b_row[cf:cf + 1, :]
            b_last = bc[L - 1:L, :] if d == 0 else bc[0:1, :]
            m_prev = m_scr[r:r + 1, 0:1]
            pair = (h // 2) * LANES
            half = (lane >= (h % 2) * dk) & (lane < (h % 2 + 1) * dk)
            q = jnp.where(half, qk_ref[:, pair:pair + LANES] * (dk ** -0.5), 0.0).astype(BF16)
            k = qk_ref[:, B_HEADS * dk + pair:B_HEADS * dk + pair + LANES].astype(BF16)
            v = v_ref[:, h * dv:(h + 1) * dv]
            dm = jnp.where(keep, bc - br + li_r, NEG)
            g_in = bc + m_prev
            m_t = jnp.maximum(jnp.max(dm, axis=1, keepdims=True), g_in)
            sm = (_dot_nt(q, k) * jnp.exp(dm - m_t)).astype(BF16)
            vaug = jnp.concatenate([v, ones_v], axis=1)
            c_prev = c_scr[r]
            nd = _dot(sm, vaug) + jnp.exp(g_in - m_t) * _dot(q, c_prev.astype(BF16))
            den = jnp.maximum(jnp.abs(nd[:, dv:]), jnp.exp(-m_t))
            o_ref[:, h * dv:(h + 1) * dv] = nd[:, :dv] / den
            w_c = b_last - bc + li_c
            m_loc = jnp.max(w_c, axis=0, keepdims=True)
            e = jnp.exp(w_c - m_loc)
            ev = jnp.concatenate([e * v.astype(F32), jnp.broadcast_to(e, (L, dv))], axis=1).astype(BF16)
            c_loc = _dot_tn(k, ev)
            own = (srow >= (h % 2) * dk) & (srow < (h % 2 + 1) * dk)
            m_new = jnp.maximum(b_last + m_prev, m_loc)
            c_scr[r] = jnp.exp(b_last + m_prev - m_new) * c_prev + jnp.where(own, jnp.exp(m_loc - m_new) * c_loc, 0.0)
            m_scr[r:r + 1, :] = jnp.broadcast_to(m_new, (1, LANES))


def _scan_specs(width, ctx_chunks, n_chunks):
    fwd = pl.BlockSpec((None, CHUNK, width), lambda b, j: (b, j, 0))
    bwd = pl.BlockSpec((None, CHUNK, width), lambda b, j: (b, _bwd_chunk(j, ctx_chunks, n_chunks), 0))
    return fwd, bwd


def _mlstm_call(qk, v, gates, bias, ctx_len):
    bsz, s, w = v.shape
    nc = s // CHUNK
    cc = ctx_len // CHUNK
    qf, qb = _scan_specs(qk.shape[-1], cc, nc)
    vf, vb = _scan_specs(w, cc, nc)
    gf, gb = _scan_specs(LANES, cc, nc)
    return pl.pallas_call(
        _mlstm_body,
        grid=(bsz, nc),
        in_specs=[qf, vf, gf, qb, vb, gb, pl.BlockSpec((1, LANES), lambda b, j: (0, 0))],
        out_specs=[vf, vb],
        out_shape=[jax.ShapeDtypeStruct((bsz, s, w), F32)] * 2,
        scratch_shapes=[pltpu.VMEM((2 * B_HEADS, LANES, 2 * LANES), F32),
                        pltpu.VMEM((2 * B_HEADS, LANES), F32)],
        compiler_params=_cparams("parallel", "arbitrary"),
        name="mlstm_scan",
    )(qk, v, gates, qk, v, gates, bias)


def _ssd_body(xf_ref, dtf_ref, xb_ref, dtb_ref, dtbias_ref, nega_ref, skip_ref, yf_ref, yb_ref, s_scr):
    L = CHUNK
    hd = HEAD_DIM
    inner = 4 * D_GROUPS * hd
    hpg = 4

    @pl.when(pl.program_id(1) == 0)
    def _():
        s_scr[...] = jnp.zeros_like(s_scr)

    lane = lax.broadcasted_iota(jnp.int32, (L, LANES), 1)
    lo = lane < hd
    lo1 = lo[0:1, :]
    dirs = ((xf_ref, dtf_ref, yf_ref), (xb_ref, dtb_ref, yb_ref))
    for d, (x_ref, dt_ref, y_ref) in enumerate(dirs):
        keep, tri = _tri_masks(d)
        dt = _softplus(dt_ref[...] + dtbias_ref[...])
        a = dt * nega_ref[...]
        dt_t = dt.T[0:16, :]
        a_t = a.T[0:16, :]
        ac_col = _cumsum_cols(tri, a)
        ac_row = _cumsum_rows(a_t, tri)
        for g in range(D_GROUPS):
            bg = x_ref[:, inner + g * LANES:inner + (g + 1) * LANES].astype(BF16)
            cg = x_ref[:, inner + (D_GROUPS + g) * LANES:inner + (D_GROUPS + g + 1) * LANES].astype(BF16)
            cb = _dot_nt(cg, bg)
            for pr in range(2):
                c0 = g * hpg * hd + pr * LANES
                xp = x_ref[:, c0:c0 + LANES]
                xpb = xp.astype(BF16)
                ys, eas, wsts, als = [], [], [], []
                for hh in range(2):
                    col = d * D_GROUPS * hpg + g * hpg + pr * 2 + hh
                    acc = ac_col[:, col:col + 1]
                    acr = ac_row[col:col + 1, :]
                    seg = jnp.where(keep, acc - acr, NEG)
                    mix = (cb * jnp.exp(seg) * dt_t[col:col + 1, :]).astype(BF16)
                    ys.append(_dot(mix, xpb))
                    a_last = acc[L - 1:L, :] if d == 0 else acc[0:1, :]
                    eas.append(jnp.exp(acc))
                    wsts.append(jnp.exp(a_last - acc) * dt[:, col:col + 1])
                    als.append(jnp.exp(a_last))
                sidx = d * 2 * D_GROUPS + g * 2 + pr
                st = s_scr[sidx]
                y = jnp.where(lo, ys[0], ys[1]) + _dot(cg, st.astype(BF16)) * jnp.where(lo, eas[0], eas[1])
                if d == 0:
                    y = y + skip_ref[:, c0:c0 + LANES] * xp
                y_ref[:, c0:c0 + LANES] = y
                xw = (xp * jnp.where(lo, wsts[0], wsts[1])).astype(BF16)
                s_scr[sidx] = jnp.where(lo1, als[0], als[1]) * st + _dot_tn(bg, xw)


def _ssd_call(xbc, dt, dt_bias, neg_a, skip, ctx_len):
    bsz, s, w = xbc.shape
    nc = s // CHUNK
    cc = ctx_len // CHUNK
    inner = skip.shape[-1]
    xf, xb = _scan_specs(w, cc, nc)
    df, db = _scan_specs(LANES, cc, nc)
    yf, yb = _scan_specs(inner, cc, nc)
    vec = lambda n: pl.BlockSpec((1, n), lambda b, j: (0, 0))
    return pl.pallas_call(
        _ssd_body,
        grid=(bsz, nc),
        in_specs=[xf, df, xb, db, vec(LANES), vec(LANES), vec(inner)],
        out_specs=[yf, yb],
        out_shape=[jax.ShapeDtypeStruct((bsz, s, inner), F32)] * 2,
        scratch_shapes=[pltpu.VMEM((4 * D_GROUPS, LANES, LANES), F32)],
        compiler_params=_cparams("parallel", "arbitrary"),
        name="ssd_scan",
    )(xbc, dt, xbc, dt, dt_bias, neg_a, skip)


def _diff_attn_body(lam_ref, q_ref, k_ref, v_ref, o_ref, *, lam_init):
    lv = lam_ref[...]
    lam = (jnp.exp(jnp.sum(lv[0:1] * lv[1:2], axis=1, keepdims=True))
           - jnp.exp(jnp.sum(lv[2:3] * lv[3:4], axis=1, keepdims=True)) + lam_init)
    tq = q_ref.shape[0]
    lane = lax.broadcasted_iota(jnp.int32, (tq, LANES), 1)
    nheads = q_ref.shape[1] // LANES

    def logits(h):
        q = q_ref[:, h * LANES:(h + 1) * LANES]
        k = k_ref[:, h * LANES:(h + 1) * LANES]
        return [_dot_nt(jnp.where((lane >= m * HEAD_DIM) & (lane < (m + 1) * HEAD_DIM), q, jnp.zeros((), BF16)), k)
                for m in range(2)]

    s_next = logits(0)
    for h in range(nheads):
        s_cur = s_next
        if h + 1 < nheads:
            s_next = logits(h + 1)
        ps, ls = [], []
        for s in s_cur:
            p = jnp.exp2(s - jnp.max(s, axis=1, keepdims=True))
            ps.append(p)
            ls.append(jnp.sum(p, axis=1, keepdims=True))
        a = (ps[0] - (lam * ls[0] / ls[1]) * ps[1]).astype(BF16)
        o_ref[:, h * LANES:(h + 1) * LANES] = _dot(a, v_ref[:, h * LANES:(h + 1) * LANES]) / ls[0]


def _diff_attn_call(q, k, v, lam_vecs, lam_init, ctx_len):
    bsz, s, w = q.shape
    t = s - ctx_len
    off = ctx_len // TQ
    return pl.pallas_call(
        functools.partial(_diff_attn_body, lam_init=lam_init),
        grid=(bsz, t // TQ),
        in_specs=[pl.BlockSpec(lam_vecs.shape, lambda b, i: (0, 0)),
                  pl.BlockSpec((None, TQ, w), lambda b, i: (b, i + off, 0)),
                  pl.BlockSpec((None, s, w), lambda b, i: (b, 0, 0)),
                  pl.BlockSpec((None, s, w), lambda b, i: (b, 0, 0))],
        out_specs=pl.BlockSpec((None, TQ, w), lambda b, i: (b, i, 0)),
        out_shape=jax.ShapeDtypeStruct((bsz, t, w), F32),
        compiler_params=_cparams("parallel", "arbitrary"),
        name="diff_attention",
    )(lam_vecs, q, k, v)


def _group_rms(x, width):
    return jnp.concatenate([_rms(x[:, a:a + width]) for a in range(0, x.shape[1], width)], axis=1)


def _residual_out(y, h_ref, g_ref, gate_ref, out_ref):
    out_ref[...] = h_ref[...] + gate_ref[...] * (_rms(y) * g_ref[...])


def _outproj_even_body(ya_ref, hf_ref, hb_ref, og_ref, ng_ref, w_ref, h_ref, g_ref, gate_ref, out_ref):
    half = ya_ref.shape[1]
    hn = _group_rms(hf_ref[...] + hb_ref[...], LANES) * ng_ref[...] * jax.nn.sigmoid(og_ref[...])
    y = _dot(ya_ref[...], w_ref[0:half, :]) + _dot(hn.astype(BF16), w_ref[half:, :])
    _residual_out(y, h_ref, g_ref, gate_ref, out_ref)


def _outproj_odd_body(oa_ref, yf_ref, yb_ref, z_ref, cg_ref, dg_ref, w_ref, h_ref, g_ref, gate_ref, out_ref,
                      *, out_scale):
    half = oa_ref.shape[1]
    on = _group_rms(oa_ref[...], LANES) * cg_ref[...] * out_scale
    yz = (yf_ref[...] + yb_ref[...]) * _silu(z_ref[...])
    sn = _group_rms(yz, half // D_GROUPS) * dg_ref[...]
    y = _dot(on.astype(BF16), w_ref[0:half, :]) + _dot(sn.astype(BF16), w_ref[half:, :])
    _residual_out(y, h_ref, g_ref, gate_ref, out_ref)


def _outproj_call(body, acts, act_offs, vecs, w, h, g, gate, ctx_tiles, t_off, n_t):
    bsz, _, d = h.shape
    seg = _seg_map(ctx_tiles, t_off)
    out_rows = n_t * TM

    def act_spec(arr, off):
        return pl.BlockSpec((None, TM, arr.shape[-1]), lambda b, t: (b, t + off, 0))

    in_specs = ([act_spec(a_, o_) for a_, o_ in zip(acts, act_offs)]
                + [pl.BlockSpec(v_.shape, lambda b, t: (0, 0)) for v_ in vecs]
                + [pl.BlockSpec(w.shape, lambda b, t: (0, 0)),
                   pl.BlockSpec((None, TM, d), lambda b, t: (b, t + t_off, 0)),
                   pl.BlockSpec((1, d), lambda b, t: (0, 0)),
                   pl.BlockSpec((None, None, 1, d), seg)])
    return pl.pallas_call(
        body, grid=(bsz, n_t), in_specs=in_specs,
        out_specs=pl.BlockSpec((None, TM, d), lambda b, t: (b, t, 0)),
        out_shape=jax.ShapeDtypeStruct((bsz, out_rows, d), F32),
        compiler_params=_cparams("parallel", "arbitrary"),
        name="out_proj",
    )(*acts, *vecs, w, h, g, gate)


def _ffn_body(hm_ref, hp_ref, hn_ref, g2_ref, sh_ref, sc_ref, wg_ref, wu_ref, cw_ref, wd_ref, g3_ref, gate_ref,
              out_ref, gscr, uscr, *, nt, seg_starts):
    t = pl.program_id(1)
    prev_ok, nxt_ok = _halo_flags(t, nt, seg_starts)

    def normmod(h):
        return (_rms(h) * g2_ref[...]) * (1.0 + sc_ref[...]) + sh_ref[...]

    um = normmod(hm_ref[...])
    up = jnp.where(prev_ok, normmod(hp_ref[...]), 0.0)
    un = jnp.where(nxt_ok, normmod(hn_ref[...]), 0.0)
    ucat = jnp.concatenate([up, um, un], axis=0).astype(BF16)
    umb = um.astype(BF16)

    nchunk = wg_ref.shape[0]

    def front(c):
        gscr[c % 2] = _dot(ucat, wg_ref[c])
        uscr[c % 2] = _dot(umb, wu_ref[c])

    front(0)
    y = None
    for c in range(nchunk):
        if c + 1 < nchunk:
            front(c + 1)
        gbuf = gscr.at[c % 2]
        cw = cw_ref[c]
        gc = (gbuf[pl.ds(HALO - 1, TM), :] * cw[0:1, :] + gbuf[pl.ds(HALO, TM), :] * cw[1:2, :]
              + gbuf[pl.ds(HALO + 1, TM), :] * cw[2:3, :])
        hid = (_silu(gc) * uscr[c % 2]).astype(BF16)
        part = _dot(hid, wd_ref[c])
        y = part if y is None else y + part
    _residual_out(y, hm_ref, g3_ref, gate_ref, out_ref)


def _ffn_call(h, g2, shift, scale, wg, wu, cw, wd, g3, gate, ctx_tiles, seg_off, seg_starts):
    bsz, s, d = h.shape
    n_t = s // TM
    const3 = lambda arr: pl.BlockSpec(arr.shape, lambda b, t: (0, 0, 0))
    vec = pl.BlockSpec((1, d), lambda b, t: (0, 0))
    mod = pl.BlockSpec((None, None, 1, d), _seg_map(ctx_tiles, seg_off))
    return pl.pallas_call(
        functools.partial(_ffn_body, nt=n_t, seg_starts=seg_starts),
        grid=(bsz, n_t),
        in_specs=_halo_specs(d, s, 0) + [vec, mod, mod, const3(wg), const3(wu), const3(cw), const3(wd), vec, mod],
        out_specs=pl.BlockSpec((None, TM, d), lambda b, t: (b, t, 0)),
        out_shape=jax.ShapeDtypeStruct((bsz, s, d), F32),
        scratch_shapes=[pltpu.VMEM((2, TM + 2 * HALO, FFN_CHUNK), F32), pltpu.VMEM((2, TM, FFN_CHUNK), F32)],
        compiler_params=_cparams("parallel", "arbitrary"),
        name="conv_ffn",
    )(h, h, h, g2, shift, scale, wg, wu, cw, wd, g3, gate)


def _rope_tables(ctx_len, t):
    pos = jnp.arange(t)
    row = (pos // GRID_W).astype(F32)
    col = (pos % GRID_W).astype(F32)
    nf = HEAD_DIM // 4
    inv = ROPE_BASE ** (-jnp.arange(nf, dtype=F32) / nf)
    ar = row[:, None] * inv
    ac = col[:, None] * inv
    cos = jnp.concatenate([jnp.cos(ar), jnp.cos(ar), jnp.cos(ac), jnp.cos(ac)], axis=1)
    sin = jnp.concatenate([-jnp.sin(ar), jnp.sin(ar), -jnp.sin(ac), jnp.sin(ac)], axis=1)
    cos = jnp.concatenate([jnp.ones((ctx_len, HEAD_DIM), F32), cos], axis=0)
    sin = jnp.concatenate([jnp.zeros((ctx_len, HEAD_DIM), F32), sin], axis=0)
    return jnp.tile(cos, (1, LANES // HEAD_DIM)), jnp.tile(sin, (1, LANES // HEAD_DIM))


def _pad_cols(w, n):
    return jnp.pad(w, ((0, 0), (0, n - w.shape[1])))


def _gqa_perm():
    idx = [(h * A_GROUP + g) * HEAD_DIM + dd
           for g in range(A_GROUP) for h in range(A_KV_HEADS) for dd in range(HEAD_DIM)]
    return jnp.array(idx, dtype=jnp.int32)


def _ffn_weights(w_gate, w_up, conv, w_down):
    d, hid = w_gate.shape
    nchunk = hid // FFN_CHUNK
    wg = w_gate.astype(BF16).reshape(d, nchunk, FFN_CHUNK).transpose(1, 0, 2)
    wu = w_up.astype(BF16).reshape(d, nchunk, FFN_CHUNK).transpose(1, 0, 2)
    cw = conv.reshape(conv.shape[0], nchunk, FFN_CHUNK).transpose(1, 0, 2)
    wd = w_down.astype(BF16).reshape(nchunk, FFN_CHUNK, d)
    return wg, wu, cw, wd


def kernel(x, c, ctx, c_ctx, mod_w, mod_b, norm_g, ffn_w_gate, ffn_w_up, ffn_conv, ffn_w_down, ev_w_in, ev_w_out, a_sink, b_conv, b_gate_b, b_norm_g, od_w_in, od_w_out, c_lambda, c_norm_g, d_conv, d_conv_b, d_dt_bias, d_a_log, d_skip, d_norm_g):
    bsz, t, d = x.shape
    ctx_len = ctx.shape[1]
    depth = mod_w.shape[0]
    s = ctx_len + t
    half = d // 2
    assert ctx_len % TM == 0 and t % TM == 0 and t % GRID_W == 0 and ffn_w_gate.shape[-1] % FFN_CHUNK == 0
    n_t = s // TM
    ctx_tiles = ctx_len // TM

    mod_rows = -(-(bsz + 1) // HALO) * HALO
    cc = jnp.concatenate([c, c_ctx[None, :], jnp.zeros((mod_rows - bsz - 1, d), F32)], axis=0)
    mods = _mod_call(cc, mod_w, mod_b).reshape(depth, mod_rows, 6, d)

    def layer_mods(l):
        lat = mods[l, :bsz]
        cx = jnp.broadcast_to(mods[l, bsz][None], (bsz, 6, d))
        m = jnp.stack([cx, lat], axis=1)
        return [m[:, :, i][:, :, None, :] for i in range(6)]

    cos, sin = _rope_tables(ctx_len, t)
    h = jnp.concatenate([ctx, x], axis=1)

    for l in range(depth):
        last = l == depth - 1
        j = l // 2
        m = layer_mods(l)
        g = norm_g[l]
        wg, wu, cw, wd = _ffn_weights(ffn_w_gate[l], ffn_w_up[l], ffn_conv[l], ffn_w_down[l])
        if l % 2 == 0:
            w_in = ev_w_in[j]
            perm = _gqa_perm()
            akv = A_KV_HEADS * HEAD_DIM
            o1 = half + 2 * akv
            w_cat = jnp.concatenate([w_in[:, :half][:, perm], w_in[:, half:o1 + 3 * half],
                                     _pad_cols(w_in[:, o1 + 3 * half:], LANES)], axis=1).astype(BF16)
            groups = ((0, half, True, HEAD_DIM ** -0.5), (half, akv, True, 1.0), (half + akv, akv, False, 1.0),
                      (o1, half, False, 1.0), (o1 + half, half, False, 1.0), (o1 + 2 * half, half, False, 1.0),
                      (o1 + 3 * half, LANES, False, 1.0))
            q, k, v, mqk, mv, mo, gates = _inproj_call(
                h, g[0:1], m[0], m[1], cos, sin, w_cat, groups, (BF16, BF16, BF16, F32, BF16, F32, F32), ctx_tiles)
            ya = _win_attn_call(q, k, v, a_sink[j], ctx_len)
            qkc = _conv_call(mqk, b_conv[j], jnp.zeros((1, half), F32), (0, ctx_tiles))
            hf, hb = _mlstm_call(qkc, mv, gates, _pad_cols(b_gate_b[j][None, :], LANES), ctx_len)
            w_out = ev_w_out[j]
            w_out = jnp.concatenate([w_out[:half][perm], w_out[half:]], axis=0).astype(BF16)
            acts, vecs, body = (ya, hf, hb, mo), (b_norm_g[j][None, :],), _outproj_even_body
        else:
            w_in = od_w_in[j]
            xbc_w = 2 * half
            w_cat = jnp.concatenate([w_in[:, :4 * half + xbc_w],
                                     _pad_cols(w_in[:, 4 * half + xbc_w:], LANES)], axis=1).astype(BF16)
            groups = ((0, half, True, HEAD_DIM ** -0.5 * LOG2E), (half, half, True, 1.0), (2 * half, half, False, 1.0),
                      (3 * half, half, False, 1.0), (4 * half, xbc_w, False, 1.0), (4 * half + xbc_w, LANES, False, 1.0))
            q, k, v, z, xbc, dt = _inproj_call(
                h, g[0:1], m[0], m[1], cos, sin, w_cat, groups, (BF16, BF16, BF16, F32, F32, F32), ctx_tiles)
            lam_init = 0.8 - 0.6 * math.exp(-LAM_DEPTH_RATE * l)
            oa = _diff_attn_call(q, k, v, c_lambda[j], lam_init, ctx_len)
            xbcc = _conv_call(xbc, d_conv[j], d_conv_b[j][None, :], (0, ctx_tiles))
            dt_bias = _pad_cols(d_dt_bias[j].reshape(1, -1), LANES)
            neg_a = _pad_cols(-jnp.exp(d_a_log[j].astype(F32)).reshape(1, -1), LANES)
            skip = jnp.repeat(d_skip[j].astype(F32), HEAD_DIM)[None, :]
            yf, yb = _ssd_call(xbcc, dt, dt_bias, neg_a, skip, ctx_len)
            w_out = od_w_out[j].astype(BF16)
            acts, vecs = (oa, yf, yb, z), (c_norm_g[j][None, :], d_norm_g[j][None, :])
            body = functools.partial(_outproj_odd_body, out_scale=1.0 - lam_init)
        t_off = ctx_tiles if last else 0
        if l % 2 == 0:
            act_offs = (t_off,) * 4
        else:
            assert last, "differential attention is only computed for latent queries"
            act_offs = (0,) + (t_off,) * 3
        h_mid = _outproj_call(body, acts, act_offs, vecs, w_out, h, g[1:2], m[2], ctx_tiles, t_off, n_t - t_off)
        seg_starts = (0,) if last else (0, ctx_tiles)
        h = _ffn_call(h_mid, g[2:3], m[3], m[4], wg, wu, cw, wd, g[3:4], m[5], ctx_tiles, t_off, seg_starts)
    return h
```

```python
import functools
import math

import jax
import jax.numpy as jnp
from jax import lax
from jax.experimental import pallas as pl
from jax.experimental.pallas import tpu as pltpu

F32 = jnp.float32
BF16 = jnp.bfloat16

EPS = 1e-6
ROPE_BASE = 10000.0
GRID_W = 64
HEAD_DIM = 64
A_KV_HEADS = 2
A_GROUP = 4
A_WINDOW = 128
B_HEADS = 4
D_GROUPS = 2
LAM_DEPTH_RATE = 0.3

LANES = 128
MXU_COLS = 256
HALO = 8
TM = 256
CHUNK = 128
FFN_CHUNK = 256
TQ = 256
WQ = 256
NEG = -1e30
LOG2E = math.log2(math.e)
VMEM_LIMIT = 56 * 1024 * 1024


def _cparams(*sem):
    return pltpu.CompilerParams(dimension_semantics=sem, vmem_limit_bytes=VMEM_LIMIT)


def _silu(x):
    return x * jax.nn.sigmoid(x)


def _softplus(x):
    return jnp.maximum(x, 0.0) + jnp.log1p(jnp.exp(-jnp.abs(x)))


def _log_sigmoid(x):
    return -_softplus(-x)


def _rms(x):
    return x * lax.rsqrt(jnp.mean(x * x, axis=-1, keepdims=True) + EPS)


def _dot(a, b):
    return jnp.dot(a, b, preferred_element_type=F32)


def _dot_nt(a, b):
    return lax.dot_general(a, b, (((1,), (1,)), ((), ())), preferred_element_type=F32)


def _dot_tn(a, b):
    return lax.dot_general(a, b, (((0,), (0,)), ((), ())), preferred_element_type=F32)


def _split3(x):
    x1 = x.astype(BF16)
    r1 = x - x1.astype(F32)
    x2 = r1.astype(BF16)
    x3 = (r1 - x2.astype(F32)).astype(BF16)
    return x1, x2, x3


def _cumsum_cols(tri, x):
    x1, x2, x3 = _split3(x)
    return _dot(tri, x1) + _dot(tri, x2) + _dot(tri, x3)


def _cumsum_rows(x, tri):
    x1, x2, x3 = _split3(x)
    return _dot_nt(x1, tri) + _dot_nt(x2, tri) + _dot_nt(x3, tri)


def _mod_body(c_ref, w_ref, b_ref, o_ref):
    a = _silu(c_ref[...]).astype(BF16)
    o_ref[...] = _dot(a, w_ref[...].astype(BF16)) + b_ref[...]


def _mod_call(cc, mod_w, mod_b):
    depth, d, n6 = mod_w.shape
    rows = cc.shape[0]
    tn = 1024
    return pl.pallas_call(
        _mod_body,
        grid=(depth, n6 // tn),
        in_specs=[pl.BlockSpec((rows, d), lambda l, j: (0, 0)),
                  pl.BlockSpec((None, d, tn), lambda l, j: (l, 0, j)),
                  pl.BlockSpec((None, 1, tn), lambda l, j: (l, 0, j))],
        out_specs=pl.BlockSpec((None, rows, tn), lambda l, j: (l, 0, j)),
        out_shape=jax.ShapeDtypeStruct((depth, rows, n6), F32),
        compiler_params=_cparams("arbitrary", "arbitrary"),
        name="modulation",
    )(cc, mod_w, mod_b.reshape(depth, 1, n6))


def _rope_block(y, cos, sin, lo):
    partner = jnp.where(lo, pltpu.roll(y, LANES - 16, 1), pltpu.roll(y, 16, 1))
    return y * cos + partner * sin


def _row_specs(hs, ctx_tiles, t_off):
    d = hs[0].shape[-1]
    if len(hs) == 1:
        return [pl.BlockSpec((None, TM, d), lambda b, t: (b, t + t_off, 0))]
    return [pl.BlockSpec((None, TM, d), lambda b, t: (b, jnp.minimum(t + t_off, ctx_tiles - 1), 0)),
            pl.BlockSpec((None, TM, d), lambda b, t: (b, jnp.maximum(t + t_off - ctx_tiles, 0), 0))]


def _load_rows(h_refs, ctx_tiles, t_off):
    if len(h_refs) == 1:
        return h_refs[0][...]
    return jnp.where(pl.program_id(1) + t_off < ctx_tiles, h_refs[0][...], h_refs[1][...])


def _inproj_body(*refs, groups, n_src, ctx_tiles):
    h_refs = refs[:n_src]
    g_ref, sh_ref, sc_ref, cos_ref, sin_ref, w_ref = refs[n_src:n_src + 6]
    out_refs = refs[n_src + 6:]
    u = _rms(_load_rows(h_refs, ctx_tiles, 0)) * g_ref[...]
    u = u * (1.0 + sc_ref[...]) + sh_ref[...]
    ub = u.astype(BF16)
    cos = cos_ref[...]
    sin = sin_ref[...]
    lane = lax.broadcasted_iota(jnp.int32, cos.shape, 1)
    lo = (lane & 31) < 16
    for (start, width, rope_cols, qscale), o_ref in zip(groups, out_refs):
        step = min(width, MXU_COLS if rope_cols else 2 * MXU_COLS)
        for a in range(0, width, step):
            y = _dot(ub, w_ref[:, start + a:start + a + step])
            for r in range(0, step, LANES):
                yr = y[:, r:r + LANES]
                if a + r < rope_cols:
                    yr = _rope_block(yr, cos, sin, lo)
                if qscale != 1.0:
                    yr = yr * qscale
                o_ref[:, a + r:a + r + LANES] = yr.astype(o_ref.dtype)


def _seg_map(ctx_tiles, off):
    return lambda b, t: (b, jnp.where(t + off >= ctx_tiles, 1, 0), 0, 0)


def _inproj_call(hs, g, shift, scale, cos, sin, w, groups, dtypes, ctx_tiles):
    bsz, _, d = hs[0].shape
    s = sum(h_.shape[1] for h_ in hs)
    nt = s // TM
    seg = _seg_map(ctx_tiles, 0)
    in_specs = _row_specs(hs, ctx_tiles, 0) + [
                pl.BlockSpec((1, d), lambda b, t: (0, 0)),
                pl.BlockSpec((None, None, 1, d), seg),
                pl.BlockSpec((None, None, 1, d), seg),
                pl.BlockSpec((TM, LANES), lambda b, t: (t, 0)),
                pl.BlockSpec((TM, LANES), lambda b, t: (t, 0)),
                pl.BlockSpec(w.shape, lambda b, t: (0, 0))]
    out_specs = [pl.BlockSpec((None, TM, gr[1]), lambda b, t: (b, t, 0)) for gr in groups]
    out_shape = [jax.ShapeDtypeStruct((bsz, s, gr[1]), dt) for gr, dt in zip(groups, dtypes)]
    return pl.pallas_call(
        functools.partial(_inproj_body, groups=groups, n_src=len(hs), ctx_tiles=ctx_tiles),
        grid=(bsz, nt), in_specs=in_specs, out_specs=out_specs, out_shape=out_shape,
        compiler_params=_cparams("parallel", "arbitrary"),
        name="in_proj",
    )(*hs, g, shift, scale, cos, sin, w)


def _halo_flags(t, nt, seg_starts):
    prev_ok = t >= 0
    nxt_ok = (t + 1) < nt
    for s0 in seg_starts:
        prev_ok = jnp.logical_and(prev_ok, t != s0)
        nxt_ok = jnp.logical_and(nxt_ok, (t + 1) != s0)
    return prev_ok, nxt_ok


def _conv_body(xm_ref, xp_ref, xn_ref, w_ref, b_ref, o_ref, scr, *, nt, seg_starts, taps):
    t = pl.program_id(1)
    prev_ok, nxt_ok = _halo_flags(t, nt, seg_starts)
    scr[0:HALO, :] = jnp.where(prev_ok, xp_ref[...], 0.0)
    scr[HALO:HALO + TM, :] = xm_ref[...]
    scr[HALO + TM:HALO + TM + HALO, :] = jnp.where(nxt_ok, xn_ref[...], 0.0)
    pad = taps // 2
    y = scr[pl.ds(HALO - pad, TM), :] * w_ref[0:1, :]
    for j in range(1, taps):
        y = y + scr[pl.ds(HALO - pad + j, TM), :] * w_ref[j:j + 1, :]
    o_ref[...] = _silu(y + b_ref[...]).astype(o_ref.dtype)


def _halo_specs(width, s, t_off):
    per = TM // HALO
    last = s // HALO - 1
    return [pl.BlockSpec((None, TM, width), lambda b, t: (b, t + t_off, 0)),
            pl.BlockSpec((None, HALO, width), lambda b, t: (b, jnp.maximum((t + t_off) * per - 1, 0), 0)),
            pl.BlockSpec((None, HALO, width), lambda b, t: (b, jnp.minimum((t + t_off + 1) * per, last), 0))]


def _conv_call(x, w, bias, seg_starts):
    bsz, s, c = x.shape
    nt = s // TM
    taps = w.shape[0]
    return pl.pallas_call(
        functools.partial(_conv_body, nt=nt, seg_starts=seg_starts, taps=taps),
        grid=(bsz, nt),
        in_specs=_halo_specs(c, s, 0) + [pl.BlockSpec((taps, c), lambda b, t: (0, 0)),
                                         pl.BlockSpec((1, c), lambda b, t: (0, 0))],
        out_specs=pl.BlockSpec((None, TM, c), lambda b, t: (b, t, 0)),
        out_shape=jax.ShapeDtypeStruct((bsz, s, c), F32),
        scratch_shapes=[pltpu.VMEM((TM + 2 * HALO, c), F32)],
        compiler_params=_cparams("parallel", "arbitrary"),
        name="short_conv",
    )(x, x, x, w, bias)


def _win_attn_body(sink_ref, bias_ref, q_ref, kvp_ref, kvc_ref, kvn_ref, kvx_ref, o_ref):
    blk = q_ref.shape[0]
    kvcat = jnp.concatenate([kvp_ref[...], kvc_ref[...], kvn_ref[...], kvx_ref[...]], axis=0)
    kcat = kvcat[:, :LANES]
    vcat = kvcat[:, LANES:]
    nkeys = kvcat.shape[0]
    rows = A_GROUP * blk
    bias = bias_ref[...]
    lane = lax.broadcasted_iota(jnp.int32, (blk, LANES), 1)
    rowg = lax.broadcasted_iota(jnp.int32, (rows, 1), 0) // blk
    q = q_ref[...]

    def logits(h):
        half = (lane >= h * HEAD_DIM) & (lane < (h + 1) * HEAD_DIM)
        qh = jnp.concatenate(
            [jnp.where(half, q[:, g * LANES:(g + 1) * LANES], jnp.zeros((), BF16)) for g in range(A_GROUP)], axis=0)
        return _dot_nt(qh, kcat)

    raw = [logits(h) for h in range(A_KV_HEADS)]
    outs = []
    for h in range(A_KV_HEADS):
        s = (raw[h].reshape(A_GROUP, blk, nkeys) + bias[None]).reshape(rows, nkeys)
        sink = jnp.zeros((rows, 1), F32)
        for g in range(A_GROUP):
            sink = jnp.where(rowg == g, sink_ref[h * A_GROUP + g] * LOG2E, sink)
        m = jnp.maximum(jnp.max(s, axis=1, keepdims=True), sink)
        p = jnp.exp2(s - m)
        den = jnp.sum(p, axis=1, keepdims=True) + jnp.exp2(sink - m)
        outs.append(_dot(p.astype(BF16), vcat) / den)
    for g in range(A_GROUP):
        o_ref[:, g * LANES:(g + 1) * LANES] = jnp.where(
            lane < HEAD_DIM, outs[0][g * blk:(g + 1) * blk], outs[1][g * blk:(g + 1) * blk]).astype(o_ref.dtype)


def _win_attn_bias(ctx_len):
    qi = jnp.arange(WQ)[:, None]
    kj = jnp.arange(WQ + 2 * A_WINDOW)[None, :]
    band = jnp.abs(kj - A_WINDOW - qi) <= A_WINDOW
    variants = [jnp.zeros_like(band), band & (kj >= A_WINDOW), band, band & (kj < A_WINDOW + WQ)]
    win = jnp.stack([jnp.where(v_, 0.0, NEG) for v_ in variants]).astype(F32)
    return jnp.concatenate([win, jnp.zeros((4, WQ, ctx_len), F32)], axis=2)


def _win_attn_call(q, kv, sink, ctx_len):
    bsz, s, qw = q.shape
    kw = kv.shape[-1]
    nq = s // WQ
    cq = ctx_len // WQ
    per = WQ // A_WINDOW
    nblk = s // A_WINDOW
    assert ctx_len % WQ == 0 and nq - cq >= 2
    bias = _win_attn_bias(ctx_len)
    clamp = lambda j: jnp.clip(j, cq * per, nblk - 1)
    variant = lambda i: jnp.where(i < cq, 0, jnp.where(i == cq, 1, jnp.where(i == nq - 1, 3, 2)))
    kv_specs = [pl.BlockSpec((None, A_WINDOW, kw), lambda b, i: (b, clamp(i * per - 1), 0)),
                pl.BlockSpec((None, WQ, kw), lambda b, i: (b, i, 0)),
                pl.BlockSpec((None, A_WINDOW, kw), lambda b, i: (b, clamp((i + 1) * per), 0)),
                pl.BlockSpec((None, ctx_len, kw), lambda b, i: (b, 0, 0))]
    return pl.pallas_call(
        _win_attn_body,
        grid=(bsz, nq),
        in_specs=[pl.BlockSpec(memory_space=pltpu.SMEM),
                  pl.BlockSpec((None,) + bias.shape[1:], lambda b, i: (variant(i), 0, 0)),
                  pl.BlockSpec((None, WQ, qw), lambda b, i: (b, i, 0))] + kv_specs,
        out_specs=pl.BlockSpec((None, WQ, qw), lambda b, i: (b, i, 0)),
        out_shape=jax.ShapeDtypeStruct((bsz, s, qw), BF16),
        compiler_params=_cparams("parallel", "arbitrary"),
        name="window_attention",
    )(sink, bias, q, kv, kv, kv, kv)


def _bwd_chunk(j, ctx_chunks, n_chunks):
    return jnp.where(j < ctx_chunks, ctx_chunks - 1 - j, n_chunks + ctx_chunks - 1 - j)


def _tri_masks(d):
    row = lax.broadcasted_iota(jnp.int32, (CHUNK, CHUNK), 0)
    col = lax.broadcasted_iota(jnp.int32, (CHUNK, CHUNK), 1)
    keep = (col <= row) if d == 0 else (col >= row)
    return keep, jnp.where(keep, 1.0, 0.0).astype(BF16)


def _mlstm_body(qkf_ref, vf_ref, gf_ref, qkb_ref, vb_ref, gb_ref, bias_ref, hf_ref, hb_ref, c_scr, m_scr):
    L = CHUNK
    dk = HEAD_DIM
    dv = LANES

    @pl.when(pl.program_id(1) == 0)
    def _():
        c_scr[...] = jnp.zeros_like(c_scr)
        m_scr[...] = jnp.zeros_like(m_scr)

    lane = lax.broadcasted_iota(jnp.int32, (L, LANES), 1)
    srow = lax.broadcasted_iota(jnp.int32, (LANES, 2 * dv), 0)
    ones_v = jnp.ones((L, dv), BF16)
    dirs = ((qkf_ref, vf_ref, gf_ref, hf_ref), (qkb_ref, vb_ref, gb_ref, hb_ref))
    for d, (qk_ref, v_ref, g_ref, o_ref) in enumerate(dirs):
        keep, tri = _tri_masks(d)
        gates = g_ref[...] + bias_ref[...]
        gates_t = gates.T[0:16, :]
        b_col = _cumsum_cols(tri, _log_sigmoid(gates))
        b_row = _cumsum_rows(_log_sigmoid(gates_t), tri)
        k_pairs = [qk_ref[:, B_HEADS * dk + p * LANES:B_HEADS * dk + (p + 1) * LANES] for p in range(B_HEADS // 2)]
        kt_pairs = [kp.T.astype(BF16) for kp in k_pairs]
        for h in range(B_HEADS):
            ci = d * B_HEADS + h
            cf = 2 * B_HEADS + ci
            r = ci
            li_c = jnp.broadcast_to(gates[:, ci:ci + 1], (L, LANES))
            li_r = gates_t[ci:ci + 1, :]
            bc = jnp.broadcast_to(b_col[:, cf:cf + 1], (L, LANES))
            br = b_row[cf:cf + 1, :]
            b_last = bc[L - 1:L, :] if d == 0 else bc[0:1, :]
            m_prev = m_scr[r:r + 1, :]
            pair = (h // 2) * LANES
            half = (lane >= (h % 2) * dk) & (lane < (h % 2 + 1) * dk)
            q = jnp.where(half, qk_ref[:, pair:pair + LANES] * (dk ** -0.5), 0.0).astype(BF16)
            k = k_pairs[h // 2].astype(BF16)
            v = v_ref[:, h * dv:(h + 1) * dv]
            dm = jnp.where(keep, bc - br + li_r, NEG)
            g_in = bc + m_prev
            m_t = jnp.maximum(jnp.broadcast_to(jnp.max(dm, axis=1, keepdims=True), (L, LANES)), g_in)
            sm = (_dot_nt(q, k) * jnp.exp(dm - m_t)).astype(BF16)
            vaug = jnp.concatenate([v, ones_v], axis=1)
            c_prev = c_scr[r]
            intra = _dot(sm, vaug)
            inter = _dot(q, c_prev.astype(BF16))
            a_in = jnp.exp(g_in - m_t)
            den = jnp.maximum(jnp.abs(intra[:, dv:] + a_in * inter[:, dv:]), jnp.exp(-m_t))
            o_ref[:, h * dv:(h + 1) * dv] = (intra[:, :dv] + a_in * inter[:, :dv]) / den
            w_c = b_last - bc + li_c
            m_loc = jnp.max(w_c, axis=0, keepdims=True)
            e = jnp.exp(w_c - m_loc)
            ev = jnp.concatenate([e * v.astype(F32), e], axis=1).astype(BF16)
            c_loc = _dot(kt_pairs[h // 2], ev)
            own = (srow >= (h % 2) * dk) & (srow < (h % 2 + 1) * dk)
            m_new = jnp.maximum(b_last + m_prev, m_loc)
            keep_old = jnp.exp(b_last + m_prev - m_new)
            take_new = jnp.exp(m_loc - m_new)
            c_scr[r] = (jnp.concatenate([keep_old, keep_old], axis=1) * c_prev
                        + jnp.where(own, jnp.concatenate([take_new, take_new], axis=1) * c_loc, 0.0))
            m_scr[r:r + 1, :] = m_new


def _scan_specs(width, ctx_chunks, n_chunks):
    fwd = pl.BlockSpec((None, CHUNK, width), lambda b, j: (b, j, 0))
    bwd = pl.BlockSpec((None, CHUNK, width), lambda b, j: (b, _bwd_chunk(j, ctx_chunks, n_chunks), 0))
    return fwd, bwd


def _mlstm_call(qk, v, gates, bias, ctx_len):
    bsz, s, w = v.shape
    nc = s // CHUNK
    cc = ctx_len // CHUNK
    qf, qb = _scan_specs(qk.shape[-1], cc, nc)
    vf, vb = _scan_specs(w, cc, nc)
    gf, gb = _scan_specs(LANES, cc, nc)
    return pl.pallas_call(
        _mlstm_body,
        grid=(bsz, nc),
        in_specs=[qf, vf, gf, qb, vb, gb, pl.BlockSpec((1, LANES), lambda b, j: (0, 0))],
        out_specs=[vf, vb],
        out_shape=[jax.ShapeDtypeStruct((bsz, s, w), F32)] * 2,
        scratch_shapes=[pltpu.VMEM((2 * B_HEADS, LANES, 2 * LANES), F32),
                        pltpu.VMEM((2 * B_HEADS, LANES), F32)],
        compiler_params=_cparams("parallel", "arbitrary"),
        name="mlstm_scan",
    )(qk, v, gates, qk, v, gates, bias)


def _ssd_body(xf_ref, dtf_ref, xb_ref, dtb_ref, dtbias_ref, nega_ref, skip_ref, yf_ref, yb_ref, s_scr):
    L = CHUNK
    hd = HEAD_DIM
    inner = 4 * D_GROUPS * hd
    hpg = 4

    @pl.when(pl.program_id(1) == 0)
    def _():
        s_scr[...] = jnp.zeros_like(s_scr)

    lane = lax.broadcasted_iota(jnp.int32, (L, LANES), 1)
    lo = lane < hd
    lo1 = lo[0:1, :]
    dirs = ((xf_ref, dtf_ref, yf_ref), (xb_ref, dtb_ref, yb_ref))
    for d, (x_ref, dt_ref, y_ref) in enumerate(dirs):
        keep, tri = _tri_masks(d)
        dt = _softplus(dt_ref[...] + dtbias_ref[...])
        a = dt * nega_ref[...]
        dt_t = dt.T[0:16, :]
        a_t = a.T[0:16, :]
        ac_col = _cumsum_cols(tri, a)
        ac_row = _cumsum_rows(a_t, tri)
        for g in range(D_GROUPS):
            bg_f = x_ref[:, inner + g * LANES:inner + (g + 1) * LANES]
            bg = bg_f.astype(BF16)
            bg_t = bg_f.T.astype(BF16)
            cg = x_ref[:, inner + (D_GROUPS + g) * LANES:inner + (D_GROUPS + g + 1) * LANES].astype(BF16)
            cb = _dot_nt(cg, bg)
            for pr in range(2):
                c0 = g * hpg * hd + pr * LANES
                xp = x_ref[:, c0:c0 + LANES]
                xpb = xp.astype(BF16)
                ys, eas, wsts, als = [], [], [], []
                for hh in range(2):
                    col = d * D_GROUPS * hpg + g * hpg + pr * 2 + hh
                    acc = jnp.broadcast_to(ac_col[:, col:col + 1], (L, LANES))
                    acr = ac_row[col:col + 1, :]
                    seg = jnp.where(keep, acc - acr, NEG)
                    mix = (cb * jnp.exp(seg) * dt_t[col:col + 1, :]).astype(BF16)
                    ys.append(_dot(mix, xpb))
                    a_last = acc[L - 1:L, :] if d == 0 else acc[0:1, :]
                    eas.append(jnp.exp(acc))
                    wsts.append(jnp.exp(a_last - acc) * jnp.broadcast_to(dt[:, col:col + 1], (L, LANES)))
                    als.append(jnp.exp(a_last))
                sidx = d * 2 * D_GROUPS + g * 2 + pr
                st = s_scr[sidx]
                y = jnp.where(lo, ys[0], ys[1]) + _dot(cg, st.astype(BF16)) * jnp.where(lo, eas[0], eas[1])
                if d == 0:
                    y = y + skip_ref[:, c0:c0 + LANES] * xp
                y_ref[:, c0:c0 + LANES] = y
                xw = (xp * jnp.where(lo, wsts[0], wsts[1])).astype(BF16)
                s_scr[sidx] = jnp.where(lo1, als[0], als[1]) * st + _dot(bg_t, xw)


def _ssd_call(xbc, dt, dt_bias, neg_a, skip, ctx_len):
    bsz, s, w = xbc.shape
    nc = s // CHUNK
    cc = ctx_len // CHUNK
    inner = skip.shape[-1]
    xf, xb = _scan_specs(w, cc, nc)
    df, db = _scan_specs(LANES, cc, nc)
    yf, yb = _scan_specs(inner, cc, nc)
    vec = lambda n: pl.BlockSpec((1, n), lambda b, j: (0, 0))
    return pl.pallas_call(
        _ssd_body,
        grid=(bsz, nc),
        in_specs=[xf, df, xb, db, vec(LANES), vec(LANES), vec(inner)],
        out_specs=[yf, yb],
        out_shape=[jax.ShapeDtypeStruct((bsz, s, inner), F32)] * 2,
        scratch_shapes=[pltpu.VMEM((4 * D_GROUPS, LANES, LANES), F32)],
        compiler_params=_cparams("parallel", "arbitrary"),
        name="ssd_scan",
    )(xbc, dt, xbc, dt, dt_bias, neg_a, skip)


def _diff_attn_body(lam_ref, q_ref, k_ref, v_ref, o_ref, *, lam_init):
    lv = lam_ref[...]
    lam = (jnp.exp(jnp.sum(lv[0:1] * lv[1:2], axis=1, keepdims=True))
           - jnp.exp(jnp.sum(lv[2:3] * lv[3:4], axis=1, keepdims=True)) + lam_init)
    tq = q_ref.shape[0]
    lane = lax.broadcasted_iota(jnp.int32, (tq, LANES), 1)
    nheads = q_ref.shape[1] // LANES

    def logits(h):
        q = q_ref[:, h * LANES:(h + 1) * LANES]
        k = k_ref[:, h * LANES:(h + 1) * LANES]
        return [_dot_nt(jnp.where((lane >= m * HEAD_DIM) & (lane < (m + 1) * HEAD_DIM), q, jnp.zeros((), BF16)), k)
                for m in range(2)]

    s_next = logits(0)
    for h in range(nheads):
        s_cur = s_next
        if h + 1 < nheads:
            s_next = logits(h + 1)
        ps, ls = [], []
        for s in s_cur:
            p = jnp.exp2(s - jnp.max(s, axis=1, keepdims=True))
            ps.append(p)
            ls.append(jnp.sum(p, axis=1, keepdims=True))
        a = (ps[0] - (lam * ls[0] / ls[1]) * ps[1]).astype(BF16)
        o_ref[:, h * LANES:(h + 1) * LANES] = _dot(a, v_ref[:, h * LANES:(h + 1) * LANES]) / ls[0]


def _diff_attn_call(q, k, v, lam_vecs, lam_init, ctx_len):
    bsz, s, w = q.shape
    t = s - ctx_len
    off = ctx_len // TQ
    return pl.pallas_call(
        functools.partial(_diff_attn_body, lam_init=lam_init),
        grid=(bsz, t // TQ),
        in_specs=[pl.BlockSpec(lam_vecs.shape, lambda b, i: (0, 0)),
                  pl.BlockSpec((None, TQ, w), lambda b, i: (b, i + off, 0)),
                  pl.BlockSpec((None, s, w), lambda b, i: (b, 0, 0)),
                  pl.BlockSpec((None, s, w), lambda b, i: (b, 0, 0))],
        out_specs=pl.BlockSpec((None, TQ, w), lambda b, i: (b, i, 0)),
        out_shape=jax.ShapeDtypeStruct((bsz, t, w), F32),
        compiler_params=_cparams("parallel", "arbitrary"),
        name="diff_attention",
    )(lam_vecs, q, k, v)


def _group_rms(x, width):
    return jnp.concatenate([_rms(x[:, a:a + width]) for a in range(0, x.shape[1], width)], axis=1)


def _residual_out(y, h, g_ref, gate_ref, out_ref):
    out_ref[...] = h + gate_ref[...] * (_rms(y) * g_ref[...])


def _outproj_even_body(ya_ref, hf_ref, hb_ref, og_ref, ng_ref, w_ref, g_ref, gate_ref, *tail, ctx_tiles, t_off):
    half = ya_ref.shape[1]
    hn = _group_rms(hf_ref[...] + hb_ref[...], LANES) * ng_ref[...] * jax.nn.sigmoid(og_ref[...])
    y = _dot(ya_ref[...], w_ref[0:half, :]) + _dot(hn.astype(BF16), w_ref[half:, :])
    _residual_out(y, _load_rows(tail[:-1], ctx_tiles, t_off), g_ref, gate_ref, tail[-1])


def _outproj_odd_body(oa_ref, yf_ref, yb_ref, z_ref, cg_ref, dg_ref, w_ref, g_ref, gate_ref, *tail,
                      out_scale, ctx_tiles, t_off):
    half = oa_ref.shape[1]
    on = _group_rms(oa_ref[...], LANES) * cg_ref[...] * out_scale
    yz = (yf_ref[...] + yb_ref[...]) * _silu(z_ref[...])
    sn = _group_rms(yz, half // D_GROUPS) * dg_ref[...]
    y = _dot(on.astype(BF16), w_ref[0:half, :]) + _dot(sn.astype(BF16), w_ref[half:, :])
    _residual_out(y, _load_rows(tail[:-1], ctx_tiles, t_off), g_ref, gate_ref, tail[-1])


def _outproj_call(body, acts, act_offs, vecs, w, hs, g, gate, ctx_tiles, t_off, n_t):
    bsz, _, d = hs[0].shape
    seg = _seg_map(ctx_tiles, t_off)
    out_rows = n_t * TM

    def act_spec(arr, off):
        return pl.BlockSpec((None, TM, arr.shape[-1]), lambda b, t: (b, t + off, 0))

    in_specs = ([act_spec(a_, o_) for a_, o_ in zip(acts, act_offs)]
                + [pl.BlockSpec(v_.shape, lambda b, t: (0, 0)) for v_ in vecs]
                + [pl.BlockSpec(w.shape, lambda b, t: (0, 0)),
                   pl.BlockSpec((1, d), lambda b, t: (0, 0)),
                   pl.BlockSpec((None, None, 1, d), seg)]
                + _row_specs(hs, ctx_tiles, t_off))
    return pl.pallas_call(
        functools.partial(body, ctx_tiles=ctx_tiles, t_off=t_off), grid=(bsz, n_t), in_specs=in_specs,
        out_specs=pl.BlockSpec((None, TM, d), lambda b, t: (b, t, 0)),
        out_shape=jax.ShapeDtypeStruct((bsz, out_rows, d), F32),
        compiler_params=_cparams("parallel", "arbitrary"),
        name="out_proj",
    )(*acts, *vecs, w, g, gate, *hs)


def _ffn_body(hm_ref, hp_ref, hn_ref, g2_ref, sh_ref, sc_ref, wg_ref, wu_ref, cw_ref, wd_ref, g3_ref, gate_ref,
              out_ref, gscr, uscr, *, nt, seg_starts):
    t = pl.program_id(1)
    prev_ok, nxt_ok = _halo_flags(t, nt, seg_starts)

    def normmod(h):
        return (_rms(h) * g2_ref[...]) * (1.0 + sc_ref[...]) + sh_ref[...]

    um = normmod(hm_ref[...])
    up = jnp.where(prev_ok, normmod(hp_ref[...]), 0.0)
    un = jnp.where(nxt_ok, normmod(hn_ref[...]), 0.0)
    ucat = jnp.concatenate([up, um, un], axis=0).astype(BF16)
    umb = um.astype(BF16)

    nchunk = wg_ref.shape[1] // FFN_CHUNK
    cols = lambda c: slice(c * FFN_CHUNK, (c + 1) * FFN_CHUNK)

    def front(c):
        gscr[c % 2] = _dot(ucat, wg_ref[:, cols(c)])
        uscr[c % 2] = _dot(umb, wu_ref[:, cols(c)])

    front(0)
    y = None
    for c in range(nchunk):
        if c + 1 < nchunk:
            front(c + 1)
        gbuf = gscr.at[c % 2]
        gc = (gbuf[pl.ds(HALO - 1, TM), :] * cw_ref[0:1, cols(c)] + gbuf[pl.ds(HALO, TM), :] * cw_ref[1:2, cols(c)]
              + gbuf[pl.ds(HALO + 1, TM), :] * cw_ref[2:3, cols(c)])
        hid = (_silu(gc) * uscr[c % 2]).astype(BF16)
        part = _dot(hid, wd_ref[cols(c), :])
        y = part if y is None else y + part
    _residual_out(y, hm_ref[...], g3_ref, gate_ref, out_ref)


def _ffn_call(h, g2, shift, scale, wg, wu, cw, wd, g3, gate, ctx_tiles, seg_off, seg_starts):
    bsz, s, d = h.shape
    n_t = s // TM
    const3 = lambda arr: pl.BlockSpec(arr.shape, lambda b, t: (0, 0))
    vec = pl.BlockSpec((1, d), lambda b, t: (0, 0))
    mod = pl.BlockSpec((None, None, 1, d), _seg_map(ctx_tiles, seg_off))
    return pl.pallas_call(
        functools.partial(_ffn_body, nt=n_t, seg_starts=seg_starts),
        grid=(bsz, n_t),
        in_specs=_halo_specs(d, s, 0) + [vec, mod, mod, const3(wg), const3(wu), const3(cw), const3(wd), vec, mod],
        out_specs=pl.BlockSpec((None, TM, d), lambda b, t: (b, t, 0)),
        out_shape=jax.ShapeDtypeStruct((bsz, s, d), F32),
        scratch_shapes=[pltpu.VMEM((2, TM + 2 * HALO, FFN_CHUNK), F32), pltpu.VMEM((2, TM, FFN_CHUNK), F32)],
        compiler_params=_cparams("parallel", "arbitrary"),
        name="conv_ffn",
    )(h, h, h, g2, shift, scale, wg, wu, cw, wd, g3, gate)


def _rope_tables(ctx_len, t):
    pos = jnp.arange(t)
    row = (pos // GRID_W).astype(F32)
    col = (pos % GRID_W).astype(F32)
    nf = HEAD_DIM // 4
    inv = ROPE_BASE ** (-jnp.arange(nf, dtype=F32) / nf)
    ar = row[:, None] * inv
    ac = col[:, None] * inv
    cos = jnp.concatenate([jnp.cos(ar), jnp.cos(ar), jnp.cos(ac), jnp.cos(ac)], axis=1)
    sin = jnp.concatenate([-jnp.sin(ar), jnp.sin(ar), -jnp.sin(ac), jnp.sin(ac)], axis=1)
    cos = jnp.concatenate([jnp.ones((ctx_len, HEAD_DIM), F32), cos], axis=0)
    sin = jnp.concatenate([jnp.zeros((ctx_len, HEAD_DIM), F32), sin], axis=0)
    return jnp.tile(cos, (1, LANES // HEAD_DIM)), jnp.tile(sin, (1, LANES // HEAD_DIM))


def _pad_cols(w, n):
    return jnp.pad(w, ((0, 0), (0, n - w.shape[1])))


def _gqa_perm():
    idx = [(h * A_GROUP + g) * HEAD_DIM + dd
           for g in range(A_GROUP) for h in range(A_KV_HEADS) for dd in range(HEAD_DIM)]
    return jnp.array(idx, dtype=jnp.int32)


def _ffn_weights(w_gate, w_up, conv, w_down):
    return w_gate.astype(BF16), w_up.astype(BF16), conv, w_down.astype(BF16)


def kernel(x, c, ctx, c_ctx, mod_w, mod_b, norm_g, ffn_w_gate, ffn_w_up, ffn_conv, ffn_w_down, ev_w_in, ev_w_out, a_sink, b_conv, b_gate_b, b_norm_g, od_w_in, od_w_out, c_lambda, c_norm_g, d_conv, d_conv_b, d_dt_bias, d_a_log, d_skip, d_norm_g):
    bsz, t, d = x.shape
    ctx_len = ctx.shape[1]
    depth = mod_w.shape[0]
    s = ctx_len + t
    half = d // 2
    assert ctx_len % TM == 0 and t % TM == 0 and t % GRID_W == 0 and ffn_w_gate.shape[-1] % FFN_CHUNK == 0
    n_t = s // TM
    ctx_tiles = ctx_len // TM

    mod_rows = -(-(bsz + 1) // HALO) * HALO
    cc = jnp.concatenate([c, c_ctx[None, :], jnp.zeros((mod_rows - bsz - 1, d), F32)], axis=0)
    mods = _mod_call(cc, mod_w, mod_b).reshape(depth, mod_rows, 6, d)

    def layer_mods(l):
        lat = mods[l, :bsz]
        cx = jnp.broadcast_to(mods[l, bsz][None], (bsz, 6, d))
        m = jnp.stack([cx, lat], axis=1)
        return [m[:, :, i][:, :, None, :] for i in range(6)]

    cos, sin = _rope_tables(ctx_len, t)
    hs = (ctx, x)

    for l in range(depth):
        last = l == depth - 1
        j = l // 2
        m = layer_mods(l)
        g = norm_g[l]
        wg, wu, cw, wd = _ffn_weights(ffn_w_gate[l], ffn_w_up[l], ffn_conv[l], ffn_w_down[l])
        if l % 2 == 0:
            w_in = ev_w_in[j]
            perm = _gqa_perm()
            akv = A_KV_HEADS * HEAD_DIM
            o1 = half + 2 * akv
            w_cat = jnp.concatenate([w_in[:, :half][:, perm], w_in[:, half:o1 + 3 * half],
                                     _pad_cols(w_in[:, o1 + 3 * half:], LANES)], axis=1).astype(BF16)
            groups = ((0, half, half, HEAD_DIM ** -0.5 * LOG2E), (half, 2 * akv, akv, 1.0),
                      (o1, half, 0, 1.0), (o1 + half, half, 0, 1.0), (o1 + 2 * half, half, 0, 1.0),
                      (o1 + 3 * half, LANES, 0, 1.0))
            q, kv, mqk, mv, mo, gates = _inproj_call(
                hs, g[0:1], m[0], m[1], cos, sin, w_cat, groups, (BF16, BF16, F32, BF16, F32, F32), ctx_tiles)
            ya = _win_attn_call(q, kv, a_sink[j], ctx_len)
            qkc = _conv_call(mqk, b_conv[j], jnp.zeros((1, half), F32), (0, ctx_tiles))
            hf, hb = _mlstm_call(qkc, mv, gates, _pad_cols(b_gate_b[j][None, :], LANES), ctx_len)
            w_out = ev_w_out[j]
            w_out = jnp.concatenate([w_out[:half][perm], w_out[half:]], axis=0).astype(BF16)
            acts, vecs, body = (ya, hf, hb, mo), (b_norm_g[j][None, :],), _outproj_even_body
        else:
            w_in = od_w_in[j]
            xbc_w = 2 * half
            w_cat = jnp.concatenate([w_in[:, :4 * half + xbc_w],
                                     _pad_cols(w_in[:, 4 * half + xbc_w:], LANES)], axis=1).astype(BF16)
            groups = ((0, half, half, HEAD_DIM ** -0.5 * LOG2E), (half, half, half, 1.0), (2 * half, half, 0, 1.0),
                      (3 * half, half, 0, 1.0), (4 * half, xbc_w, 0, 1.0), (4 * half + xbc_w, LANES, 0, 1.0))
            q, k, v, z, xbc, dt = _inproj_call(
                hs, g[0:1], m[0], m[1], cos, sin, w_cat, groups, (BF16, BF16, BF16, F32, F32, F32), ctx_tiles)
            lam_init = 0.8 - 0.6 * math.exp(-LAM_DEPTH_RATE * l)
            oa = _diff_attn_call(q, k, v, c_lambda[j], lam_init, ctx_len)
            xbcc = _conv_call(xbc, d_conv[j], d_conv_b[j][None, :], (0, ctx_tiles))
            dt_bias = _pad_cols(d_dt_bias[j].reshape(1, -1), LANES)
            neg_a = _pad_cols(-jnp.exp(d_a_log[j].astype(F32)).reshape(1, -1), LANES)
            skip = jnp.repeat(d_skip[j].astype(F32), HEAD_DIM)[None, :]
            yf, yb = _ssd_call(xbcc, dt, dt_bias, neg_a, skip, ctx_len)
            w_out = od_w_out[j].astype(BF16)
            acts, vecs = (oa, yf, yb, z), (c_norm_g[j][None, :], d_norm_g[j][None, :])
            body = functools.partial(_outproj_odd_body, out_scale=1.0 - lam_init)
        t_off = ctx_tiles if last else 0
        if l % 2 == 0:
            act_offs = (t_off,) * 4
        else:
            assert last, "differential attention is only computed for latent queries"
            act_offs = (0,) + (t_off,) * 3
        h_mid = _outproj_call(body, acts, act_offs, vecs, w_out, hs, g[1:2], m[2], ctx_tiles, t_off, n_t - t_off)
        seg_starts = (0,) if last else (0, ctx_tiles)
        hs = (_ffn_call(h_mid, g[2:3], m[3], m[4], wg, wu, cw, wd, g[3:4], m[5], ctx_tiles, t_off, seg_starts),)
    return hs[0]
```

```python
import functools
import math

import jax
import jax.numpy as jnp
from jax import lax
from jax.experimental import pallas as pl
from jax.experimental.pallas import tpu as pltpu

F32 = jnp.float32
BF16 = jnp.bfloat16

EPS = 1e-6
ROPE_BASE = 10000.0
GRID_W = 64
HEAD_DIM = 64
A_KV_HEADS = 2
A_GROUP = 4
A_WINDOW = 128
B_HEADS = 4
D_GROUPS = 2
LAM_DEPTH_RATE = 0.3

LANES = 128
MXU_COLS = 256
HALO = 8
TM = 256
CHUNK = 128
SCAN_ROWS = 2
FFN_CHUNK = 256
TQ = 256
WQ = 256
NEG = -1e30
LOG2E = math.log2(math.e)
VMEM_LIMIT = 56 * 1024 * 1024


def _cparams(*sem):
    return pltpu.CompilerParams(dimension_semantics=sem, vmem_limit_bytes=VMEM_LIMIT)


def _silu(x):
    return x * jax.nn.sigmoid(x)


def _softplus(x):
    return jnp.maximum(x, 0.0) + jnp.log1p(jnp.exp(-jnp.abs(x)))


def _log_sigmoid(x):
    return -_softplus(-x)


def _rms(x):
    return x * lax.rsqrt(jnp.mean(x * x, axis=-1, keepdims=True) + EPS)


def _dot(a, b):
    return jnp.dot(a, b, preferred_element_type=F32)


def _dot_nt(a, b):
    return lax.dot_general(a, b, (((1,), (1,)), ((), ())), preferred_element_type=F32)


def _dot_tn(a, b):
    return lax.dot_general(a, b, (((0,), (0,)), ((), ())), preferred_element_type=F32)


def _split3(x):
    x1 = x.astype(BF16)
    r1 = x - x1.astype(F32)
    x2 = r1.astype(BF16)
    x3 = (r1 - x2.astype(F32)).astype(BF16)
    return x1, x2, x3


def _cumsum_cols(tri, x):
    x1, x2, x3 = _split3(x)
    return _dot(tri, x1) + _dot(tri, x2) + _dot(tri, x3)


def _cumsum_rows(x, tri):
    x1, x2, x3 = _split3(x)
    return _dot_nt(x1, tri) + _dot_nt(x2, tri) + _dot_nt(x3, tri)


def _mod_body(c_ref, w_ref, b_ref, o_ref):
    a = _silu(c_ref[...]).astype(BF16)
    o_ref[...] = _dot(a, w_ref[...].astype(BF16)) + b_ref[...]


def _mod_call(cc, mod_w, mod_b):
    depth, d, n6 = mod_w.shape
    rows = cc.shape[0]
    tn = 1024
    return pl.pallas_call(
        _mod_body,
        grid=(depth, n6 // tn),
        in_specs=[pl.BlockSpec((rows, d), lambda l, j: (0, 0)),
                  pl.BlockSpec((None, d, tn), lambda l, j: (l, 0, j)),
                  pl.BlockSpec((None, 1, tn), lambda l, j: (l, 0, j))],
        out_specs=pl.BlockSpec((None, rows, tn), lambda l, j: (l, 0, j)),
        out_shape=jax.ShapeDtypeStruct((depth, rows, n6), F32),
        compiler_params=_cparams("arbitrary", "arbitrary"),
        name="modulation",
    )(cc, mod_w, mod_b.reshape(depth, 1, n6))


def _rope_block(y, cos, sin, lo):
    partner = jnp.where(lo, pltpu.roll(y, LANES - 16, 1), pltpu.roll(y, 16, 1))
    return y * cos + partner * sin


def _row_specs(hs, ctx_tiles, t_off):
    d = hs[0].shape[-1]
    if len(hs) == 1:
        return [pl.BlockSpec((None, TM, d), lambda b, t: (b, t + t_off, 0))]
    return [pl.BlockSpec((None, TM, d), lambda b, t: (b, jnp.minimum(t + t_off, ctx_tiles - 1), 0)),
            pl.BlockSpec((None, TM, d), lambda b, t: (b, jnp.maximum(t + t_off - ctx_tiles, 0), 0))]


def _load_rows(h_refs, ctx_tiles, t_off):
    if len(h_refs) == 1:
        return h_refs[0][...]
    return jnp.where(pl.program_id(1) + t_off < ctx_tiles, h_refs[0][...], h_refs[1][...])


def _halo_row_specs(hs, ctx_tiles):
    d = hs[0].shape[-1]
    per = TM // HALO
    offs = (0,) if len(hs) == 1 else (0, ctx_tiles)
    prevs, nexts = [], []
    for h_, off in zip(hs, offs):
        last = h_.shape[1] // HALO - 1
        prevs.append(pl.BlockSpec(
            (None, HALO, d), lambda b, t, off=off, last=last: (b, jnp.clip((t - off) * per - 1, 0, last), 0)))
        nexts.append(pl.BlockSpec(
            (None, HALO, d), lambda b, t, off=off, last=last: (b, jnp.clip((t - off + 1) * per, 0, last), 0)))
    return prevs + nexts


def _inproj_body(*refs, groups, conv_idx, n_src, ctx_tiles, nt, seg_starts):
    mains, prevs, nexts = refs[:n_src], refs[n_src:2 * n_src], refs[2 * n_src:3 * n_src]
    g_ref, sh_ref, sc_ref, cos_ref, sin_ref, w_ref, cw_ref, cb_ref = refs[3 * n_src:3 * n_src + 8]
    out_refs = refs[3 * n_src + 8:-2]
    scr, ubuf = refs[-2:]

    def normmod(h):
        return (_rms(h) * g_ref[...]) * (1.0 + sc_ref[...]) + sh_ref[...]

    ubuf[0:TM, :] = normmod(_load_rows(mains, ctx_tiles, 0)).astype(BF16)
    prev_ok, nxt_ok = _halo_flags(pl.program_id(1), nt, seg_starts)
    up = jnp.where(prev_ok, normmod(_load_rows(prevs, ctx_tiles, 0)), 0.0)
    un = jnp.where(nxt_ok, normmod(_load_rows(nexts, ctx_tiles, 0)), 0.0)
    ubuf[TM:TM + 2 * HALO, :] = jnp.concatenate([up, un], axis=0).astype(BF16)
    cos = cos_ref[...]
    sin = sin_ref[...]
    lane = lax.broadcasted_iota(jnp.int32, cos.shape, 1)
    lo = (lane & 31) < 16
    c_start, c_width = groups[conv_idx][:2]
    c_out = out_refs[conv_idx]
    taps = cw_ref.shape[0]
    for a in range(0, c_width, MXU_COLS):
        proj = _dot(ubuf[...], w_ref[:, c_start + a:c_start + a + MXU_COLS])
        scr[0:HALO, a:a + MXU_COLS] = proj[TM:TM + HALO]
        scr[HALO:HALO + TM, a:a + MXU_COLS] = proj[0:TM]
        scr[HALO + TM:, a:a + MXU_COLS] = proj[TM + HALO:]

    def conv_piece(a):
        cols = slice(a, a + MXU_COLS)
        y = scr[pl.ds(HALO - taps // 2, TM), cols] * cw_ref[0:1, cols]
        for j in range(1, taps):
            y = y + scr[pl.ds(HALO - taps // 2 + j, TM), cols] * cw_ref[j:j + 1, cols]
        c_out[:, cols] = _silu(y + cb_ref[:, cols]).astype(c_out.dtype)

    def project_piece(gi, a, step):
        start, _, rope_cols, qscale = groups[gi]
        y = _dot(ubuf[0:TM, :], w_ref[:, start + a:start + a + step])
        for r in range(0, step, LANES):
            yr = y[:, r:r + LANES]
            if a + r < rope_cols:
                yr = _rope_block(yr, cos, sin, lo)
            if qscale != 1.0:
                yr = yr * qscale
            out_refs[gi][:, a + r:a + r + LANES] = yr.astype(out_refs[gi].dtype)

    conv_pieces = [functools.partial(conv_piece, a) for a in range(0, c_width, MXU_COLS)]
    proj_pieces = []
    for gi, (_, width, rope_cols, _) in enumerate(groups):
        if gi != conv_idx:
            step = min(width, MXU_COLS if rope_cols else 2 * MXU_COLS)
            proj_pieces += [functools.partial(project_piece, gi, a, step) for a in range(0, width, step)]
    for i in range(max(len(conv_pieces), len(proj_pieces))):
        if i < len(proj_pieces):
            proj_pieces[i]()
        if i < len(conv_pieces):
            conv_pieces[i]()


def _seg_map(ctx_tiles, off):
    return lambda b, t: (b, jnp.where(t + off >= ctx_tiles, 1, 0), 0, 0)


def _inproj_call(hs, g, shift, scale, cos, sin, w, groups, dtypes, ctx_tiles, conv_idx, conv_w, conv_b):
    bsz, _, d = hs[0].shape
    s = sum(h_.shape[1] for h_ in hs)
    nt = s // TM
    seg = _seg_map(ctx_tiles, 0)
    const = lambda arr: pl.BlockSpec(arr.shape, lambda b, t: (0, 0))
    in_specs = _row_specs(hs, ctx_tiles, 0) + _halo_row_specs(hs, ctx_tiles) + [
                pl.BlockSpec((1, d), lambda b, t: (0, 0)),
                pl.BlockSpec((None, None, 1, d), seg),
                pl.BlockSpec((None, None, 1, d), seg),
                pl.BlockSpec((TM, LANES), lambda b, t: (t, 0)),
                pl.BlockSpec((TM, LANES), lambda b, t: (t, 0)),
                const(w), const(conv_w), const(conv_b)]
    out_specs = [pl.BlockSpec((None, TM, gr[1]), lambda b, t: (b, t, 0)) for gr in groups]
    out_shape = [jax.ShapeDtypeStruct((bsz, s, gr[1]), dt) for gr, dt in zip(groups, dtypes)]
    body = functools.partial(_inproj_body, groups=groups, conv_idx=conv_idx, n_src=len(hs), ctx_tiles=ctx_tiles,
                             nt=nt, seg_starts=(0, ctx_tiles))
    return pl.pallas_call(
        body, grid=(bsz, nt), in_specs=in_specs, out_specs=out_specs, out_shape=out_shape,
        scratch_shapes=[pltpu.VMEM((TM + 2 * HALO, groups[conv_idx][1]), F32),
                        pltpu.VMEM((TM + 2 * HALO, d), BF16)],
        compiler_params=_cparams("parallel", "arbitrary"),
        name="in_proj",
    )(*hs, *hs, *hs, g, shift, scale, cos, sin, w, conv_w, conv_b)


def _halo_flags(t, nt, seg_starts):
    prev_ok = t >= 0
    nxt_ok = (t + 1) < nt
    for s0 in seg_starts:
        prev_ok = jnp.logical_and(prev_ok, t != s0)
        nxt_ok = jnp.logical_and(nxt_ok, (t + 1) != s0)
    return prev_ok, nxt_ok


def _halo_specs(width, s, t_off):
    per = TM // HALO
    last = s // HALO - 1
    return [pl.BlockSpec((None, TM, width), lambda b, t: (b, t + t_off, 0)),
            pl.BlockSpec((None, HALO, width), lambda b, t: (b, jnp.maximum((t + t_off) * per - 1, 0), 0)),
            pl.BlockSpec((None, HALO, width), lambda b, t: (b, jnp.minimum((t + t_off + 1) * per, last), 0))]


def _win_attn_body(sink_ref, bias_ref, q_ref, kvp_ref, kvc_ref, kvn_ref, kvx_ref, o_ref):
    blk = q_ref.shape[0]
    kvcat = jnp.concatenate([kvp_ref[...], kvc_ref[...], kvn_ref[...], kvx_ref[...]], axis=0)
    kcat = kvcat[:, :LANES]
    vcat = kvcat[:, LANES:]
    nkeys = kvcat.shape[0]
    rows = A_GROUP * blk
    bias = bias_ref[...]
    lane = lax.broadcasted_iota(jnp.int32, (blk, LANES), 1)
    rowg = lax.broadcasted_iota(jnp.int32, (rows, 1), 0) // blk
    q = q_ref[...]

    def logits(h):
        half = (lane >= h * HEAD_DIM) & (lane < (h + 1) * HEAD_DIM)
        qh = jnp.concatenate(
            [jnp.where(half, q[:, g * LANES:(g + 1) * LANES], jnp.zeros((), BF16)) for g in range(A_GROUP)], axis=0)
        return _dot_nt(qh, kcat)

    raw = [logits(h) for h in range(A_KV_HEADS)]
    outs = []
    for h in range(A_KV_HEADS):
        s = (raw[h].reshape(A_GROUP, blk, nkeys) + bias[None]).reshape(rows, nkeys)
        sink = jnp.zeros((rows, 1), F32)
        for g in range(A_GROUP):
            sink = jnp.where(rowg == g, sink_ref[h * A_GROUP + g] * LOG2E, sink)
        m = jnp.maximum(jnp.max(s, axis=1, keepdims=True), sink)
        p = jnp.exp2(s - m)
        den = jnp.sum(p, axis=1, keepdims=True) + jnp.exp2(sink - m)
        outs.append(_dot(p.astype(BF16), vcat) / den)
    for g in range(A_GROUP):
        o_ref[:, g * LANES:(g + 1) * LANES] = jnp.where(
            lane < HEAD_DIM, outs[0][g * blk:(g + 1) * blk], outs[1][g * blk:(g + 1) * blk]).astype(o_ref.dtype)


def _win_attn_bias(ctx_len):
    qi = jnp.arange(WQ)[:, None]
    kj = jnp.arange(WQ + 2 * A_WINDOW)[None, :]
    band = jnp.abs(kj - A_WINDOW - qi) <= A_WINDOW
    variants = [jnp.zeros_like(band), band & (kj >= A_WINDOW), band, band & (kj < A_WINDOW + WQ)]
    win = jnp.stack([jnp.where(v_, 0.0, NEG) for v_ in variants]).astype(F32)
    return jnp.concatenate([win, jnp.zeros((4, WQ, ctx_len), F32)], axis=2)


def _win_attn_call(q, kv, sink, ctx_len):
    bsz, s, qw = q.shape
    kw = kv.shape[-1]
    nq = s // WQ
    cq = ctx_len // WQ
    per = WQ // A_WINDOW
    nblk = s // A_WINDOW
    assert ctx_len % WQ == 0 and nq - cq >= 2
    bias = _win_attn_bias(ctx_len)
    clamp = lambda j: jnp.clip(j, cq * per, nblk - 1)
    variant = lambda i: jnp.where(i < cq, 0, jnp.where(i == cq, 1, jnp.where(i == nq - 1, 3, 2)))
    kv_specs = [pl.BlockSpec((None, A_WINDOW, kw), lambda b, i: (b, clamp(i * per - 1), 0)),
                pl.BlockSpec((None, WQ, kw), lambda b, i: (b, i, 0)),
                pl.BlockSpec((None, A_WINDOW, kw), lambda b, i: (b, clamp((i + 1) * per), 0)),
                pl.BlockSpec((None, ctx_len, kw), lambda b, i: (b, 0, 0))]
    return pl.pallas_call(
        _win_attn_body,
        grid=(bsz, nq),
        in_specs=[pl.BlockSpec(memory_space=pltpu.SMEM),
                  pl.BlockSpec((None,) + bias.shape[1:], lambda b, i: (variant(i), 0, 0)),
                  pl.BlockSpec((None, WQ, qw), lambda b, i: (b, i, 0))] + kv_specs,
        out_specs=pl.BlockSpec((None, WQ, qw), lambda b, i: (b, i, 0)),
        out_shape=jax.ShapeDtypeStruct((bsz, s, qw), BF16),
        compiler_params=_cparams("parallel", "arbitrary"),
        name="window_attention",
    )(sink, bias, q, kv, kv, kv, kv)


def _bwd_chunk(j, ctx_chunks, n_chunks):
    return jnp.where(j < ctx_chunks, ctx_chunks - 1 - j, n_chunks + ctx_chunks - 1 - j)


def _scan_rows(dirs):
    return [(bb, d, tuple(r.at[bb] for r in refs)) for bb in range(SCAN_ROWS) for d, refs in enumerate(dirs)]


def _tri_masks(d):
    row = lax.broadcasted_iota(jnp.int32, (CHUNK, CHUNK), 0)
    col = lax.broadcasted_iota(jnp.int32, (CHUNK, CHUNK), 1)
    keep = (col <= row) if d == 0 else (col >= row)
    return keep, jnp.where(keep, 1.0, 0.0).astype(BF16)


def _mlstm_body(qkf_ref, vf_ref, gf_ref, qkb_ref, vb_ref, gb_ref, bias_ref, hf_ref, hb_ref, c_scr, m_scr):
    L = CHUNK
    dk = HEAD_DIM
    dv = LANES

    @pl.when(pl.program_id(1) == 0)
    def _():
        c_scr[...] = jnp.zeros_like(c_scr)
        m_scr[...] = jnp.zeros_like(m_scr)

    lane = lax.broadcasted_iota(jnp.int32, (L, LANES), 1)
    srow = lax.broadcasted_iota(jnp.int32, (LANES, 2 * dv), 0)
    ones_v = jnp.ones((L, dv), BF16)
    dirs = ((qkf_ref, vf_ref, gf_ref, hf_ref), (qkb_ref, vb_ref, gb_ref, hb_ref))
    for bb, d, (qk_ref, v_ref, g_ref, o_ref) in _scan_rows(dirs):
        keep, tri = _tri_masks(d)
        gates = g_ref[...] + bias_ref[...]
        gates_t = gates.T[0:16, :]
        b_col = _cumsum_cols(tri, _log_sigmoid(gates))
        b_row = _cumsum_rows(_log_sigmoid(gates_t), tri)
        k_pairs = [qk_ref[:, B_HEADS * dk + p * LANES:B_HEADS * dk + (p + 1) * LANES] for p in range(B_HEADS // 2)]
        kt_pairs = [kp.astype(F32).T.astype(BF16) for kp in k_pairs]
        for h in range(B_HEADS):
            ci = d * B_HEADS + h
            cf = 2 * B_HEADS + ci
            r = bb * 2 * B_HEADS + ci
            li_c = jnp.broadcast_to(gates[:, ci:ci + 1], (L, LANES))
            li_r = gates_t[ci:ci + 1, :]
            bc = jnp.broadcast_to(b_col[:, cf:cf + 1], (L, LANES))
            br = b_row[cf:cf + 1, :]
            b_last = bc[L - 1:L, :] if d == 0 else bc[0:1, :]
            m_prev = m_scr[r:r + 1, :]
            pair = (h // 2) * LANES
            half = (lane >= (h % 2) * dk) & (lane < (h % 2 + 1) * dk)
            q = jnp.where(half, qk_ref[:, pair:pair + LANES] * (dk ** -0.5), 0.0).astype(BF16)
            k = k_pairs[h // 2].astype(BF16)
            v = v_ref[:, h * dv:(h + 1) * dv]
            dm = jnp.where(keep, bc - br + li_r, NEG)
            g_in = bc + m_prev
            m_t = jnp.maximum(jnp.broadcast_to(jnp.max(dm, axis=1, keepdims=True), (L, LANES)), g_in)
            sm = (_dot_nt(q, k) * jnp.exp(dm - m_t)).astype(BF16)
            vaug = jnp.concatenate([v, ones_v], axis=1)
            c_prev = c_scr[r]
            intra = _dot(sm, vaug)
            inter = _dot(q, c_prev.astype(BF16))
            a_in = jnp.exp(g_in - m_t)
            den = jnp.maximum(jnp.abs(intra[:, dv:] + a_in * inter[:, dv:]), jnp.exp(-m_t))
            o_ref[:, h * dv:(h + 1) * dv] = (intra[:, :dv] + a_in * inter[:, :dv]) / den
            w_c = b_last - bc + li_c
            m_loc = jnp.max(w_c, axis=0, keepdims=True)
            e = jnp.exp(w_c - m_loc)
            ev = jnp.concatenate([e * v.astype(F32), e], axis=1).astype(BF16)
            c_loc = _dot(kt_pairs[h // 2], ev)
            own = (srow >= (h % 2) * dk) & (srow < (h % 2 + 1) * dk)
            m_new = jnp.maximum(b_last + m_prev, m_loc)
            keep_old = jnp.exp(b_last + m_prev - m_new)
            take_new = jnp.exp(m_loc - m_new)
            c_scr[r] = (jnp.concatenate([keep_old, keep_old], axis=1) * c_prev
                        + jnp.where(own, jnp.concatenate([take_new, take_new], axis=1) * c_loc, 0.0))
            m_scr[r:r + 1, :] = m_new


def _scan_specs(width, ctx_chunks, n_chunks):
    fwd = pl.BlockSpec((SCAN_ROWS, CHUNK, width), lambda b, j: (b, j, 0))
    bwd = pl.BlockSpec((SCAN_ROWS, CHUNK, width), lambda b, j: (b, _bwd_chunk(j, ctx_chunks, n_chunks), 0))
    return fwd, bwd


def _mlstm_call(qk, v, gates, bias, ctx_len):
    bsz, s, w = v.shape
    nc = s // CHUNK
    cc = ctx_len // CHUNK
    qf, qb = _scan_specs(qk.shape[-1], cc, nc)
    vf, vb = _scan_specs(w, cc, nc)
    gf, gb = _scan_specs(LANES, cc, nc)
    return pl.pallas_call(
        _mlstm_body,
        grid=(bsz // SCAN_ROWS, nc),
        in_specs=[qf, vf, gf, qb, vb, gb, pl.BlockSpec((1, LANES), lambda b, j: (0, 0))],
        out_specs=[vf, vb],
        out_shape=[jax.ShapeDtypeStruct((bsz, s, w), F32)] * 2,
        scratch_shapes=[pltpu.VMEM((SCAN_ROWS * 2 * B_HEADS, LANES, 2 * LANES), F32),
                        pltpu.VMEM((SCAN_ROWS * 2 * B_HEADS, LANES), F32)],
        compiler_params=_cparams("parallel", "arbitrary"),
        name="mlstm_scan",
    )(qk, v, gates, qk, v, gates, bias)


def _ssd_body(xf_ref, dtf_ref, xb_ref, dtb_ref, dtbias_ref, nega_ref, skip_ref, yf_ref, yb_ref, s_scr):
    L = CHUNK
    hd = HEAD_DIM
    inner = 4 * D_GROUPS * hd
    hpg = 4

    @pl.when(pl.program_id(1) == 0)
    def _():
        s_scr[...] = jnp.zeros_like(s_scr)

    lane = lax.broadcasted_iota(jnp.int32, (L, LANES), 1)
    lo = lane < hd
    lo1 = lo[0:1, :]
    dirs = ((xf_ref, dtf_ref, yf_ref), (xb_ref, dtb_ref, yb_ref))
    for bb, d, (x_ref, dt_ref, y_ref) in _scan_rows(dirs):
        keep, tri = _tri_masks(d)
        dt = _softplus(dt_ref[...] + dtbias_ref[...])
        a = dt * nega_ref[...]
        dt_t = dt.T[0:16, :]
        a_t = a.T[0:16, :]
        ac_col = _cumsum_cols(tri, a)
        ac_row = _cumsum_rows(a_t, tri)
        for g in range(D_GROUPS):
            bg_f = x_ref[:, inner + g * LANES:inner + (g + 1) * LANES]
            bg = bg_f.astype(BF16)
            bg_t = bg_f.T.astype(BF16)
            cg = x_ref[:, inner + (D_GROUPS + g) * LANES:inner + (D_GROUPS + g + 1) * LANES].astype(BF16)
            cb = _dot_nt(cg, bg)
            for pr in range(2):
                c0 = g * hpg * hd + pr * LANES
                xp = x_ref[:, c0:c0 + LANES]
                xpb = xp.astype(BF16)
                ys, eas, wsts, als = [], [], [], []
                for hh in range(2):
                    col = d * D_GROUPS * hpg + g * hpg + pr * 2 + hh
                    acc = jnp.broadcast_to(ac_col[:, col:col + 1], (L, LANES))
                    acr = ac_row[col:col + 1, :]
                    seg = jnp.where(keep, acc - acr, NEG)
                    mix = (cb * jnp.exp(seg) * dt_t[col:col + 1, :]).astype(BF16)
                    ys.append(_dot(mix, xpb))
                    a_last = acc[L - 1:L, :] if d == 0 else acc[0:1, :]
                    eas.append(jnp.exp(acc))
                    wsts.append(jnp.exp(a_last - acc) * jnp.broadcast_to(dt[:, col:col + 1], (L, LANES)))
                    als.append(jnp.exp(a_last))
                sidx = (bb * 2 + d) * 2 * D_GROUPS + g * 2 + pr
                st = s_scr[sidx]
                y = jnp.where(lo, ys[0], ys[1]) + _dot(cg, st.astype(BF16)) * jnp.where(lo, eas[0], eas[1])
                if d == 0:
                    y = y + skip_ref[:, c0:c0 + LANES] * xp
                y_ref[:, c0:c0 + LANES] = y
                xw = (xp * jnp.where(lo, wsts[0], wsts[1])).astype(BF16)
                s_scr[sidx] = jnp.where(lo1, als[0], als[1]) * st + _dot(bg_t, xw)


def _ssd_call(xbc, dt, dt_bias, neg_a, skip, ctx_len):
    bsz, s, w = xbc.shape
    nc = s // CHUNK
    cc = ctx_len // CHUNK
    inner = skip.shape[-1]
    xf, xb = _scan_specs(w, cc, nc)
    df, db = _scan_specs(LANES, cc, nc)
    yf, yb = _scan_specs(inner, cc, nc)
    vec = lambda n: pl.BlockSpec((1, n), lambda b, j: (0, 0))
    return pl.pallas_call(
        _ssd_body,
        grid=(bsz // SCAN_ROWS, nc),
        in_specs=[xf, df, xb, db, vec(LANES), vec(LANES), vec(inner)],
        out_specs=[yf, yb],
        out_shape=[jax.ShapeDtypeStruct((bsz, s, inner), F32)] * 2,
        scratch_shapes=[pltpu.VMEM((SCAN_ROWS * 4 * D_GROUPS, LANES, LANES), F32)],
        compiler_params=_cparams("parallel", "arbitrary"),
        name="ssd_scan",
    )(xbc, dt, xbc, dt, dt_bias, neg_a, skip)


def _diff_attn_body(lam_ref, q_ref, k_ref, v_ref, o_ref, *, lam_init):
    lv = lam_ref[...]
    lam = (jnp.exp(jnp.sum(lv[0:1] * lv[1:2], axis=1, keepdims=True))
           - jnp.exp(jnp.sum(lv[2:3] * lv[3:4], axis=1, keepdims=True)) + lam_init)
    tq = q_ref.shape[0]
    lane = lax.broadcasted_iota(jnp.int32, (tq, LANES), 1)
    nheads = q_ref.shape[1] // LANES

    def logits(h):
        q = q_ref[:, h * LANES:(h + 1) * LANES]
        k = k_ref[:, h * LANES:(h + 1) * LANES]
        return [_dot_nt(jnp.where((lane >= m * HEAD_DIM) & (lane < (m + 1) * HEAD_DIM), q, jnp.zeros((), BF16)), k)
                for m in range(2)]

    s_next = logits(0)
    for h in range(nheads):
        s_cur = s_next
        if h + 1 < nheads:
            s_next = logits(h + 1)
        ps, ls = [], []
        for s in s_cur:
            p = jnp.exp2(s - jnp.max(s, axis=1, keepdims=True))
            ps.append(p)
            ls.append(jnp.sum(p, axis=1, keepdims=True))
        a = (ps[0] - (lam * ls[0] / ls[1]) * ps[1]).astype(BF16)
        o_ref[:, h * LANES:(h + 1) * LANES] = _dot(a, v_ref[:, h * LANES:(h + 1) * LANES]) / ls[0]


def _diff_attn_call(q, k, v, lam_vecs, lam_init, ctx_len):
    bsz, s, w = q.shape
    t = s - ctx_len
    off = ctx_len // TQ
    return pl.pallas_call(
        functools.partial(_diff_attn_body, lam_init=lam_init),
        grid=(bsz, t // TQ),
        in_specs=[pl.BlockSpec(lam_vecs.shape, lambda b, i: (0, 0)),
                  pl.BlockSpec((None, TQ, w), lambda b, i: (b, i + off, 0)),
                  pl.BlockSpec((None, s, w), lambda b, i: (b, 0, 0)),
                  pl.BlockSpec((None, s, w), lambda b, i: (b, 0, 0))],
        out_specs=pl.BlockSpec((None, TQ, w), lambda b, i: (b, i, 0)),
        out_shape=jax.ShapeDtypeStruct((bsz, t, w), F32),
        compiler_params=_cparams("parallel", "arbitrary"),
        name="diff_attention",
    )(lam_vecs, q, k, v)


def _group_rms(x, width):
    return jnp.concatenate([_rms(x[:, a:a + width]) for a in range(0, x.shape[1], width)], axis=1)


def _residual_out(y, h, g_ref, gate_ref, out_ref):
    out_ref[...] = h + gate_ref[...] * (_rms(y) * g_ref[...])


def _outproj_even_body(ya_ref, hf_ref, hb_ref, og_ref, ng_ref, w_ref, g_ref, gate_ref, *tail, ctx_tiles, t_off):
    half = ya_ref.shape[1]
    hn = _group_rms(hf_ref[...] + hb_ref[...], LANES) * ng_ref[...] * jax.nn.sigmoid(og_ref[...])
    y = _dot(ya_ref[...], w_ref[0:half, :]) + _dot(hn.astype(BF16), w_ref[half:, :])
    _residual_out(y, _load_rows(tail[:-1], ctx_tiles, t_off), g_ref, gate_ref, tail[-1])


def _outproj_odd_body(oa_ref, yf_ref, yb_ref, z_ref, cg_ref, dg_ref, w_ref, g_ref, gate_ref, *tail,
                      out_scale, ctx_tiles, t_off):
    half = oa_ref.shape[1]
    on = _group_rms(oa_ref[...], LANES) * cg_ref[...] * out_scale
    yz = (yf_ref[...] + yb_ref[...]) * _silu(z_ref[...])
    sn = _group_rms(yz, half // D_GROUPS) * dg_ref[...]
    y = _dot(on.astype(BF16), w_ref[0:half, :]) + _dot(sn.astype(BF16), w_ref[half:, :])
    _residual_out(y, _load_rows(tail[:-1], ctx_tiles, t_off), g_ref, gate_ref, tail[-1])


def _outproj_call(body, acts, act_offs, vecs, w, hs, g, gate, ctx_tiles, t_off, n_t):
    bsz, _, d = hs[0].shape
    seg = _seg_map(ctx_tiles, t_off)
    out_rows = n_t * TM

    def act_spec(arr, off):
        return pl.BlockSpec((None, TM, arr.shape[-1]), lambda b, t: (b, t + off, 0))

    in_specs = ([act_spec(a_, o_) for a_, o_ in zip(acts, act_offs)]
                + [pl.BlockSpec(v_.shape, lambda b, t: (0, 0)) for v_ in vecs]
                + [pl.BlockSpec(w.shape, lambda b, t: (0, 0)),
                   pl.BlockSpec((1, d), lambda b, t: (0, 0)),
                   pl.BlockSpec((None, None, 1, d), seg)]
                + _row_specs(hs, ctx_tiles, t_off))
    return pl.pallas_call(
        functools.partial(body, ctx_tiles=ctx_tiles, t_off=t_off), grid=(bsz, n_t), in_specs=in_specs,
        out_specs=pl.BlockSpec((None, TM, d), lambda b, t: (b, t, 0)),
        out_shape=jax.ShapeDtypeStruct((bsz, out_rows, d), F32),
        compiler_params=_cparams("parallel", "arbitrary"),
        name="out_proj",
    )(*acts, *vecs, w, g, gate, *hs)


def _ffn_body(hm_ref, hp_ref, hn_ref, g2_ref, sh_ref, sc_ref, wg_ref, wu_ref, cw_ref, wd_ref, g3_ref, gate_ref,
              out_ref, gscr, uscr, *, nt, seg_starts):
    t = pl.program_id(1)
    prev_ok, nxt_ok = _halo_flags(t, nt, seg_starts)

    def normmod(h):
        return (_rms(h) * g2_ref[...]) * (1.0 + sc_ref[...]) + sh_ref[...]

    um = normmod(hm_ref[...])
    up = jnp.where(prev_ok, normmod(hp_ref[...]), 0.0)
    un = jnp.where(nxt_ok, normmod(hn_ref[...]), 0.0)
    ucat = jnp.concatenate([up, um, un], axis=0).astype(BF16)
    umb = um.astype(BF16)

    nchunk = wg_ref.shape[1] // FFN_CHUNK
    cols = lambda c: slice(c * FFN_CHUNK, (c + 1) * FFN_CHUNK)

    def front(c):
        gscr[c % 2] = _dot(ucat, wg_ref[:, cols(c)])
        uscr[c % 2] = _dot(umb, wu_ref[:, cols(c)])

    front(0)
    y = None
    for c in range(nchunk):
        if c + 1 < nchunk:
            front(c + 1)
        gbuf = gscr.at[c % 2]
        gc = (gbuf[pl.ds(HALO - 1, TM), :] * cw_ref[0:1, cols(c)] + gbuf[pl.ds(HALO, TM), :] * cw_ref[1:2, cols(c)]
              + gbuf[pl.ds(HALO + 1, TM), :] * cw_ref[2:3, cols(c)])
        hid = (_silu(gc) * uscr[c % 2]).astype(BF16)
        part = _dot(hid, wd_ref[cols(c), :])
        y = part if y is None else y + part
    _residual_out(y, hm_ref[...], g3_ref, gate_ref, out_ref)


def _ffn_call(h, g2, shift, scale, wg, wu, cw, wd, g3, gate, ctx_tiles, seg_off, seg_starts):
    bsz, s, d = h.shape
    n_t = s // TM
    const3 = lambda arr: pl.BlockSpec(arr.shape, lambda b, t: (0, 0))
    vec = pl.BlockSpec((1, d), lambda b, t: (0, 0))
    mod = pl.BlockSpec((None, None, 1, d), _seg_map(ctx_tiles, seg_off))
    return pl.pallas_call(
        functools.partial(_ffn_body, nt=n_t, seg_starts=seg_starts),
        grid=(bsz, n_t),
        in_specs=_halo_specs(d, s, 0) + [vec, mod, mod, const3(wg), const3(wu), const3(cw), const3(wd), vec, mod],
        out_specs=pl.BlockSpec((None, TM, d), lambda b, t: (b, t, 0)),
        out_shape=jax.ShapeDtypeStruct((bsz, s, d), F32),
        scratch_shapes=[pltpu.VMEM((2, TM + 2 * HALO, FFN_CHUNK), F32), pltpu.VMEM((2, TM, FFN_CHUNK), F32)],
        compiler_params=_cparams("parallel", "arbitrary"),
        name="conv_ffn",
    )(h, h, h, g2, shift, scale, wg, wu, cw, wd, g3, gate)


def _rope_tables(ctx_len, t):
    pos = jnp.arange(t)
    row = (pos // GRID_W).astype(F32)
    col = (pos % GRID_W).astype(F32)
    nf = HEAD_DIM // 4
    inv = ROPE_BASE ** (-jnp.arange(nf, dtype=F32) / nf)
    ar = row[:, None] * inv
    ac = col[:, None] * inv
    cos = jnp.concatenate([jnp.cos(ar), jnp.cos(ar), jnp.cos(ac), jnp.cos(ac)], axis=1)
    sin = jnp.concatenate([-jnp.sin(ar), jnp.sin(ar), -jnp.sin(ac), jnp.sin(ac)], axis=1)
    cos = jnp.concatenate([jnp.ones((ctx_len, HEAD_DIM), F32), cos], axis=0)
    sin = jnp.concatenate([jnp.zeros((ctx_len, HEAD_DIM), F32), sin], axis=0)
    return jnp.tile(cos, (1, LANES // HEAD_DIM)), jnp.tile(sin, (1, LANES // HEAD_DIM))


def _pad_cols(w, n):
    return jnp.pad(w, ((0, 0), (0, n - w.shape[1])))


def _gqa_perm():
    idx = [(h * A_GROUP + g) * HEAD_DIM + dd
           for g in range(A_GROUP) for h in range(A_KV_HEADS) for dd in range(HEAD_DIM)]
    return jnp.array(idx, dtype=jnp.int32)


def _ffn_weights(w_gate, w_up, conv, w_down):
    return w_gate.astype(BF16), w_up.astype(BF16), conv, w_down.astype(BF16)


def kernel(x, c, ctx, c_ctx, mod_w, mod_b, norm_g, ffn_w_gate, ffn_w_up, ffn_conv, ffn_w_down, ev_w_in, ev_w_out, a_sink, b_conv, b_gate_b, b_norm_g, od_w_in, od_w_out, c_lambda, c_norm_g, d_conv, d_conv_b, d_dt_bias, d_a_log, d_skip, d_norm_g):
    bsz, t, d = x.shape
    ctx_len = ctx.shape[1]
    depth = mod_w.shape[0]
    s = ctx_len + t
    half = d // 2
    assert ctx_len % TM == 0 and t % TM == 0 and t % GRID_W == 0 and ffn_w_gate.shape[-1] % FFN_CHUNK == 0
    n_t = s // TM
    ctx_tiles = ctx_len // TM

    mod_rows = -(-(bsz + 1) // HALO) * HALO
    cc = jnp.concatenate([c, c_ctx[None, :], jnp.zeros((mod_rows - bsz - 1, d), F32)], axis=0)
    mods = _mod_call(cc, mod_w, mod_b).reshape(depth, mod_rows, 6, d)

    def layer_mods(l):
        lat = mods[l, :bsz]
        cx = jnp.broadcast_to(mods[l, bsz][None], (bsz, 6, d))
        m = jnp.stack([cx, lat], axis=1)
        return [m[:, :, i][:, :, None, :] for i in range(6)]

    cos, sin = _rope_tables(ctx_len, t)
    hs = (ctx, x)

    for l in range(depth):
        last = l == depth - 1
        j = l // 2
        m = layer_mods(l)
        g = norm_g[l]
        wg, wu, cw, wd = _ffn_weights(ffn_w_gate[l], ffn_w_up[l], ffn_conv[l], ffn_w_down[l])
        if l % 2 == 0:
            w_in = ev_w_in[j]
            perm = _gqa_perm()
            akv = A_KV_HEADS * HEAD_DIM
            o1 = half + 2 * akv
            w_cat = jnp.concatenate([w_in[:, :half][:, perm], w_in[:, half:o1 + 3 * half],
                                     _pad_cols(w_in[:, o1 + 3 * half:], LANES)], axis=1).astype(BF16)
            groups = ((0, half, half, HEAD_DIM ** -0.5 * LOG2E), (half, 2 * akv, akv, 1.0),
                      (o1, half, 0, 1.0), (o1 + half, half, 0, 1.0), (o1 + 2 * half, half, 0, 1.0),
                      (o1 + 3 * half, LANES, 0, 1.0))
            q, kv, qkc, mv, mo, gates = _inproj_call(
                hs, g[0:1], m[0], m[1], cos, sin, w_cat, groups, (BF16, BF16, BF16, BF16, F32, F32), ctx_tiles,
                2, b_conv[j], jnp.zeros((1, half), F32))
            ya = _win_attn_call(q, kv, a_sink[j], ctx_len)
            hf, hb = _mlstm_call(qkc, mv, gates, _pad_cols(b_gate_b[j][None, :], LANES), ctx_len)
            w_out = ev_w_out[j]
            w_out = jnp.concatenate([w_out[:half][perm], w_out[half:]], axis=0).astype(BF16)
            acts, vecs, body = (ya, hf, hb, mo), (b_norm_g[j][None, :],), _outproj_even_body
        else:
            w_in = od_w_in[j]
            xbc_w = 2 * half
            w_cat = jnp.concatenate([w_in[:, :4 * half + xbc_w],
                                     _pad_cols(w_in[:, 4 * half + xbc_w:], LANES)], axis=1).astype(BF16)
            groups = ((0, half, half, HEAD_DIM ** -0.5 * LOG2E), (half, half, half, 1.0), (2 * half, half, 0, 1.0),
                      (3 * half, half, 0, 1.0), (4 * half, xbc_w, 0, 1.0), (4 * half + xbc_w, LANES, 0, 1.0))
            q, k, v, z, xbcc, dt = _inproj_call(
                hs, g[0:1], m[0], m[1], cos, sin, w_cat, groups, (BF16, BF16, BF16, F32, F32, F32), ctx_tiles,
                4, d_conv[j], d_conv_b[j][None, :])
            lam_init = 0.8 - 0.6 * math.exp(-LAM_DEPTH_RATE * l)
            oa = _diff_attn_call(q, k, v, c_lambda[j], lam_init, ctx_len)
            dt_bias = _pad_cols(d_dt_bias[j].reshape(1, -1), LANES)
            neg_a = _pad_cols(-jnp.exp(d_a_log[j].astype(F32)).reshape(1, -1), LANES)
            skip = jnp.repeat(d_skip[j].astype(F32), HEAD_DIM)[None, :]
            yf, yb = _ssd_call(xbcc, dt, dt_bias, neg_a, skip, ctx_len)
            w_out = od_w_out[j].astype(BF16)
            acts, vecs = (oa, yf, yb, z), (c_norm_g[j][None, :], d_norm_g[j][None, :])
            body = functools.partial(_outproj_odd_body, out_scale=1.0 - lam_init)
        t_off = ctx_tiles if last else 0
        if l % 2 == 0:
            act_offs = (t_off,) * 4
        else:
            assert last, "differential attention is only computed for latent queries"
            act_offs = (0,) + (t_off,) * 3
        h_mid = _outproj_call(body, acts, act_offs, vecs, w_out, hs, g[1:2], m[2], ctx_tiles, t_off, n_t - t_off)
        seg_starts = (0,) if last else (0, ctx_tiles)
        hs = (_ffn_call(h_mid, g[2:3], m[3], m[4], wg, wu, cw, wd, g[3:4], m[5], ctx_tiles, t_off, seg_starts),)
    return hs[0]
```

```python
import functools
import math

import jax
import jax.numpy as jnp
from jax import lax
from jax.experimental import pallas as pl
from jax.experimental.pallas import tpu as pltpu

F32 = jnp.float32
BF16 = jnp.bfloat16

EPS = 1e-6
ROPE_BASE = 10000.0
GRID_W = 64
HEAD_DIM = 64
A_KV_HEADS = 2
A_GROUP = 4
A_WINDOW = 128
B_HEADS = 4
D_GROUPS = 2
LAM_DEPTH_RATE = 0.3

LANES = 128
MXU_COLS = 256
HALO = 8
TM = 256
CHUNK = 128
SCAN_ROWS = 2
FFN_CHUNK = 256
TQ = 256
WQ = 256
NEG = -1e30
LOG2E = math.log2(math.e)
VMEM_LIMIT = 56 * 1024 * 1024


def _cparams(*sem):
    return pltpu.CompilerParams(dimension_semantics=sem, vmem_limit_bytes=VMEM_LIMIT)


def _sigmoid(x):
    return 0.5 * jnp.tanh(0.5 * x) + 0.5


def _silu(x):
    return x * _sigmoid(x)


def _softplus(x):
    return jnp.maximum(x, 0.0) + jnp.log(1.0 + jnp.exp(-jnp.abs(x)))


def _log_sigmoid(x):
    return -_softplus(-x)


def _rms(x):
    return x * lax.rsqrt(jnp.mean(x * x, axis=-1, keepdims=True) + EPS)


def _dot(a, b):
    return jnp.dot(a, b, preferred_element_type=F32)


def _dot_nt(a, b):
    return lax.dot_general(a, b, (((1,), (1,)), ((), ())), preferred_element_type=F32)


def _dot_tn(a, b):
    return lax.dot_general(a, b, (((0,), (0,)), ((), ())), preferred_element_type=F32)


def _split3(x):
    x1 = x.astype(BF16)
    r1 = x - x1.astype(F32)
    x2 = r1.astype(BF16)
    x3 = (r1 - x2.astype(F32)).astype(BF16)
    return x1, x2, x3


def _cumsum_cols(tri, x):
    x1, x2, x3 = _split3(x)
    return _dot(tri, x1) + _dot(tri, x2) + _dot(tri, x3)


def _cumsum_rows(x, tri):
    x1, x2, x3 = _split3(x)
    return _dot_nt(x1, tri) + _dot_nt(x2, tri) + _dot_nt(x3, tri)


def _mod_body(c_ref, w_ref, b_ref, o_ref):
    a = _silu(c_ref[...]).astype(BF16)
    o_ref[...] = _dot(a, w_ref[...].astype(BF16)) + b_ref[...]


def _mod_call(cc, mod_w, mod_b):
    depth, d, n6 = mod_w.shape
    rows = cc.shape[0]
    tn = 1024
    return pl.pallas_call(
        _mod_body,
        grid=(depth, n6 // tn),
        in_specs=[pl.BlockSpec((rows, d), lambda l, j: (0, 0)),
                  pl.BlockSpec((None, d, tn), lambda l, j: (l, 0, j)),
                  pl.BlockSpec((None, 1, tn), lambda l, j: (l, 0, j))],
        out_specs=pl.BlockSpec((None, rows, tn), lambda l, j: (l, 0, j)),
        out_shape=jax.ShapeDtypeStruct((depth, rows, n6), F32),
        compiler_params=_cparams("arbitrary", "arbitrary"),
        name="modulation",
    )(cc, mod_w, mod_b.reshape(depth, 1, n6))


def _rope_block(y, cos, sin, lo):
    partner = jnp.where(lo, pltpu.roll(y, LANES - 16, 1), pltpu.roll(y, 16, 1))
    return y * cos + partner * sin


def _row_specs(hs, ctx_tiles, t_off):
    d = hs[0].shape[-1]
    if len(hs) == 1:
        return [pl.BlockSpec((None, TM, d), lambda b, t: (b, t + t_off, 0))]
    return [pl.BlockSpec((None, TM, d), lambda b, t: (b, jnp.minimum(t + t_off, ctx_tiles - 1), 0)),
            pl.BlockSpec((None, TM, d), lambda b, t: (b, jnp.maximum(t + t_off - ctx_tiles, 0), 0))]


def _load_rows(h_refs, ctx_tiles, t_off):
    if len(h_refs) == 1:
        return h_refs[0][...]
    return jnp.where(pl.program_id(1) + t_off < ctx_tiles, h_refs[0][...], h_refs[1][...])


def _halo_row_specs(hs, ctx_tiles):
    d = hs[0].shape[-1]
    per = TM // HALO
    offs = (0,) if len(hs) == 1 else (0, ctx_tiles)
    prevs, nexts = [], []
    for h_, off in zip(hs, offs):
        last = h_.shape[1] // HALO - 1
        prevs.append(pl.BlockSpec(
            (None, HALO, d), lambda b, t, off=off, last=last: (b, jnp.clip((t - off) * per - 1, 0, last), 0)))
        nexts.append(pl.BlockSpec(
            (None, HALO, d), lambda b, t, off=off, last=last: (b, jnp.clip((t - off + 1) * per, 0, last), 0)))
    return prevs + nexts


def _inproj_body(*refs, groups, conv_idx, n_src, ctx_tiles, nt, seg_starts):
    mains, prevs, nexts = refs[:n_src], refs[n_src:2 * n_src], refs[2 * n_src:3 * n_src]
    g_ref, sh_ref, sc_ref, cos_ref, sin_ref, w_ref, cw_ref, cb_ref = refs[3 * n_src:3 * n_src + 8]
    out_refs = refs[3 * n_src + 8:-2]
    scr, ubuf = refs[-2:]

    def normmod(h):
        return (_rms(h) * g_ref[...]) * (1.0 + sc_ref[...]) + sh_ref[...]

    ubuf[0:TM, :] = normmod(_load_rows(mains, ctx_tiles, 0)).astype(BF16)
    prev_ok, nxt_ok = _halo_flags(pl.program_id(1), nt, seg_starts)
    up = jnp.where(prev_ok, normmod(_load_rows(prevs, ctx_tiles, 0)), 0.0)
    un = jnp.where(nxt_ok, normmod(_load_rows(nexts, ctx_tiles, 0)), 0.0)
    ubuf[TM:TM + 2 * HALO, :] = jnp.concatenate([up, un], axis=0).astype(BF16)
    cos = cos_ref[...]
    sin = sin_ref[...]
    lane = lax.broadcasted_iota(jnp.int32, cos.shape, 1)
    lo = (lane & 31) < 16
    c_start, c_width = groups[conv_idx][:2]
    c_out = out_refs[conv_idx]
    taps = cw_ref.shape[0]
    for a in range(0, c_width, MXU_COLS):
        proj = _dot(ubuf[...], w_ref[:, c_start + a:c_start + a + MXU_COLS])
        scr[0:HALO, a:a + MXU_COLS] = proj[TM:TM + HALO]
        scr[HALO:HALO + TM, a:a + MXU_COLS] = proj[0:TM]
        scr[HALO + TM:, a:a + MXU_COLS] = proj[TM + HALO:]

    def conv_piece(a):
        cols = slice(a, a + MXU_COLS)
        y = scr[pl.ds(HALO - taps // 2, TM), cols] * cw_ref[0:1, cols]
        for j in range(1, taps):
            y = y + scr[pl.ds(HALO - taps // 2 + j, TM), cols] * cw_ref[j:j + 1, cols]
        c_out[:, cols] = _silu(y + cb_ref[:, cols]).astype(c_out.dtype)

    def project_piece(gi, a, step):
        start, _, rope_cols, qscale = groups[gi]
        y = _dot(ubuf[0:TM, :], w_ref[:, start + a:start + a + step])
        for r in range(0, step, LANES):
            yr = y[:, r:r + LANES]
            if a + r < rope_cols:
                yr = _rope_block(yr, cos, sin, lo)
            if qscale != 1.0:
                yr = yr * qscale
            out_refs[gi][:, a + r:a + r + LANES] = yr.astype(out_refs[gi].dtype)

    conv_pieces = [functools.partial(conv_piece, a) for a in range(0, c_width, MXU_COLS)]
    proj_pieces = []
    for gi, (_, width, rope_cols, _) in enumerate(groups):
        if gi != conv_idx:
            step = min(width, MXU_COLS if rope_cols else 2 * MXU_COLS)
            proj_pieces += [functools.partial(project_piece, gi, a, step) for a in range(0, width, step)]
    for i in range(max(len(conv_pieces), len(proj_pieces))):
        if i < len(proj_pieces):
            proj_pieces[i]()
        if i < len(conv_pieces):
            conv_pieces[i]()


def _seg_map(ctx_tiles, off):
    return lambda b, t: (b, jnp.where(t + off >= ctx_tiles, 1, 0), 0, 0)


def _inproj_call(hs, g, shift, scale, cos, sin, w, groups, dtypes, ctx_tiles, conv_idx, conv_w, conv_b):
    bsz, _, d = hs[0].shape
    s = sum(h_.shape[1] for h_ in hs)
    nt = s // TM
    seg = _seg_map(ctx_tiles, 0)
    const = lambda arr: pl.BlockSpec(arr.shape, lambda b, t: (0, 0))
    in_specs = _row_specs(hs, ctx_tiles, 0) + _halo_row_specs(hs, ctx_tiles) + [
                pl.BlockSpec((1, d), lambda b, t: (0, 0)),
                pl.BlockSpec((None, None, 1, d), seg),
                pl.BlockSpec((None, None, 1, d), seg),
                pl.BlockSpec((TM, LANES), lambda b, t: (t, 0)),
                pl.BlockSpec((TM, LANES), lambda b, t: (t, 0)),
                const(w), const(conv_w), const(conv_b)]
    out_specs = [pl.BlockSpec((None, TM, gr[1]), lambda b, t: (b, t, 0)) for gr in groups]
    out_shape = [jax.ShapeDtypeStruct((bsz, s, gr[1]), dt) for gr, dt in zip(groups, dtypes)]
    body = functools.partial(_inproj_body, groups=groups, conv_idx=conv_idx, n_src=len(hs), ctx_tiles=ctx_tiles,
                             nt=nt, seg_starts=(0, ctx_tiles))
    return pl.pallas_call(
        body, grid=(bsz, nt), in_specs=in_specs, out_specs=out_specs, out_shape=out_shape,
        scratch_shapes=[pltpu.VMEM((TM + 2 * HALO, groups[conv_idx][1]), F32),
                        pltpu.VMEM((TM + 2 * HALO, d), BF16)],
        compiler_params=_cparams("parallel", "arbitrary"),
        name="in_proj",
    )(*hs, *hs, *hs, g, shift, scale, cos, sin, w, conv_w, conv_b)


def _halo_flags(t, nt, seg_starts):
    prev_ok = t >= 0
    nxt_ok = (t + 1) < nt
    for s0 in seg_starts:
        prev_ok = jnp.logical_and(prev_ok, t != s0)
        nxt_ok = jnp.logical_and(nxt_ok, (t + 1) != s0)
    return prev_ok, nxt_ok


def _halo_specs(width, s, t_off):
    per = TM // HALO
    last = s // HALO - 1
    return [pl.BlockSpec((None, TM, width), lambda b, t: (b, t + t_off, 0)),
            pl.BlockSpec((None, HALO, width), lambda b, t: (b, jnp.maximum((t + t_off) * per - 1, 0), 0)),
            pl.BlockSpec((None, HALO, width), lambda b, t: (b, jnp.minimum((t + t_off + 1) * per, last), 0))]


def _win_attn_body(sink_ref, bias_ref, q_ref, kvp_ref, kvc_ref, kvn_ref, kvx_ref, o_ref):
    blk = q_ref.shape[0]
    kvcat = jnp.concatenate([kvp_ref[...], kvc_ref[...], kvn_ref[...], kvx_ref[...]], axis=0)
    kcat = kvcat[:, :LANES]
    vcat = kvcat[:, LANES:]
    nkeys = kvcat.shape[0]
    rows = A_GROUP * blk
    bias = bias_ref[...]
    lane = lax.broadcasted_iota(jnp.int32, (blk, LANES), 1)
    rowg = lax.broadcasted_iota(jnp.int32, (rows, 1), 0) // blk
    q = q_ref[...]

    def logits(h):
        half = (lane >= h * HEAD_DIM) & (lane < (h + 1) * HEAD_DIM)
        qh = jnp.concatenate(
            [jnp.where(half, q[:, g * LANES:(g + 1) * LANES], jnp.zeros((), BF16)) for g in range(A_GROUP)], axis=0)
        return _dot_nt(qh, kcat)

    raw = [logits(h) for h in range(A_KV_HEADS)]
    outs = []
    for h in range(A_KV_HEADS):
        s = (raw[h].reshape(A_GROUP, blk, nkeys) + bias[None]).reshape(rows, nkeys)
        sink = jnp.zeros((rows, 1), F32)
        for g in range(A_GROUP):
            sink = jnp.where(rowg == g, sink_ref[h * A_GROUP + g] * LOG2E, sink)
        m = jnp.maximum(jnp.max(s, axis=1, keepdims=True), sink)
        p = jnp.exp2(s - m)
        den = jnp.sum(p, axis=1, keepdims=True) + jnp.exp2(sink - m)
        outs.append(_dot(p.astype(BF16), vcat) / den)
    for g in range(A_GROUP):
        o_ref[:, g * LANES:(g + 1) * LANES] = jnp.where(
            lane < HEAD_DIM, outs[0][g * blk:(g + 1) * blk], outs[1][g * blk:(g + 1) * blk]).astype(o_ref.dtype)


def _win_attn_bias(ctx_len):
    qi = jnp.arange(WQ)[:, None]
    kj = jnp.arange(WQ + 2 * A_WINDOW)[None, :]
    band = jnp.abs(kj - A_WINDOW - qi) <= A_WINDOW
    variants = [jnp.zeros_like(band), band & (kj >= A_WINDOW), band, band & (kj < A_WINDOW + WQ)]
    win = jnp.stack([jnp.where(v_, 0.0, NEG) for v_ in variants]).astype(F32)
    return jnp.concatenate([win, jnp.zeros((4, WQ, ctx_len), F32)], axis=2)


def _win_attn_call(q, kv, sink, ctx_len):
    bsz, s, qw = q.shape
    kw = kv.shape[-1]
    nq = s // WQ
    cq = ctx_len // WQ
    per = WQ // A_WINDOW
    nblk = s // A_WINDOW
    assert ctx_len % WQ == 0 and nq - cq >= 2
    bias = _win_attn_bias(ctx_len)
    clamp = lambda j: jnp.clip(j, cq * per, nblk - 1)
    variant = lambda i: jnp.where(i < cq, 0, jnp.where(i == cq, 1, jnp.where(i == nq - 1, 3, 2)))
    kv_specs = [pl.BlockSpec((None, A_WINDOW, kw), lambda b, i: (b, clamp(i * per - 1), 0)),
                pl.BlockSpec((None, WQ, kw), lambda b, i: (b, i, 0)),
                pl.BlockSpec((None, A_WINDOW, kw), lambda b, i: (b, clamp((i + 1) * per), 0)),
                pl.BlockSpec((None, ctx_len, kw), lambda b, i: (b, 0, 0))]
    return pl.pallas_call(
        _win_attn_body,
        grid=(bsz, nq),
        in_specs=[pl.BlockSpec(memory_space=pltpu.SMEM),
                  pl.BlockSpec((None,) + bias.shape[1:], lambda b, i: (variant(i), 0, 0)),
                  pl.BlockSpec((None, WQ, qw), lambda b, i: (b, i, 0))] + kv_specs,
        out_specs=pl.BlockSpec((None, WQ, qw), lambda b, i: (b, i, 0)),
        out_shape=jax.ShapeDtypeStruct((bsz, s, qw), BF16),
        compiler_params=_cparams("parallel", "arbitrary"),
        name="window_attention",
    )(sink, bias, q, kv, kv, kv, kv)


def _bwd_chunk(j, ctx_chunks, n_chunks):
    return jnp.where(j < ctx_chunks, ctx_chunks - 1 - j, n_chunks + ctx_chunks - 1 - j)


def _scan_rows(dirs):
    return [(bb, d, tuple(r.at[bb] for r in refs)) for bb in range(SCAN_ROWS) for d, refs in enumerate(dirs)]


def _tri_masks(d):
    row = lax.broadcasted_iota(jnp.int32, (CHUNK, CHUNK), 0)
    col = lax.broadcasted_iota(jnp.int32, (CHUNK, CHUNK), 1)
    keep = (col <= row) if d == 0 else (col >= row)
    return keep, jnp.where(keep, 1.0, 0.0).astype(BF16)


def _mlstm_body(qkf_ref, vf_ref, gf_ref, qkb_ref, vb_ref, gb_ref, bias_ref, hf_ref, hb_ref, c_scr, m_scr):
    L = CHUNK
    dk = HEAD_DIM
    dv = LANES

    @pl.when(pl.program_id(1) == 0)
    def _():
        c_scr[...] = jnp.zeros_like(c_scr)
        m_scr[...] = jnp.zeros_like(m_scr)

    lane = lax.broadcasted_iota(jnp.int32, (L, LANES), 1)
    srow = lax.broadcasted_iota(jnp.int32, (LANES, 2 * dv), 0)
    ones_v = jnp.ones((L, dv), BF16)
    dirs = ((qkf_ref, vf_ref, gf_ref, hf_ref), (qkb_ref, vb_ref, gb_ref, hb_ref))
    for bb, d, (qk_ref, v_ref, g_ref, o_ref) in _scan_rows(dirs):
        keep, tri = _tri_masks(d)
        gates = g_ref[...] + bias_ref[...]
        gates_t = gates.T[0:16, :]
        b_col = _cumsum_cols(tri, _log_sigmoid(gates))
        b_row = _cumsum_rows(_log_sigmoid(gates_t), tri)
        k_pairs = [qk_ref[:, B_HEADS * dk + p * LANES:B_HEADS * dk + (p + 1) * LANES] for p in range(B_HEADS // 2)]
        kt_pairs = [kp.astype(F32).T.astype(BF16) for kp in k_pairs]
        for h in range(B_HEADS):
            ci = d * B_HEADS + h
            cf = 2 * B_HEADS + ci
            r = bb * 2 * B_HEADS + ci
            li_c = jnp.broadcast_to(gates[:, ci:ci + 1], (L, LANES))
            li_r = gates_t[ci:ci + 1, :]
            bc = jnp.broadcast_to(b_col[:, cf:cf + 1], (L, LANES))
            br = b_row[cf:cf + 1, :]
            b_last = bc[L - 1:L, :] if d == 0 else bc[0:1, :]
            m_prev = m_scr[r:r + 1, :]
            pair = (h // 2) * LANES
            half = (lane >= (h % 2) * dk) & (lane < (h % 2 + 1) * dk)
            q = jnp.where(half, qk_ref[:, pair:pair + LANES] * (dk ** -0.5), 0.0).astype(BF16)
            k = k_pairs[h // 2].astype(BF16)
            v = v_ref[:, h * dv:(h + 1) * dv]
            dm = jnp.where(keep, bc - br + li_r, NEG)
            g_in = bc + m_prev
            m_t = jnp.maximum(jnp.broadcast_to(jnp.max(dm, axis=1, keepdims=True), (L, LANES)), g_in)
            sm = (_dot_nt(q, k) * jnp.exp(dm - m_t)).astype(BF16)
            vaug = jnp.concatenate([v, ones_v], axis=1)
            c_prev = c_scr[r]
            intra = _dot(sm, vaug)
            inter = _dot(q, c_prev.astype(BF16))
            a_in = jnp.exp(g_in - m_t)
            den = jnp.maximum(jnp.abs(intra[:, dv:] + a_in * inter[:, dv:]), jnp.exp(-m_t))
            o_ref[:, h * dv:(h + 1) * dv] = ((intra[:, :dv] + a_in * inter[:, :dv]) / den).astype(o_ref.dtype)
            w_c = b_last - bc + li_c
            m_loc = jnp.max(w_c, axis=0, keepdims=True)
            e = jnp.exp(w_c - m_loc)
            ev = jnp.concatenate([e * v.astype(F32), e], axis=1).astype(BF16)
            c_loc = _dot(kt_pairs[h // 2], ev)
            own = (srow >= (h % 2) * dk) & (srow < (h % 2 + 1) * dk)
            m_new = jnp.maximum(b_last + m_prev, m_loc)
            keep_old = jnp.exp(b_last + m_prev - m_new)
            take_new = jnp.exp(m_loc - m_new)
            c_scr[r] = (jnp.concatenate([keep_old, keep_old], axis=1) * c_prev
                        + jnp.where(own, jnp.concatenate([take_new, take_new], axis=1) * c_loc, 0.0))
            m_scr[r:r + 1, :] = m_new


def _scan_specs(width, ctx_chunks, n_chunks):
    fwd = pl.BlockSpec((SCAN_ROWS, CHUNK, width), lambda b, j: (b, j, 0))
    bwd = pl.BlockSpec((SCAN_ROWS, CHUNK, width), lambda b, j: (b, _bwd_chunk(j, ctx_chunks, n_chunks), 0))
    return fwd, bwd


def _mlstm_call(qk, v, gates, bias, ctx_len):
    bsz, s, w = v.shape
    nc = s // CHUNK
    cc = ctx_len // CHUNK
    qf, qb = _scan_specs(qk.shape[-1], cc, nc)
    vf, vb = _scan_specs(w, cc, nc)
    gf, gb = _scan_specs(LANES, cc, nc)
    return pl.pallas_call(
        _mlstm_body,
        grid=(bsz // SCAN_ROWS, nc),
        in_specs=[qf, vf, gf, qb, vb, gb, pl.BlockSpec((1, LANES), lambda b, j: (0, 0))],
        out_specs=[vf, vb],
        out_shape=[jax.ShapeDtypeStruct((bsz, s, w), BF16)] * 2,
        scratch_shapes=[pltpu.VMEM((SCAN_ROWS * 2 * B_HEADS, LANES, 2 * LANES), F32),
                        pltpu.VMEM((SCAN_ROWS * 2 * B_HEADS, LANES), F32)],
        compiler_params=_cparams("parallel", "arbitrary"),
        name="mlstm_scan",
    )(qk, v, gates, qk, v, gates, bias)


def _ssd_body(xf_ref, dtf_ref, xb_ref, dtb_ref, dtbias_ref, nega_ref, skip_ref, yf_ref, yb_ref, s_scr):
    L = CHUNK
    hd = HEAD_DIM
    inner = 4 * D_GROUPS * hd
    hpg = 4

    @pl.when(pl.program_id(1) == 0)
    def _():
        s_scr[...] = jnp.zeros_like(s_scr)

    lane = lax.broadcasted_iota(jnp.int32, (L, LANES), 1)
    lo = lane < hd
    lo1 = lo[0:1, :]
    dirs = ((xf_ref, dtf_ref, yf_ref), (xb_ref, dtb_ref, yb_ref))
    for bb, d, (x_ref, dt_ref, y_ref) in _scan_rows(dirs):
        keep, tri = _tri_masks(d)
        dt = _softplus(dt_ref[...] + dtbias_ref[...])
        a = dt * nega_ref[...]
        dt_t = dt.T[0:16, :]
        a_t = a.T[0:16, :]
        ac_col = _cumsum_cols(tri, a)
        ac_row = _cumsum_rows(a_t, tri)
        for g in range(D_GROUPS):
            bg_f = x_ref[:, inner + g * LANES:inner + (g + 1) * LANES]
            bg = bg_f.astype(BF16)
            bg_t = bg_f.T.astype(BF16)
            cg = x_ref[:, inner + (D_GROUPS + g) * LANES:inner + (D_GROUPS + g + 1) * LANES].astype(BF16)
            cb = _dot_nt(cg, bg)
            for pr in range(2):
                c0 = g * hpg * hd + pr * LANES
                xp = x_ref[:, c0:c0 + LANES]
                xpb = xp.astype(BF16)
                ys, eas, wsts, als = [], [], [], []
                for hh in range(2):
                    col = d * D_GROUPS * hpg + g * hpg + pr * 2 + hh
                    acc = jnp.broadcast_to(ac_col[:, col:col + 1], (L, LANES))
                    acr = ac_row[col:col + 1, :]
                    seg = jnp.where(keep, acc - acr, NEG)
                    mix = (cb * jnp.exp(seg) * dt_t[col:col + 1, :]).astype(BF16)
                    ys.append(_dot(mix, xpb))
                    a_last = acc[L - 1:L, :] if d == 0 else acc[0:1, :]
                    eas.append(jnp.exp(acc))
                    wsts.append(jnp.exp(a_last - acc) * jnp.broadcast_to(dt[:, col:col + 1], (L, LANES)))
                    als.append(jnp.exp(a_last))
                sidx = (bb * 2 + d) * 2 * D_GROUPS + g * 2 + pr
                st = s_scr[sidx]
                y = jnp.where(lo, ys[0], ys[1]) + _dot(cg, st.astype(BF16)) * jnp.where(lo, eas[0], eas[1])
                if d == 0:
                    y = y + skip_ref[:, c0:c0 + LANES] * xp
                y_ref[:, c0:c0 + LANES] = y.astype(y_ref.dtype)
                xw = (xp * jnp.where(lo, wsts[0], wsts[1])).astype(BF16)
                s_scr[sidx] = jnp.where(lo1, als[0], als[1]) * st + _dot(bg_t, xw)


def _ssd_call(xbc, dt, dt_bias, neg_a, skip, ctx_len):
    bsz, s, w = xbc.shape
    nc = s // CHUNK
    cc = ctx_len // CHUNK
    inner = skip.shape[-1]
    xf, xb = _scan_specs(w, cc, nc)
    df, db = _scan_specs(LANES, cc, nc)
    yf, yb = _scan_specs(inner, cc, nc)
    vec = lambda n: pl.BlockSpec((1, n), lambda b, j: (0, 0))
    return pl.pallas_call(
        _ssd_body,
        grid=(bsz // SCAN_ROWS, nc),
        in_specs=[xf, df, xb, db, vec(LANES), vec(LANES), vec(inner)],
        out_specs=[yf, yb],
        out_shape=[jax.ShapeDtypeStruct((bsz, s, inner), BF16)] * 2,
        scratch_shapes=[pltpu.VMEM((SCAN_ROWS * 4 * D_GROUPS, LANES, LANES), F32)],
        compiler_params=_cparams("parallel", "arbitrary"),
        name="ssd_scan",
    )(xbc, dt, xbc, dt, dt_bias, neg_a, skip)


def _diff_attn_body(lam_ref, q_ref, k_ref, v_ref, o_ref, *, lam_init):
    lv = lam_ref[...]
    lam = (jnp.exp(jnp.sum(lv[0:1] * lv[1:2], axis=1, keepdims=True))
           - jnp.exp(jnp.sum(lv[2:3] * lv[3:4], axis=1, keepdims=True)) + lam_init)
    tq = q_ref.shape[0]
    lane = lax.broadcasted_iota(jnp.int32, (tq, LANES), 1)
    nheads = q_ref.shape[1] // LANES

    def logits(h):
        q = q_ref[:, h * LANES:(h + 1) * LANES]
        k = k_ref[:, h * LANES:(h + 1) * LANES]
        return [_dot_nt(jnp.where((lane >= m * HEAD_DIM) & (lane < (m + 1) * HEAD_DIM), q, jnp.zeros((), BF16)), k)
                for m in range(2)]

    s_next = logits(0)
    for h in range(nheads):
        s_cur = s_next
        if h + 1 < nheads:
            s_next = logits(h + 1)
        ps, ls = [], []
        for s in s_cur:
            p = jnp.exp2(s - jnp.max(s, axis=1, keepdims=True))
            ps.append(p)
            ls.append(jnp.sum(p, axis=1, keepdims=True))
        a = (ps[0] - (lam * ls[0] / ls[1]) * ps[1]).astype(BF16)
        o_ref[:, h * LANES:(h + 1) * LANES] = _dot(a, v_ref[:, h * LANES:(h + 1) * LANES]) / ls[0]


def _diff_attn_call(q, k, v, lam_vecs, lam_init, ctx_len):
    bsz, s, w = q.shape
    t = s - ctx_len
    off = ctx_len // TQ
    return pl.pallas_call(
        functools.partial(_diff_attn_body, lam_init=lam_init),
        grid=(bsz, t // TQ),
        in_specs=[pl.BlockSpec(lam_vecs.shape, lambda b, i: (0, 0)),
                  pl.BlockSpec((None, TQ, w), lambda b, i: (b, i + off, 0)),
                  pl.BlockSpec((None, s, w), lambda b, i: (b, 0, 0)),
                  pl.BlockSpec((None, s, w), lambda b, i: (b, 0, 0))],
        out_specs=pl.BlockSpec((None, TQ, w), lambda b, i: (b, i, 0)),
        out_shape=jax.ShapeDtypeStruct((bsz, t, w), F32),
        compiler_params=_cparams("parallel", "arbitrary"),
        name="diff_attention",
    )(lam_vecs, q, k, v)


def _group_rms(x, width):
    return jnp.concatenate([_rms(x[:, a:a + width]) for a in range(0, x.shape[1], width)], axis=1)


def _residual_out(y, h, g_ref, gate_ref, out_ref):
    out_ref[...] = h + gate_ref[...] * (_rms(y) * g_ref[...])


def _outproj_even_body(ya_ref, hf_ref, hb_ref, og_ref, ng_ref, w_ref, g_ref, gate_ref, *tail, ctx_tiles, t_off):
    half = ya_ref.shape[1]
    f32 = lambda r: r[...].astype(F32)
    hn = _group_rms(f32(hf_ref) + f32(hb_ref), LANES) * ng_ref[...] * _sigmoid(f32(og_ref))
    y = _dot(ya_ref[...], w_ref[0:half, :]) + _dot(hn.astype(BF16), w_ref[half:, :])
    _residual_out(y, _load_rows(tail[:-1], ctx_tiles, t_off), g_ref, gate_ref, tail[-1])


def _outproj_odd_body(oa_ref, yf_ref, yb_ref, z_ref, cg_ref, dg_ref, w_ref, g_ref, gate_ref, *tail,
                      out_scale, ctx_tiles, t_off):
    half = oa_ref.shape[1]
    on = _group_rms(oa_ref[...], LANES) * cg_ref[...] * out_scale
    f32 = lambda r: r[...].astype(F32)
    yz = (f32(yf_ref) + f32(yb_ref)) * _silu(f32(z_ref))
    sn = _group_rms(yz, half // D_GROUPS) * dg_ref[...]
    y = _dot(on.astype(BF16), w_ref[0:half, :]) + _dot(sn.astype(BF16), w_ref[half:, :])
    _residual_out(y, _load_rows(tail[:-1], ctx_tiles, t_off), g_ref, gate_ref, tail[-1])


def _outproj_call(body, acts, act_offs, vecs, w, hs, g, gate, ctx_tiles, t_off, n_t):
    bsz, _, d = hs[0].shape
    seg = _seg_map(ctx_tiles, t_off)
    out_rows = n_t * TM

    def act_spec(arr, off):
        return pl.BlockSpec((None, TM, arr.shape[-1]), lambda b, t: (b, t + off, 0))

    in_specs = ([act_spec(a_, o_) for a_, o_ in zip(acts, act_offs)]
                + [pl.BlockSpec(v_.shape, lambda b, t: (0, 0)) for v_ in vecs]
                + [pl.BlockSpec(w.shape, lambda b, t: (0, 0)),
                   pl.BlockSpec((1, d), lambda b, t: (0, 0)),
                   pl.BlockSpec((None, None, 1, d), seg)]
                + _row_specs(hs, ctx_tiles, t_off))
    return pl.pallas_call(
        functools.partial(body, ctx_tiles=ctx_tiles, t_off=t_off), grid=(bsz, n_t), in_specs=in_specs,
        out_specs=pl.BlockSpec((None, TM, d), lambda b, t: (b, t, 0)),
        out_shape=jax.ShapeDtypeStruct((bsz, out_rows, d), F32),
        compiler_params=_cparams("parallel", "arbitrary"),
        name="out_proj",
    )(*acts, *vecs, w, g, gate, *hs)


def _ffn_body(hm_ref, hp_ref, hn_ref, g2_ref, sh_ref, sc_ref, wg_ref, wu_ref, cw_ref, wd_ref, g3_ref, gate_ref,
              out_ref, gscr, uscr, *, nt, seg_starts):
    t = pl.program_id(1)
    prev_ok, nxt_ok = _halo_flags(t, nt, seg_starts)

    def normmod(h):
        return (_rms(h) * g2_ref[...]) * (1.0 + sc_ref[...]) + sh_ref[...]

    um = normmod(hm_ref[...])
    up = jnp.where(prev_ok, normmod(hp_ref[...]), 0.0)
    un = jnp.where(nxt_ok, normmod(hn_ref[...]), 0.0)
    ucat = jnp.concatenate([up, um, un], axis=0).astype(BF16)
    umb = um.astype(BF16)

    nchunk = wg_ref.shape[1] // FFN_CHUNK
    cols = lambda c: slice(c * FFN_CHUNK, (c + 1) * FFN_CHUNK)

    def front(c):
        gscr[c % 2] = _dot(ucat, wg_ref[:, cols(c)])
        uscr[c % 2] = _dot(umb, wu_ref[:, cols(c)])

    front(0)
    y = None
    for c in range(nchunk):
        if c + 1 < nchunk:
            front(c + 1)
        gbuf = gscr.at[c % 2]
        gc = (gbuf[pl.ds(HALO - 1, TM), :] * cw_ref[0:1, cols(c)] + gbuf[pl.ds(HALO, TM), :] * cw_ref[1:2, cols(c)]
              + gbuf[pl.ds(HALO + 1, TM), :] * cw_ref[2:3, cols(c)])
        hid = (_silu(gc) * uscr[c % 2]).astype(BF16)
        part = _dot(hid, wd_ref[cols(c), :])
        y = part if y is None else y + part
    _residual_out(y, hm_ref[...], g3_ref, gate_ref, out_ref)


def _ffn_call(h, g2, shift, scale, wg, wu, cw, wd, g3, gate, ctx_tiles, seg_off, seg_starts):
    bsz, s, d = h.shape
    n_t = s // TM
    const3 = lambda arr: pl.BlockSpec(arr.shape, lambda b, t: (0, 0))
    vec = pl.BlockSpec((1, d), lambda b, t: (0, 0))
    mod = pl.BlockSpec((None, None, 1, d), _seg_map(ctx_tiles, seg_off))
    return pl.pallas_call(
        functools.partial(_ffn_body, nt=n_t, seg_starts=seg_starts),
        grid=(bsz, n_t),
        in_specs=_halo_specs(d, s, 0) + [vec, mod, mod, const3(wg), const3(wu), const3(cw), const3(wd), vec, mod],
        out_specs=pl.BlockSpec((None, TM, d), lambda b, t: (b, t, 0)),
        out_shape=jax.ShapeDtypeStruct((bsz, s, d), F32),
        scratch_shapes=[pltpu.VMEM((2, TM + 2 * HALO, FFN_CHUNK), F32), pltpu.VMEM((2, TM, FFN_CHUNK), F32)],
        compiler_params=_cparams("parallel", "arbitrary"),
        name="conv_ffn",
    )(h, h, h, g2, shift, scale, wg, wu, cw, wd, g3, gate)


def _rope_tables(ctx_len, t):
    pos = jnp.arange(t)
    row = (pos // GRID_W).astype(F32)
    col = (pos % GRID_W).astype(F32)
    nf = HEAD_DIM // 4
    inv = ROPE_BASE ** (-jnp.arange(nf, dtype=F32) / nf)
    ar = row[:, None] * inv
    ac = col[:, None] * inv
    cos = jnp.concatenate([jnp.cos(ar), jnp.cos(ar), jnp.cos(ac), jnp.cos(ac)], axis=1)
    sin = jnp.concatenate([-jnp.sin(ar), jnp.sin(ar), -jnp.sin(ac), jnp.sin(ac)], axis=1)
    cos = jnp.concatenate([jnp.ones((ctx_len, HEAD_DIM), F32), cos], axis=0)
    sin = jnp.concatenate([jnp.zeros((ctx_len, HEAD_DIM), F32), sin], axis=0)
    return jnp.tile(cos, (1, LANES // HEAD_DIM)), jnp.tile(sin, (1, LANES // HEAD_DIM))


def _pad_cols(w, n):
    return jnp.pad(w, ((0, 0), (0, n - w.shape[1])))


def _gqa_perm():
    idx = [(h * A_GROUP + g) * HEAD_DIM + dd
           for g in range(A_GROUP) for h in range(A_KV_HEADS) for dd in range(HEAD_DIM)]
    return jnp.array(idx, dtype=jnp.int32)


def _ffn_weights(w_gate, w_up, conv, w_down):
    return w_gate.astype(BF16), w_up.astype(BF16), conv, w_down.astype(BF16)


def kernel(x, c, ctx, c_ctx, mod_w, mod_b, norm_g, ffn_w_gate, ffn_w_up, ffn_conv, ffn_w_down, ev_w_in, ev_w_out, a_sink, b_conv, b_gate_b, b_norm_g, od_w_in, od_w_out, c_lambda, c_norm_g, d_conv, d_conv_b, d_dt_bias, d_a_log, d_skip, d_norm_g):
    bsz, t, d = x.shape
    ctx_len = ctx.shape[1]
    depth = mod_w.shape[0]
    s = ctx_len + t
    half = d // 2
    assert ctx_len % TM == 0 and t % TM == 0 and t % GRID_W == 0 and ffn_w_gate.shape[-1] % FFN_CHUNK == 0
    n_t = s // TM
    ctx_tiles = ctx_len // TM

    mod_rows = -(-(bsz + 1) // HALO) * HALO
    cc = jnp.concatenate([c, c_ctx[None, :], jnp.zeros((mod_rows - bsz - 1, d), F32)], axis=0)
    mods = _mod_call(cc, mod_w, mod_b).reshape(depth, mod_rows, 6, d)

    def layer_mods(l):
        lat = mods[l, :bsz]
        cx = jnp.broadcast_to(mods[l, bsz][None], (bsz, 6, d))
        m = jnp.stack([cx, lat], axis=1)
        return [m[:, :, i][:, :, None, :] for i in range(6)]

    cos, sin = _rope_tables(ctx_len, t)
    hs = (ctx, x)

    for l in range(depth):
        last = l == depth - 1
        j = l // 2
        m = layer_mods(l)
        g = norm_g[l]
        wg, wu, cw, wd = _ffn_weights(ffn_w_gate[l], ffn_w_up[l], ffn_conv[l], ffn_w_down[l])
        if l % 2 == 0:
            w_in = ev_w_in[j]
            perm = _gqa_perm()
            akv = A_KV_HEADS * HEAD_DIM
            o1 = half + 2 * akv
            w_cat = jnp.concatenate([w_in[:, :half][:, perm], w_in[:, half:o1 + 3 * half],
                                     _pad_cols(w_in[:, o1 + 3 * half:], LANES)], axis=1).astype(BF16)
            groups = ((0, half, half, HEAD_DIM ** -0.5 * LOG2E), (half, 2 * akv, akv, 1.0),
                      (o1, half, 0, 1.0), (o1 + half, half, 0, 1.0), (o1 + 2 * half, half, 0, 1.0),
                      (o1 + 3 * half, LANES, 0, 1.0))
            q, kv, qkc, mv, mo, gates = _inproj_call(
                hs, g[0:1], m[0], m[1], cos, sin, w_cat, groups, (BF16, BF16, BF16, BF16, BF16, F32), ctx_tiles,
                2, b_conv[j], jnp.zeros((1, half), F32))
            ya = _win_attn_call(q, kv, a_sink[j], ctx_len)
            hf, hb = _mlstm_call(qkc, mv, gates, _pad_cols(b_gate_b[j][None, :], LANES), ctx_len)
            w_out = ev_w_out[j]
            w_out = jnp.concatenate([w_out[:half][perm], w_out[half:]], axis=0).astype(BF16)
            acts, vecs, body = (ya, hf, hb, mo), (b_norm_g[j][None, :],), _outproj_even_body
        else:
            w_in = od_w_in[j]
            xbc_w = 2 * half
            w_cat = jnp.concatenate([w_in[:, :4 * half + xbc_w],
                                     _pad_cols(w_in[:, 4 * half + xbc_w:], LANES)], axis=1).astype(BF16)
            groups = ((0, half, half, HEAD_DIM ** -0.5 * LOG2E), (half, half, half, 1.0), (2 * half, half, 0, 1.0),
                      (3 * half, half, 0, 1.0), (4 * half, xbc_w, 0, 1.0), (4 * half + xbc_w, LANES, 0, 1.0))
            q, k, v, z, xbcc, dt = _inproj_call(
                hs, g[0:1], m[0], m[1], cos, sin, w_cat, groups, (BF16, BF16, BF16, BF16, F32, F32), ctx_tiles,
                4, d_conv[j], d_conv_b[j][None, :])
            lam_init = 0.8 - 0.6 * math.exp(-LAM_DEPTH_RATE * l)
            oa = _diff_attn_call(q, k, v, c_lambda[j], lam_init, ctx_len)
            dt_bias = _pad_cols(d_dt_bias[j].reshape(1, -1), LANES)
            neg_a = _pad_cols(-jnp.exp(d_a_log[j].astype(F32)).reshape(1, -1), LANES)
            skip = jnp.repeat(d_skip[j].astype(F32), HEAD_DIM)[None, :]
            yf, yb = _ssd_call(xbcc, dt, dt_bias, neg_a, skip, ctx_len)
            w_out = od_w_out[j].astype(BF16)
            acts, vecs = (oa, yf, yb, z), (c_norm_g[j][None, :], d_norm_g[j][None, :])
            body = functools.partial(_outproj_odd_body, out_scale=1.0 - lam_init)
        t_off = ctx_tiles if last else 0
        if l % 2 == 0:
            act_offs = (t_off,) * 4
        else:
            assert last, "differential attention is only computed for latent queries"
            act_offs = (0,) + (t_off,) * 3
        h_mid = _outproj_call(body, acts, act_offs, vecs, w_out, hs, g[1:2], m[2], ctx_tiles, t_off, n_t - t_off)
        seg_starts = (0,) if last else (0, ctx_tiles)
        hs = (_ffn_call(h_mid, g[2:3], m[3], m[4], wg, wu, cw, wd, g[3:4], m[5], ctx_tiles, t_off, seg_starts),)
    return hs[0]
```

```python
import functools
import math

import jax
import jax.numpy as jnp
from jax import lax
from jax.experimental import pallas as pl
from jax.experimental.pallas import tpu as pltpu

F32 = jnp.float32
BF16 = jnp.bfloat16

EPS = 1e-6
ROPE_BASE = 10000.0
GRID_W = 64
HEAD_DIM = 64
A_KV_HEADS = 2
A_GROUP = 4
A_WINDOW = 128
B_HEADS = 4
D_GROUPS = 2
LAM_DEPTH_RATE = 0.3

LANES = 128
MXU_COLS = 256
HALO = 8
Y_HALO = 16
TM = 256
CHUNK = 128
SCAN_ROWS = 2
FFN_CHUNK = 256
TQ = 256
WQ = 256
NEG = -1e30
LOG2E = math.log2(math.e)
VMEM_LIMIT = 56 * 1024 * 1024


def _cparams(*sem):
    return pltpu.CompilerParams(dimension_semantics=sem, vmem_limit_bytes=VMEM_LIMIT)


def _sigmoid(x):
    return 0.5 * jnp.tanh(0.5 * x) + 0.5


def _silu(x):
    return x * _sigmoid(x)


def _softplus(x):
    return jnp.maximum(x, 0.0) + jnp.log(1.0 + jnp.exp(-jnp.abs(x)))


def _log_sigmoid(x):
    return -_softplus(-x)


def _rms(x):
    return x * lax.rsqrt(jnp.mean(x * x, axis=-1, keepdims=True) + EPS)


def _dot(a, b):
    return jnp.dot(a, b, preferred_element_type=F32)


def _dot_nt(a, b):
    return lax.dot_general(a, b, (((1,), (1,)), ((), ())), preferred_element_type=F32)


def _dot_tn(a, b):
    return lax.dot_general(a, b, (((0,), (0,)), ((), ())), preferred_element_type=F32)


def _split3(x):
    x1 = x.astype(BF16)
    r1 = x - x1.astype(F32)
    x2 = r1.astype(BF16)
    x3 = (r1 - x2.astype(F32)).astype(BF16)
    return x1, x2, x3


def _cumsum_cols(tri, x):
    x1, x2, x3 = _split3(x)
    return _dot(tri, x1) + _dot(tri, x2) + _dot(tri, x3)


def _cumsum_rows(x, tri):
    x1, x2, x3 = _split3(x)
    return _dot_nt(x1, tri) + _dot_nt(x2, tri) + _dot_nt(x3, tri)


def _mod_body(c_ref, w_ref, b_ref, o_ref):
    a = _silu(c_ref[...]).astype(BF16)
    o_ref[...] = _dot(a, w_ref[...].astype(BF16)) + b_ref[...]


def _mod_call(cc, mod_w, mod_b):
    depth, d, n6 = mod_w.shape
    rows = cc.shape[0]
    tn = 1024
    return pl.pallas_call(
        _mod_body,
        grid=(depth, n6 // tn),
        in_specs=[pl.BlockSpec((rows, d), lambda l, j: (0, 0)),
                  pl.BlockSpec((None, d, tn), lambda l, j: (l, 0, j)),
                  pl.BlockSpec((None, 1, tn), lambda l, j: (l, 0, j))],
        out_specs=pl.BlockSpec((None, rows, tn), lambda l, j: (l, 0, j)),
        out_shape=jax.ShapeDtypeStruct((depth, rows, n6), F32),
        compiler_params=_cparams("arbitrary", "arbitrary"),
        name="modulation",
    )(cc, mod_w, mod_b.reshape(depth, 1, n6))


def _rope_block(y, cos, sin, lo):
    partner = jnp.where(lo, pltpu.roll(y, LANES - 16, 1), pltpu.roll(y, 16, 1))
    return y * cos + partner * sin


def _row_specs(hs, ctx_tiles, t_off):
    d = hs[0].shape[-1]
    if len(hs) == 1:
        return [pl.BlockSpec((None, TM, d), lambda b, t: (b, t + t_off, 0))]
    return [pl.BlockSpec((None, TM, d), lambda b, t: (b, jnp.minimum(t + t_off, ctx_tiles - 1), 0)),
            pl.BlockSpec((None, TM, d), lambda b, t: (b, jnp.maximum(t + t_off - ctx_tiles, 0), 0))]


def _load_rows(h_refs, ctx_tiles, t_off):
    if len(h_refs) == 1:
        return h_refs[0][...]
    return jnp.where(pl.program_id(1) + t_off < ctx_tiles, h_refs[0][...], h_refs[1][...])


def _halo_row_specs(hs, ctx_tiles, t_off=0):
    d = hs[0].shape[-1]
    per = TM // HALO
    offs = (0,) if len(hs) == 1 else (0, ctx_tiles)
    prevs, nexts = [], []
    for h_, off in zip(hs, offs):
        last = h_.shape[1] // HALO - 1
        prevs.append(pl.BlockSpec(
            (None, HALO, d), lambda b, t, off=off, last=last: (b, jnp.clip((t + t_off - off) * per - 1, 0, last), 0)))
        nexts.append(pl.BlockSpec(
            (None, HALO, d), lambda b, t, off=off, last=last: (b, jnp.clip((t + t_off - off + 1) * per, 0, last), 0)))
    return prevs + nexts


def _inproj_body(*refs, groups, conv_idx, n_src, ctx_tiles, nt, seg_starts):
    mains, prevs, nexts = refs[:n_src], refs[n_src:2 * n_src], refs[2 * n_src:3 * n_src]
    g_ref, sh_ref, sc_ref, cos_ref, sin_ref, w_ref, cw_ref, cb_ref = refs[3 * n_src:3 * n_src + 8]
    out_refs = refs[3 * n_src + 8:-2]
    scr, ubuf = refs[-2:]

    def normmod(h):
        return (_rms(h) * g_ref[...]) * (1.0 + sc_ref[...]) + sh_ref[...]

    ubuf[0:TM, :] = normmod(_load_rows(mains, ctx_tiles, 0)).astype(BF16)
    prev_ok, nxt_ok = _halo_flags(pl.program_id(1), nt, seg_starts)
    up = jnp.where(prev_ok, normmod(_load_rows(prevs, ctx_tiles, 0)), 0.0)
    un = jnp.where(nxt_ok, normmod(_load_rows(nexts, ctx_tiles, 0)), 0.0)
    ubuf[TM:TM + 2 * HALO, :] = jnp.concatenate([up, un], axis=0).astype(BF16)
    cos = cos_ref[...]
    sin = sin_ref[...]
    lane = lax.broadcasted_iota(jnp.int32, cos.shape, 1)
    lo = (lane & 31) < 16
    c_start, c_width = groups[conv_idx][:2]
    c_out = out_refs[conv_idx]
    taps = cw_ref.shape[0]
    for a in range(0, c_width, MXU_COLS):
        proj = _dot(ubuf[...], w_ref[:, c_start + a:c_start + a + MXU_COLS])
        scr[0:HALO, a:a + MXU_COLS] = proj[TM:TM + HALO]
        scr[HALO:HALO + TM, a:a + MXU_COLS] = proj[0:TM]
        scr[HALO + TM:, a:a + MXU_COLS] = proj[TM + HALO:]

    def conv_piece(a):
        cols = slice(a, a + MXU_COLS)
        y = scr[pl.ds(HALO - taps // 2, TM), cols] * cw_ref[0:1, cols]
        for j in range(1, taps):
            y = y + scr[pl.ds(HALO - taps // 2 + j, TM), cols] * cw_ref[j:j + 1, cols]
        c_out[:, cols] = _silu(y + cb_ref[:, cols]).astype(c_out.dtype)

    def project_piece(gi, a, step):
        start, _, rope_cols, qscale = groups[gi]
        y = _dot(ubuf[0:TM, :], w_ref[:, start + a:start + a + step])
        for r in range(0, step, LANES):
            yr = y[:, r:r + LANES]
            if a + r < rope_cols:
                yr = _rope_block(yr, cos, sin, lo)
            if qscale != 1.0:
                yr = yr * qscale
            out_refs[gi][:, a + r:a + r + LANES] = yr.astype(out_refs[gi].dtype)

    conv_pieces = [functools.partial(conv_piece, a) for a in range(0, c_width, MXU_COLS)]
    proj_pieces = []
    for gi, (_, width, rope_cols, _) in enumerate(groups):
        if gi != conv_idx:
            step = min(width, MXU_COLS if rope_cols else 2 * MXU_COLS)
            proj_pieces += [functools.partial(project_piece, gi, a, step) for a in range(0, width, step)]
    for i in range(max(len(conv_pieces), len(proj_pieces))):
        if i < len(proj_pieces):
            proj_pieces[i]()
        if i < len(conv_pieces):
            conv_pieces[i]()


def _seg_map(ctx_tiles, off):
    return lambda b, t: (b, jnp.where(t + off >= ctx_tiles, 1, 0), 0, 0)


def _inproj_call(hs, g, shift, scale, cos, sin, w, groups, dtypes, ctx_tiles, conv_idx, conv_w, conv_b):
    bsz, _, d = hs[0].shape
    s = sum(h_.shape[1] for h_ in hs)
    nt = s // TM
    seg = _seg_map(ctx_tiles, 0)
    const = lambda arr: pl.BlockSpec(arr.shape, lambda b, t: (0, 0))
    in_specs = _row_specs(hs, ctx_tiles, 0) + _halo_row_specs(hs, ctx_tiles) + [
                pl.BlockSpec((1, d), lambda b, t: (0, 0)),
                pl.BlockSpec((None, None, 1, d), seg),
                pl.BlockSpec((None, None, 1, d), seg),
                pl.BlockSpec((TM, LANES), lambda b, t: (t, 0)),
                pl.BlockSpec((TM, LANES), lambda b, t: (t, 0)),
                const(w), const(conv_w), const(conv_b)]
    out_specs = [pl.BlockSpec((None, TM, gr[1]), lambda b, t: (b, t, 0)) for gr in groups]
    out_shape = [jax.ShapeDtypeStruct((bsz, s, gr[1]), dt) for gr, dt in zip(groups, dtypes)]
    body = functools.partial(_inproj_body, groups=groups, conv_idx=conv_idx, n_src=len(hs), ctx_tiles=ctx_tiles,
                             nt=nt, seg_starts=(0, ctx_tiles))
    return pl.pallas_call(
        body, grid=(bsz, nt), in_specs=in_specs, out_specs=out_specs, out_shape=out_shape,
        scratch_shapes=[pltpu.VMEM((TM + 2 * HALO, groups[conv_idx][1]), F32),
                        pltpu.VMEM((TM + 2 * HALO, d), BF16)],
        compiler_params=_cparams("parallel", "arbitrary"),
        name="in_proj",
    )(*hs, *hs, *hs, g, shift, scale, cos, sin, w, conv_w, conv_b)


def _halo_flags(t, nt, seg_starts):
    prev_ok = t >= 0
    nxt_ok = (t + 1) < nt
    for s0 in seg_starts:
        prev_ok = jnp.logical_and(prev_ok, t != s0)
        nxt_ok = jnp.logical_and(nxt_ok, (t + 1) != s0)
    return prev_ok, nxt_ok


def _win_attn_body(sink_ref, bias_ref, q_ref, kvp_ref, kvc_ref, kvn_ref, kvx_ref, o_ref):
    blk = q_ref.shape[0]
    kvcat = jnp.concatenate([kvp_ref[...], kvc_ref[...], kvn_ref[...], kvx_ref[...]], axis=0)
    kcat = kvcat[:, :LANES]
    vcat = kvcat[:, LANES:]
    nkeys = kvcat.shape[0]
    rows = A_GROUP * blk
    bias = bias_ref[...]
    lane = lax.broadcasted_iota(jnp.int32, (blk, LANES), 1)
    rowg = lax.broadcasted_iota(jnp.int32, (rows, 1), 0) // blk
    q = q_ref[...]

    def logits(h):
        half = (lane >= h * HEAD_DIM) & (lane < (h + 1) * HEAD_DIM)
        qh = jnp.concatenate(
            [jnp.where(half, q[:, g * LANES:(g + 1) * LANES], jnp.zeros((), BF16)) for g in range(A_GROUP)], axis=0)
        return _dot_nt(qh, kcat)

    raw = [logits(h) for h in range(A_KV_HEADS)]
    outs = []
    for h in range(A_KV_HEADS):
        s = (raw[h].reshape(A_GROUP, blk, nkeys) + bias[None]).reshape(rows, nkeys)
        sink = jnp.zeros((rows, 1), F32)
        for g in range(A_GROUP):
            sink = jnp.where(rowg == g, sink_ref[h * A_GROUP + g] * LOG2E, sink)
        m = jnp.maximum(jnp.max(s, axis=1, keepdims=True), sink)
        p = jnp.exp2(s - m)
        den = jnp.sum(p, axis=1, keepdims=True) + jnp.exp2(sink - m)
        outs.append(_dot(p.astype(BF16), vcat) / den)
    for g in range(A_GROUP):
        o_ref[:, g * LANES:(g + 1) * LANES] = jnp.where(
            lane < HEAD_DIM, outs[0][g * blk:(g + 1) * blk], outs[1][g * blk:(g + 1) * blk]).astype(o_ref.dtype)


def _win_attn_bias(ctx_len):
    qi = jnp.arange(WQ)[:, None]
    kj = jnp.arange(WQ + 2 * A_WINDOW)[None, :]
    band = jnp.abs(kj - A_WINDOW - qi) <= A_WINDOW
    variants = [jnp.zeros_like(band), band & (kj >= A_WINDOW), band, band & (kj < A_WINDOW + WQ)]
    win = jnp.stack([jnp.where(v_, 0.0, NEG) for v_ in variants]).astype(F32)
    return jnp.concatenate([win, jnp.zeros((4, WQ, ctx_len), F32)], axis=2)


def _win_attn_call(q, kv, sink, ctx_len):
    bsz, s, qw = q.shape
    kw = kv.shape[-1]
    nq = s // WQ
    cq = ctx_len // WQ
    per = WQ // A_WINDOW
    nblk = s // A_WINDOW
    assert ctx_len % WQ == 0 and nq - cq >= 2
    bias = _win_attn_bias(ctx_len)
    clamp = lambda j: jnp.clip(j, cq * per, nblk - 1)
    variant = lambda i: jnp.where(i < cq, 0, jnp.where(i == cq, 1, jnp.where(i == nq - 1, 3, 2)))
    kv_specs = [pl.BlockSpec((None, A_WINDOW, kw), lambda b, i: (b, clamp(i * per - 1), 0)),
                pl.BlockSpec((None, WQ, kw), lambda b, i: (b, i, 0)),
                pl.BlockSpec((None, A_WINDOW, kw), lambda b, i: (b, clamp((i + 1) * per), 0)),
                pl.BlockSpec((None, ctx_len, kw), lambda b, i: (b, 0, 0))]
    return pl.pallas_call(
        _win_attn_body,
        grid=(bsz, nq),
        in_specs=[pl.BlockSpec(memory_space=pltpu.SMEM),
                  pl.BlockSpec((None,) + bias.shape[1:], lambda b, i: (variant(i), 0, 0)),
                  pl.BlockSpec((None, WQ, qw), lambda b, i: (b, i, 0))] + kv_specs,
        out_specs=pl.BlockSpec((None, WQ, qw), lambda b, i: (b, i, 0)),
        out_shape=jax.ShapeDtypeStruct((bsz, s, qw), BF16),
        compiler_params=_cparams("parallel", "arbitrary"),
        name="window_attention",
    )(sink, bias, q, kv, kv, kv, kv)


def _bwd_chunk(j, ctx_chunks, n_chunks):
    return jnp.where(j < ctx_chunks, ctx_chunks - 1 - j, n_chunks + ctx_chunks - 1 - j)


def _scan_rows(dirs):
    return [(bb, d, tuple(r.at[bb] for r in refs)) for bb in range(SCAN_ROWS) for d, refs in enumerate(dirs)]


def _tri_masks(d):
    row = lax.broadcasted_iota(jnp.int32, (CHUNK, CHUNK), 0)
    col = lax.broadcasted_iota(jnp.int32, (CHUNK, CHUNK), 1)
    keep = (col <= row) if d == 0 else (col >= row)
    return keep, jnp.where(keep, 1.0, 0.0).astype(BF16)


def _mlstm_body(qkf_ref, vf_ref, gf_ref, qkb_ref, vb_ref, gb_ref, bias_ref, hf_ref, hb_ref, c_scr, m_scr):
    L = CHUNK
    dk = HEAD_DIM
    dv = LANES

    @pl.when(pl.program_id(1) == 0)
    def _():
        c_scr[...] = jnp.zeros_like(c_scr)
        m_scr[...] = jnp.zeros_like(m_scr)

    lane = lax.broadcasted_iota(jnp.int32, (L, LANES), 1)
    srow = lax.broadcasted_iota(jnp.int32, (LANES, 2 * dv), 0)
    ones_v = jnp.ones((L, dv), BF16)
    dirs = ((qkf_ref, vf_ref, gf_ref, hf_ref), (qkb_ref, vb_ref, gb_ref, hb_ref))
    for bb, d, (qk_ref, v_ref, g_ref, o_ref) in _scan_rows(dirs):
        keep, tri = _tri_masks(d)
        gates = g_ref[...] + bias_ref[...]
        gates_t = gates.T[0:16, :]
        b_col = _cumsum_cols(tri, _log_sigmoid(gates))
        b_row = _cumsum_rows(_log_sigmoid(gates_t), tri)
        k_pairs = [qk_ref[:, B_HEADS * dk + p * LANES:B_HEADS * dk + (p + 1) * LANES] for p in range(B_HEADS // 2)]
        kt_pairs = [kp.astype(F32).T.astype(BF16) for kp in k_pairs]
        for h in range(B_HEADS):
            ci = d * B_HEADS + h
            cf = 2 * B_HEADS + ci
            r = bb * 2 * B_HEADS + ci
            li_c = jnp.broadcast_to(gates[:, ci:ci + 1], (L, LANES))
            li_r = gates_t[ci:ci + 1, :]
            bc = jnp.broadcast_to(b_col[:, cf:cf + 1], (L, LANES))
            br = b_row[cf:cf + 1, :]
            b_last = bc[L - 1:L, :] if d == 0 else bc[0:1, :]
            m_prev = m_scr[r:r + 1, :]
            pair = (h // 2) * LANES
            half = (lane >= (h % 2) * dk) & (lane < (h % 2 + 1) * dk)
            q = jnp.where(half, qk_ref[:, pair:pair + LANES] * (dk ** -0.5), 0.0).astype(BF16)
            k = k_pairs[h // 2].astype(BF16)
            v = v_ref[:, h * dv:(h + 1) * dv]
            dm = jnp.where(keep, bc - br + li_r, NEG)
            g_in = bc + m_prev
            m_t = jnp.maximum(jnp.broadcast_to(jnp.max(dm, axis=1, keepdims=True), (L, LANES)), g_in)
            sm = (_dot_nt(q, k) * jnp.exp(dm - m_t)).astype(BF16)
            vaug = jnp.concatenate([v, ones_v], axis=1)
            c_prev = c_scr[r]
            intra = _dot(sm, vaug)
            inter = _dot(q, c_prev.astype(BF16))
            a_in = jnp.exp(g_in - m_t)
            den = jnp.maximum(jnp.abs(intra[:, dv:] + a_in * inter[:, dv:]), jnp.exp(-m_t))
            o_ref[:, h * dv:(h + 1) * dv] = ((intra[:, :dv] + a_in * inter[:, :dv]) / den).astype(o_ref.dtype)
            w_c = b_last - bc + li_c
            m_loc = jnp.max(w_c, axis=0, keepdims=True)
            e = jnp.exp(w_c - m_loc)
            ev = jnp.concatenate([e * v.astype(F32), e], axis=1).astype(BF16)
            c_loc = _dot(kt_pairs[h // 2], ev)
            own = (srow >= (h % 2) * dk) & (srow < (h % 2 + 1) * dk)
            m_new = jnp.maximum(b_last + m_prev, m_loc)
            keep_old = jnp.exp(b_last + m_prev - m_new)
            take_new = jnp.exp(m_loc - m_new)
            c_scr[r] = (jnp.concatenate([keep_old, keep_old], axis=1) * c_prev
                        + jnp.where(own, jnp.concatenate([take_new, take_new], axis=1) * c_loc, 0.0))
            m_scr[r:r + 1, :] = m_new


def _scan_specs(width, ctx_chunks, n_chunks):
    fwd = pl.BlockSpec((SCAN_ROWS, CHUNK, width), lambda b, j: (b, j, 0))
    bwd = pl.BlockSpec((SCAN_ROWS, CHUNK, width), lambda b, j: (b, _bwd_chunk(j, ctx_chunks, n_chunks), 0))
    return fwd, bwd


def _mlstm_call(qk, v, gates, bias, ctx_len):
    bsz, s, w = v.shape
    nc = s // CHUNK
    cc = ctx_len // CHUNK
    qf, qb = _scan_specs(qk.shape[-1], cc, nc)
    vf, vb = _scan_specs(w, cc, nc)
    gf, gb = _scan_specs(LANES, cc, nc)
    return pl.pallas_call(
        _mlstm_body,
        grid=(bsz // SCAN_ROWS, nc),
        in_specs=[qf, vf, gf, qb, vb, gb, pl.BlockSpec((1, LANES), lambda b, j: (0, 0))],
        out_specs=[vf, vb],
        out_shape=[jax.ShapeDtypeStruct((bsz, s, w), BF16)] * 2,
        scratch_shapes=[pltpu.VMEM((SCAN_ROWS * 2 * B_HEADS, LANES, 2 * LANES), F32),
                        pltpu.VMEM((SCAN_ROWS * 2 * B_HEADS, LANES), F32)],
        compiler_params=_cparams("parallel", "arbitrary"),
        name="mlstm_scan",
    )(qk, v, gates, qk, v, gates, bias)


def _ssd_body(xf_ref, dtf_ref, xb_ref, dtb_ref, dtbias_ref, nega_ref, skip_ref, yf_ref, yb_ref, s_scr):
    L = CHUNK
    hd = HEAD_DIM
    inner = 4 * D_GROUPS * hd
    hpg = 4

    @pl.when(pl.program_id(1) == 0)
    def _():
        s_scr[...] = jnp.zeros_like(s_scr)

    lane = lax.broadcasted_iota(jnp.int32, (L, LANES), 1)
    lo = lane < hd
    lo1 = lo[0:1, :]
    dirs = ((xf_ref, dtf_ref, yf_ref), (xb_ref, dtb_ref, yb_ref))
    for bb, d, (x_ref, dt_ref, y_ref) in _scan_rows(dirs):
        keep, tri = _tri_masks(d)
        dt = _softplus(dt_ref[...] + dtbias_ref[...])
        a = dt * nega_ref[...]
        dt_t = dt.T[0:16, :]
        a_t = a.T[0:16, :]
        ac_col = _cumsum_cols(tri, a)
        ac_row = _cumsum_rows(a_t, tri)
        for g in range(D_GROUPS):
            bg_f = x_ref[:, inner + g * LANES:inner + (g + 1) * LANES]
            bg = bg_f.astype(BF16)
            bg_t = bg_f.T.astype(BF16)
            cg = x_ref[:, inner + (D_GROUPS + g) * LANES:inner + (D_GROUPS + g + 1) * LANES].astype(BF16)
            cb = _dot_nt(cg, bg)
            for pr in range(2):
                c0 = g * hpg * hd + pr * LANES
                xp = x_ref[:, c0:c0 + LANES]
                xpb = xp.astype(BF16)
                ys, eas, wsts, als = [], [], [], []
                for hh in range(2):
                    col = d * D_GROUPS * hpg + g * hpg + pr * 2 + hh
                    acc = jnp.broadcast_to(ac_col[:, col:col + 1], (L, LANES))
                    acr = ac_row[col:col + 1, :]
                    seg = jnp.where(keep, acc - acr, NEG)
                    mix = (cb * jnp.exp(seg) * dt_t[col:col + 1, :]).astype(BF16)
                    ys.append(_dot(mix, xpb))
                    a_last = acc[L - 1:L, :] if d == 0 else acc[0:1, :]
                    eas.append(jnp.exp(acc))
                    wsts.append(jnp.exp(a_last - acc) * jnp.broadcast_to(dt[:, col:col + 1], (L, LANES)))
                    als.append(jnp.exp(a_last))
                sidx = (bb * 2 + d) * 2 * D_GROUPS + g * 2 + pr
                st = s_scr[sidx]
                y = jnp.where(lo, ys[0], ys[1]) + _dot(cg, st.astype(BF16)) * jnp.where(lo, eas[0], eas[1])
                if d == 0:
                    y = y + skip_ref[:, c0:c0 + LANES] * xp
                y_ref[:, c0:c0 + LANES] = y.astype(y_ref.dtype)
                xw = (xp * jnp.where(lo, wsts[0], wsts[1])).astype(BF16)
                s_scr[sidx] = jnp.where(lo1, als[0], als[1]) * st + _dot(bg_t, xw)


def _ssd_call(xbc, dt, dt_bias, neg_a, skip, ctx_len):
    bsz, s, w = xbc.shape
    nc = s // CHUNK
    cc = ctx_len // CHUNK
    inner = skip.shape[-1]
    xf, xb = _scan_specs(w, cc, nc)
    df, db = _scan_specs(LANES, cc, nc)
    yf, yb = _scan_specs(inner, cc, nc)
    vec = lambda n: pl.BlockSpec((1, n), lambda b, j: (0, 0))
    return pl.pallas_call(
        _ssd_body,
        grid=(bsz // SCAN_ROWS, nc),
        in_specs=[xf, df, xb, db, vec(LANES), vec(LANES), vec(inner)],
        out_specs=[yf, yb],
        out_shape=[jax.ShapeDtypeStruct((bsz, s, inner), BF16)] * 2,
        scratch_shapes=[pltpu.VMEM((SCAN_ROWS * 4 * D_GROUPS, LANES, LANES), F32)],
        compiler_params=_cparams("parallel", "arbitrary"),
        name="ssd_scan",
    )(xbc, dt, xbc, dt, dt_bias, neg_a, skip)


def _diff_attn_body(lam_ref, q_ref, k_ref, v_ref, o_ref, *, lam_init):
    lv = lam_ref[...]
    lam = (jnp.exp(jnp.sum(lv[0:1] * lv[1:2], axis=1, keepdims=True))
           - jnp.exp(jnp.sum(lv[2:3] * lv[3:4], axis=1, keepdims=True)) + lam_init)
    tq = q_ref.shape[0]
    lane = lax.broadcasted_iota(jnp.int32, (tq, LANES), 1)
    nheads = q_ref.shape[1] // LANES

    def logits(h):
        q = q_ref[:, h * LANES:(h + 1) * LANES]
        k = k_ref[:, h * LANES:(h + 1) * LANES]
        return [_dot_nt(jnp.where((lane >= m * HEAD_DIM) & (lane < (m + 1) * HEAD_DIM), q, jnp.zeros((), BF16)), k)
                for m in range(2)]

    s_next = logits(0)
    for h in range(nheads):
        s_cur = s_next
        if h + 1 < nheads:
            s_next = logits(h + 1)
        ps, ls = [], []
        for s in s_cur:
            p = jnp.exp2(s - jnp.max(s, axis=1, keepdims=True))
            ps.append(p)
            ls.append(jnp.sum(p, axis=1, keepdims=True))
        a = (ps[0] - (lam * ls[0] / ls[1]) * ps[1]).astype(BF16)
        o_ref[:, h * LANES:(h + 1) * LANES] = _dot(a, v_ref[:, h * LANES:(h + 1) * LANES]) / ls[0]


def _diff_attn_call(q, k, v, lam_vecs, lam_init, ctx_len):
    bsz, s, w = q.shape
    t = s - ctx_len
    off = ctx_len // TQ
    return pl.pallas_call(
        functools.partial(_diff_attn_body, lam_init=lam_init),
        grid=(bsz, t // TQ),
        in_specs=[pl.BlockSpec(lam_vecs.shape, lambda b, i: (0, 0)),
                  pl.BlockSpec((None, TQ, w), lambda b, i: (b, i + off, 0)),
                  pl.BlockSpec((None, s, w), lambda b, i: (b, 0, 0)),
                  pl.BlockSpec((None, s, w), lambda b, i: (b, 0, 0))],
        out_specs=pl.BlockSpec((None, TQ, w), lambda b, i: (b, i, 0)),
        out_shape=jax.ShapeDtypeStruct((bsz, t, w), F32),
        compiler_params=_cparams("parallel", "arbitrary"),
        name="diff_attention",
    )(lam_vecs, q, k, v)


def _group_rms(x, width):
    return jnp.concatenate([_rms(x[:, a:a + width]) for a in range(0, x.shape[1], width)], axis=1)


def _outproj_even_body(ya_ref, hf_ref, hb_ref, og_ref, ng_ref, w_ref, g_ref, out_ref):
    half = ya_ref.shape[1]
    f32 = lambda r: r[...].astype(F32)
    hn = _group_rms(f32(hf_ref) + f32(hb_ref), LANES) * ng_ref[...] * _sigmoid(f32(og_ref))
    y = _dot(ya_ref[...], w_ref[0:half, :]) + _dot(hn.astype(BF16), w_ref[half:, :])
    out_ref[...] = (_rms(y) * g_ref[...]).astype(out_ref.dtype)


def _outproj_odd_body(oa_ref, yf_ref, yb_ref, z_ref, cg_ref, dg_ref, w_ref, g_ref, out_ref, *, out_scale):
    half = oa_ref.shape[1]
    on = _group_rms(oa_ref[...], LANES) * cg_ref[...] * out_scale
    f32 = lambda r: r[...].astype(F32)
    yz = (f32(yf_ref) + f32(yb_ref)) * _silu(f32(z_ref))
    sn = _group_rms(yz, half // D_GROUPS) * dg_ref[...]
    y = _dot(on.astype(BF16), w_ref[0:half, :]) + _dot(sn.astype(BF16), w_ref[half:, :])
    out_ref[...] = (_rms(y) * g_ref[...]).astype(out_ref.dtype)


def _outproj_call(body, acts, act_offs, vecs, w, g, n_t):
    bsz = acts[0].shape[0]
    d = w.shape[1]

    def act_spec(arr, off):
        return pl.BlockSpec((None, TM, arr.shape[-1]), lambda b, t: (b, t + off, 0))

    in_specs = ([act_spec(a_, o_) for a_, o_ in zip(acts, act_offs)]
                + [pl.BlockSpec(v_.shape, lambda b, t: (0, 0)) for v_ in vecs]
                + [pl.BlockSpec(w.shape, lambda b, t: (0, 0)),
                   pl.BlockSpec((1, d), lambda b, t: (0, 0))])
    return pl.pallas_call(
        body, grid=(bsz, n_t), in_specs=in_specs,
        out_specs=pl.BlockSpec((None, TM, d), lambda b, t: (b, t, 0)),
        out_shape=jax.ShapeDtypeStruct((bsz, n_t * TM, d), BF16),
        compiler_params=_cparams("parallel", "arbitrary"),
        name="out_proj",
    )(*acts, *vecs, w, g)


def _ffn_body(*refs, n_src, ctx_tiles, t_off, nt, seg_starts):
    mains, prevs, nexts = refs[:n_src], refs[n_src:2 * n_src], refs[2 * n_src:3 * n_src]
    (ym_ref, yp_ref, yn_ref, gate1_ref, g2_ref, sh_ref, sc_ref, wg_ref, wu_ref, cw_ref, wd_ref, g3_ref, gate_ref,
     out_ref, gscr, uscr) = refs[3 * n_src:]
    t = pl.program_id(1)
    prev_ok, nxt_ok = _halo_flags(t, nt, seg_starts)

    def resid(h_refs, y):
        return _load_rows(h_refs, ctx_tiles, t_off) + gate1_ref[...] * y

    def normmod(h):
        return (_rms(h) * g2_ref[...]) * (1.0 + sc_ref[...]) + sh_ref[...]

    hm = resid(mains, ym_ref[...].astype(F32))
    um = normmod(hm)
    up = jnp.where(prev_ok, normmod(resid(prevs, yp_ref[...].astype(F32)[Y_HALO - HALO:, :])), 0.0)
    un = jnp.where(nxt_ok, normmod(resid(nexts, yn_ref[...].astype(F32)[:HALO, :])), 0.0)
    ucat = jnp.concatenate([up, um, un], axis=0).astype(BF16)
    umb = um.astype(BF16)

    nchunk = wg_ref.shape[1] // FFN_CHUNK
    cols = lambda c: slice(c * FFN_CHUNK, (c + 1) * FFN_CHUNK)

    def front(c):
        gscr[c % 2] = _dot(ucat, wg_ref[:, cols(c)])
        uscr[c % 2] = _dot(umb, wu_ref[:, cols(c)])

    front(0)
    y = None
    for c in range(nchunk):
        if c + 1 < nchunk:
            front(c + 1)
        gbuf = gscr.at[c % 2]
        gc = (gbuf[pl.ds(HALO - 1, TM), :] * cw_ref[0:1, cols(c)] + gbuf[pl.ds(HALO, TM), :] * cw_ref[1:2, cols(c)]
              + gbuf[pl.ds(HALO + 1, TM), :] * cw_ref[2:3, cols(c)])
        hid = (_silu(gc) * uscr[c % 2]).astype(BF16)
        part = _dot(hid, wd_ref[cols(c), :])
        y = part if y is None else y + part
    out_ref[...] = hm + gate_ref[...] * (_rms(y) * g3_ref[...])


def _ffn_call(hs, yn, gate1, g2, shift, scale, wg, wu, cw, wd, g3, gate2, ctx_tiles, t_off, seg_starts):
    bsz, rows, d = yn.shape
    n_t = rows // TM
    per = TM // Y_HALO
    last = rows // Y_HALO - 1
    const = lambda arr: pl.BlockSpec(arr.shape, lambda b, t: (0, 0))
    vec = pl.BlockSpec((1, d), lambda b, t: (0, 0))
    mod = pl.BlockSpec((None, None, 1, d), _seg_map(ctx_tiles, t_off))
    y_specs = [pl.BlockSpec((None, TM, d), lambda b, t: (b, t, 0)),
               pl.BlockSpec((None, Y_HALO, d), lambda b, t: (b, jnp.maximum(t * per - 1, 0), 0)),
               pl.BlockSpec((None, Y_HALO, d), lambda b, t: (b, jnp.minimum((t + 1) * per, last), 0))]
    body = functools.partial(_ffn_body, n_src=len(hs), ctx_tiles=ctx_tiles, t_off=t_off, nt=n_t, seg_starts=seg_starts)
    return pl.pallas_call(
        body, grid=(bsz, n_t),
        in_specs=(_row_specs(hs, ctx_tiles, t_off) + _halo_row_specs(hs, ctx_tiles, t_off) + y_specs
                  + [mod, vec, mod, mod, const(wg), const(wu), const(cw), const(wd), vec, mod]),
        out_specs=pl.BlockSpec((None, TM, d), lambda b, t: (b, t, 0)),
        out_shape=jax.ShapeDtypeStruct((bsz, rows, d), F32),
        scratch_shapes=[pltpu.VMEM((2, TM + 2 * HALO, FFN_CHUNK), F32), pltpu.VMEM((2, TM, FFN_CHUNK), F32)],
        compiler_params=_cparams("parallel", "arbitrary"),
        name="conv_ffn",
    )(*hs, *hs, *hs, yn, yn, yn, gate1, g2, shift, scale, wg, wu, cw, wd, g3, gate2)


def _rope_tables(ctx_len, t):
    pos = jnp.arange(t)
    row = (pos // GRID_W).astype(F32)
    col = (pos % GRID_W).astype(F32)
    nf = HEAD_DIM // 4
    inv = ROPE_BASE ** (-jnp.arange(nf, dtype=F32) / nf)
    ar = row[:, None] * inv
    ac = col[:, None] * inv
    cos = jnp.concatenate([jnp.cos(ar), jnp.cos(ar), jnp.cos(ac), jnp.cos(ac)], axis=1)
    sin = jnp.concatenate([-jnp.sin(ar), jnp.sin(ar), -jnp.sin(ac), jnp.sin(ac)], axis=1)
    cos = jnp.concatenate([jnp.ones((ctx_len, HEAD_DIM), F32), cos], axis=0)
    sin = jnp.concatenate([jnp.zeros((ctx_len, HEAD_DIM), F32), sin], axis=0)
    return jnp.tile(cos, (1, LANES // HEAD_DIM)), jnp.tile(sin, (1, LANES // HEAD_DIM))


def _pad_cols(w, n):
    return jnp.pad(w, ((0, 0), (0, n - w.shape[1])))


def _gqa_perm():
    idx = [(h * A_GROUP + g) * HEAD_DIM + dd
           for g in range(A_GROUP) for h in range(A_KV_HEADS) for dd in range(HEAD_DIM)]
    return jnp.array(idx, dtype=jnp.int32)


def _ffn_weights(w_gate, w_up, conv, w_down):
    return w_gate.astype(BF16), w_up.astype(BF16), conv, w_down.astype(BF16)


def kernel(x, c, ctx, c_ctx, mod_w, mod_b, norm_g, ffn_w_gate, ffn_w_up, ffn_conv, ffn_w_down, ev_w_in, ev_w_out, a_sink, b_conv, b_gate_b, b_norm_g, od_w_in, od_w_out, c_lambda, c_norm_g, d_conv, d_conv_b, d_dt_bias, d_a_log, d_skip, d_norm_g):
    bsz, t, d = x.shape
    ctx_len = ctx.shape[1]
    depth = mod_w.shape[0]
    s = ctx_len + t
    half = d // 2
    assert ctx_len % TM == 0 and t % TM == 0 and t % GRID_W == 0 and ffn_w_gate.shape[-1] % FFN_CHUNK == 0
    assert bsz % SCAN_ROWS == 0
    n_t = s // TM
    ctx_tiles = ctx_len // TM

    mod_rows = -(-(bsz + 1) // HALO) * HALO
    cc = jnp.concatenate([c, c_ctx[None, :], jnp.zeros((mod_rows - bsz - 1, d), F32)], axis=0)
    mods = _mod_call(cc, mod_w, mod_b).reshape(depth, mod_rows, 6, d)

    def layer_mods(l):
        lat = mods[l, :bsz]
        cx = jnp.broadcast_to(mods[l, bsz][None], (bsz, 6, d))
        m = jnp.stack([cx, lat], axis=1)
        return [m[:, :, i][:, :, None, :] for i in range(6)]

    cos, sin = _rope_tables(ctx_len, t)
    hs = (ctx, x)

    for l in range(depth):
        last = l == depth - 1
        j = l // 2
        m = layer_mods(l)
        g = norm_g[l]
        wg, wu, cw, wd = _ffn_weights(ffn_w_gate[l], ffn_w_up[l], ffn_conv[l], ffn_w_down[l])
        if l % 2 == 0:
            w_in = ev_w_in[j]
            perm = _gqa_perm()
            akv = A_KV_HEADS * HEAD_DIM
            o1 = half + 2 * akv
            w_cat = jnp.concatenate([w_in[:, :half][:, perm], w_in[:, half:o1 + 3 * half],
                                     _pad_cols(w_in[:, o1 + 3 * half:], LANES)], axis=1).astype(BF16)
            groups = ((0, half, half, HEAD_DIM ** -0.5 * LOG2E), (half, 2 * akv, akv, 1.0),
                      (o1, half, 0, 1.0), (o1 + half, half, 0, 1.0), (o1 + 2 * half, half, 0, 1.0),
                      (o1 + 3 * half, LANES, 0, 1.0))
            q, kv, qkc, mv, mo, gates = _inproj_call(
                hs, g[0:1], m[0], m[1], cos, sin, w_cat, groups, (BF16, BF16, BF16, BF16, BF16, F32), ctx_tiles,
                2, b_conv[j], jnp.zeros((1, half), F32))
            ya = _win_attn_call(q, kv, a_sink[j], ctx_len)
            hf, hb = _mlstm_call(qkc, mv, gates, _pad_cols(b_gate_b[j][None, :], LANES), ctx_len)
            w_out = ev_w_out[j]
            w_out = jnp.concatenate([w_out[:half][perm], w_out[half:]], axis=0).astype(BF16)
            acts, vecs, body = (ya, hf, hb, mo), (b_norm_g[j][None, :],), _outproj_even_body
        else:
            w_in = od_w_in[j]
            xbc_w = 2 * half
            w_cat = jnp.concatenate([w_in[:, :4 * half + xbc_w],
                                     _pad_cols(w_in[:, 4 * half + xbc_w:], LANES)], axis=1).astype(BF16)
            groups = ((0, half, half, HEAD_DIM ** -0.5 * LOG2E), (half, half, half, 1.0), (2 * half, half, 0, 1.0),
                      (3 * half, half, 0, 1.0), (4 * half, xbc_w, 0, 1.0), (4 * half + xbc_w, LANES, 0, 1.0))
            q, k, v, z, xbcc, dt = _inproj_call(
                hs, g[0:1], m[0], m[1], cos, sin, w_cat, groups, (BF16, BF16, BF16, BF16, F32, F32), ctx_tiles,
                4, d_conv[j], d_conv_b[j][None, :])
            lam_init = 0.8 - 0.6 * math.exp(-LAM_DEPTH_RATE * l)
            oa = _diff_attn_call(q, k, v, c_lambda[j], lam_init, ctx_len)
            dt_bias = _pad_cols(d_dt_bias[j].reshape(1, -1), LANES)
            neg_a = _pad_cols(-jnp.exp(d_a_log[j].astype(F32)).reshape(1, -1), LANES)
            skip = jnp.repeat(d_skip[j].astype(F32), HEAD_DIM)[None, :]
            yf, yb = _ssd_call(xbcc, dt, dt_bias, neg_a, skip, ctx_len)
            w_out = od_w_out[j].astype(BF16)
            acts, vecs = (oa, yf, yb, z), (c_norm_g[j][None, :], d_norm_g[j][None, :])
            body = functools.partial(_outproj_odd_body, out_scale=1.0 - lam_init)
        t_off = ctx_tiles if last else 0
        if l % 2 == 0:
            act_offs = (t_off,) * 4
        else:
            assert last, "differential attention is only computed for latent queries"
            act_offs = (0,) + (t_off,) * 3
        yn = _outproj_call(body, acts, act_offs, vecs, w_out, g[1:2], n_t - t_off)
        seg_starts = (0,) if last else (0, ctx_tiles)
        hs = (_ffn_call(hs, yn, m[2], g[2:3], m[3], m[4], wg, wu, cw, wd, g[3:4], m[5], ctx_tiles, t_off,
                        seg_starts),)
    return hs[0]
```

```python
import functools
import math

import jax
import jax.numpy as jnp
from jax import lax
from jax.experimental import pallas as pl
from jax.experimental.pallas import tpu as pltpu

F32 = jnp.float32
BF16 = jnp.bfloat16

EPS = 1e-6
ROPE_BASE = 10000.0
GRID_W = 64
HEAD_DIM = 64
A_KV_HEADS = 2
A_GROUP = 4
A_WINDOW = 128
B_HEADS = 4
D_GROUPS = 2
LAM_DEPTH_RATE = 0.3

LANES = 128
MXU_COLS = 256
HALO = 8
Y_HALO = 16
TM = 256
CHUNK = 128
SCAN_ROWS = 2
FFN_CHUNK = 256
TQ = 256
WQ = 256
NEG = -1e30
SAFE_LOGIT = 40.0
NORM_SLACK = 1.05
LOG2E = math.log2(math.e)
VMEM_LIMIT = 56 * 1024 * 1024


def _cparams(*sem):
    return pltpu.CompilerParams(dimension_semantics=sem, vmem_limit_bytes=VMEM_LIMIT)


def _sigmoid(x):
    return 0.5 * jnp.tanh(0.5 * x) + 0.5


def _silu(x):
    return x * _sigmoid(x)


def _softplus(x):
    return jnp.maximum(x, 0.0) + jnp.log(1.0 + jnp.exp(-jnp.abs(x)))


def _log_sigmoid(x):
    return -_softplus(-x)


def _rms(x):
    return x * lax.rsqrt(jnp.mean(x * x, axis=-1, keepdims=True) + EPS)


def _dot(a, b):
    return jnp.dot(a, b, preferred_element_type=F32)


def _dot_nt(a, b):
    return lax.dot_general(a, b, (((1,), (1,)), ((), ())), preferred_element_type=F32)


def _dot_tn(a, b):
    return lax.dot_general(a, b, (((0,), (0,)), ((), ())), preferred_element_type=F32)


def _split3(x):
    x1 = x.astype(BF16)
    r1 = x - x1.astype(F32)
    x2 = r1.astype(BF16)
    x3 = (r1 - x2.astype(F32)).astype(BF16)
    return x1, x2, x3


def _cumsum_cols(tri, x):
    x1, x2, x3 = _split3(x)
    return _dot(tri, x1) + _dot(tri, x2) + _dot(tri, x3)


def _cumsum_rows(x, tri):
    x1, x2, x3 = _split3(x)
    return _dot_nt(x1, tri) + _dot_nt(x2, tri) + _dot_nt(x3, tri)


def _mod_body(c_ref, w_ref, b_ref, o_ref):
    a = _silu(c_ref[...]).astype(BF16)
    o_ref[...] = _dot(a, w_ref[...].astype(BF16)) + b_ref[...]


def _mod_call(cc, mod_w, mod_b):
    depth, d, n6 = mod_w.shape
    rows = cc.shape[0]
    tn = 1024
    return pl.pallas_call(
        _mod_body,
        grid=(depth, n6 // tn),
        in_specs=[pl.BlockSpec((rows, d), lambda l, j: (0, 0)),
                  pl.BlockSpec((None, d, tn), lambda l, j: (l, 0, j)),
                  pl.BlockSpec((None, 1, tn), lambda l, j: (l, 0, j))],
        out_specs=pl.BlockSpec((None, rows, tn), lambda l, j: (l, 0, j)),
        out_shape=jax.ShapeDtypeStruct((depth, rows, n6), F32),
        compiler_params=_cparams("arbitrary", "arbitrary"),
        name="modulation",
    )(cc, mod_w, mod_b.reshape(depth, 1, n6))


def _rope_block(y, cos, sin, lo):
    partner = jnp.where(lo, pltpu.roll(y, LANES - 16, 1), pltpu.roll(y, 16, 1))
    return y * cos + partner * sin


def _row_specs(hs, ctx_tiles, t_off):
    d = hs[0].shape[-1]
    if len(hs) == 1:
        return [pl.BlockSpec((None, TM, d), lambda b, t: (b, t + t_off, 0))]
    return [pl.BlockSpec((None, TM, d), lambda b, t: (b, jnp.minimum(t + t_off, ctx_tiles - 1), 0)),
            pl.BlockSpec((None, TM, d), lambda b, t: (b, jnp.maximum(t + t_off - ctx_tiles, 0), 0))]


def _load_rows(h_refs, ctx_tiles, t_off):
    if len(h_refs) == 1:
        return h_refs[0][...]
    return jnp.where(pl.program_id(1) + t_off < ctx_tiles, h_refs[0][...], h_refs[1][...])


def _halo_row_specs(hs, ctx_tiles, t_off=0):
    d = hs[0].shape[-1]
    per = TM // HALO
    offs = (0,) if len(hs) == 1 else (0, ctx_tiles)
    prevs, nexts = [], []
    for h_, off in zip(hs, offs):
        last = h_.shape[1] // HALO - 1
        prevs.append(pl.BlockSpec(
            (None, HALO, d), lambda b, t, off=off, last=last: (b, jnp.clip((t + t_off - off) * per - 1, 0, last), 0)))
        nexts.append(pl.BlockSpec(
            (None, HALO, d), lambda b, t, off=off, last=last: (b, jnp.clip((t + t_off - off + 1) * per, 0, last), 0)))
    return prevs + nexts


def _inproj_body(*refs, groups, conv_idx, n_src, ctx_tiles, nt, seg_starts):
    mains, prevs, nexts = refs[:n_src], refs[n_src:2 * n_src], refs[2 * n_src:3 * n_src]
    g_ref, sh_ref, sc_ref, cos_ref, sin_ref, w_ref, cw_ref, cb_ref = refs[3 * n_src:3 * n_src + 8]
    out_refs = refs[3 * n_src + 8:-2]
    scr, ubuf = refs[-2:]

    def normmod(h):
        return (_rms(h) * g_ref[...]) * (1.0 + sc_ref[...]) + sh_ref[...]

    ubuf[0:TM, :] = normmod(_load_rows(mains, ctx_tiles, 0)).astype(BF16)
    prev_ok, nxt_ok = _halo_flags(pl.program_id(1), nt, seg_starts)
    up = jnp.where(prev_ok, normmod(_load_rows(prevs, ctx_tiles, 0)), 0.0)
    un = jnp.where(nxt_ok, normmod(_load_rows(nexts, ctx_tiles, 0)), 0.0)
    ubuf[TM:TM + 2 * HALO, :] = jnp.concatenate([up, un], axis=0).astype(BF16)
    cos = cos_ref[...]
    sin = sin_ref[...]
    lane = lax.broadcasted_iota(jnp.int32, cos.shape, 1)
    lo = (lane & 31) < 16
    c_start, c_width = groups[conv_idx][:2]
    c_out = out_refs[conv_idx]
    taps = cw_ref.shape[0]
    for a in range(0, c_width, MXU_COLS):
        proj = _dot(ubuf[...], w_ref[:, c_start + a:c_start + a + MXU_COLS])
        scr[0:HALO, a:a + MXU_COLS] = proj[TM:TM + HALO]
        scr[HALO:HALO + TM, a:a + MXU_COLS] = proj[0:TM]
        scr[HALO + TM:, a:a + MXU_COLS] = proj[TM + HALO:]

    def conv_piece(a):
        cols = slice(a, a + MXU_COLS)
        y = scr[pl.ds(HALO - taps // 2, TM), cols] * cw_ref[0:1, cols]
        for j in range(1, taps):
            y = y + scr[pl.ds(HALO - taps // 2 + j, TM), cols] * cw_ref[j:j + 1, cols]
        c_out[:, cols] = _silu(y + cb_ref[:, cols]).astype(c_out.dtype)

    def project_piece(gi, a, step):
        start, _, rope_cols, qscale = groups[gi]
        y = _dot(ubuf[0:TM, :], w_ref[:, start + a:start + a + step])
        for r in range(0, step, LANES):
            yr = y[:, r:r + LANES]
            if a + r < rope_cols:
                yr = _rope_block(yr, cos, sin, lo)
            if qscale != 1.0:
                yr = yr * qscale
            out_refs[gi][:, a + r:a + r + LANES] = yr.astype(out_refs[gi].dtype)

    conv_pieces = [functools.partial(conv_piece, a) for a in range(0, c_width, MXU_COLS)]
    proj_pieces = []
    for gi, (_, width, rope_cols, _) in enumerate(groups):
        if gi != conv_idx:
            step = min(width, MXU_COLS if rope_cols else 2 * MXU_COLS)
            proj_pieces += [functools.partial(project_piece, gi, a, step) for a in range(0, width, step)]
    for i in range(max(len(conv_pieces), len(proj_pieces))):
        if i < len(proj_pieces):
            proj_pieces[i]()
        if i < len(conv_pieces):
            conv_pieces[i]()


def _seg_map(ctx_tiles, off):
    return lambda b, t: (b, jnp.where(t + off >= ctx_tiles, 1, 0), 0, 0)


def _inproj_call(hs, g, shift, scale, cos, sin, w, groups, dtypes, ctx_tiles, conv_idx, conv_w, conv_b):
    bsz, _, d = hs[0].shape
    s = sum(h_.shape[1] for h_ in hs)
    nt = s // TM
    seg = _seg_map(ctx_tiles, 0)
    const = lambda arr: pl.BlockSpec(arr.shape, lambda b, t: (0, 0))
    in_specs = _row_specs(hs, ctx_tiles, 0) + _halo_row_specs(hs, ctx_tiles) + [
                pl.BlockSpec((1, d), lambda b, t: (0, 0)),
                pl.BlockSpec((None, None, 1, d), seg),
                pl.BlockSpec((None, None, 1, d), seg),
                pl.BlockSpec((TM, LANES), lambda b, t: (t, 0)),
                pl.BlockSpec((TM, LANES), lambda b, t: (t, 0)),
                const(w), const(conv_w), const(conv_b)]
    out_specs = [pl.BlockSpec((None, TM, gr[1]), lambda b, t: (b, t, 0)) for gr in groups]
    out_shape = [jax.ShapeDtypeStruct((bsz, s, gr[1]), dt) for gr, dt in zip(groups, dtypes)]
    body = functools.partial(_inproj_body, groups=groups, conv_idx=conv_idx, n_src=len(hs), ctx_tiles=ctx_tiles,
                             nt=nt, seg_starts=(0, ctx_tiles))
    return pl.pallas_call(
        body, grid=(bsz, nt), in_specs=in_specs, out_specs=out_specs, out_shape=out_shape,
        scratch_shapes=[pltpu.VMEM((TM + 2 * HALO, groups[conv_idx][1]), F32),
                        pltpu.VMEM((TM + 2 * HALO, d), BF16)],
        compiler_params=_cparams("parallel", "arbitrary"),
        name="in_proj",
    )(*hs, *hs, *hs, g, shift, scale, cos, sin, w, conv_w, conv_b)


def _halo_flags(t, nt, seg_starts):
    prev_ok = t >= 0
    nxt_ok = (t + 1) < nt
    for s0 in seg_starts:
        prev_ok = jnp.logical_and(prev_ok, t != s0)
        nxt_ok = jnp.logical_and(nxt_ok, (t + 1) != s0)
    return prev_ok, nxt_ok


def _win_attn_body(sink_ref, bias_ref, q_ref, kvp_ref, kvc_ref, kvn_ref, kvx_ref, o_ref):
    blk = q_ref.shape[0]
    kvcat = jnp.concatenate([kvp_ref[...], kvc_ref[...], kvn_ref[...], kvx_ref[...]], axis=0)
    kcat = kvcat[:, :LANES]
    vcat = kvcat[:, LANES:]
    nkeys = kvcat.shape[0]
    rows = A_GROUP * blk
    bias = bias_ref[...]
    lane = lax.broadcasted_iota(jnp.int32, (blk, LANES), 1)
    rowg = lax.broadcasted_iota(jnp.int32, (rows, 1), 0) // blk
    q = q_ref[...]

    def logits(h):
        half = (lane >= h * HEAD_DIM) & (lane < (h + 1) * HEAD_DIM)
        qh = jnp.concatenate(
            [jnp.where(half, q[:, g * LANES:(g + 1) * LANES], jnp.zeros((), BF16)) for g in range(A_GROUP)], axis=0)
        return _dot_nt(qh, kcat)

    raw = [logits(h) for h in range(A_KV_HEADS)]
    outs = []
    for h in range(A_KV_HEADS):
        s = (raw[h].reshape(A_GROUP, blk, nkeys) + bias[None]).reshape(rows, nkeys)
        sink = jnp.zeros((rows, 1), F32)
        for g in range(A_GROUP):
            sink = jnp.where(rowg == g, sink_ref[h * A_GROUP + g] * LOG2E, sink)
        m = jnp.maximum(jnp.max(s, axis=1, keepdims=True), sink)
        p = jnp.exp2(s - m)
        den = jnp.sum(p, axis=1, keepdims=True) + jnp.exp2(sink - m)
        outs.append(_dot(p.astype(BF16), vcat) / den)
    for g in range(A_GROUP):
        o_ref[:, g * LANES:(g + 1) * LANES] = jnp.where(
            lane < HEAD_DIM, outs[0][g * blk:(g + 1) * blk], outs[1][g * blk:(g + 1) * blk]).astype(o_ref.dtype)


def _win_attn_bias(ctx_len):
    qi = jnp.arange(WQ)[:, None]
    kj = jnp.arange(WQ + 2 * A_WINDOW)[None, :]
    band = jnp.abs(kj - A_WINDOW - qi) <= A_WINDOW
    variants = [jnp.zeros_like(band), band & (kj >= A_WINDOW), band, band & (kj < A_WINDOW + WQ)]
    win = jnp.stack([jnp.where(v_, 0.0, NEG) for v_ in variants]).astype(F32)
    return jnp.concatenate([win, jnp.zeros((4, WQ, ctx_len), F32)], axis=2)


def _win_attn_call(q, kv, sink, ctx_len):
    bsz, s, qw = q.shape
    kw = kv.shape[-1]
    nq = s // WQ
    cq = ctx_len // WQ
    per = WQ // A_WINDOW
    nblk = s // A_WINDOW
    assert ctx_len % WQ == 0 and nq - cq >= 2
    bias = _win_attn_bias(ctx_len)
    clamp = lambda j: jnp.clip(j, cq * per, nblk - 1)
    variant = lambda i: jnp.where(i < cq, 0, jnp.where(i == cq, 1, jnp.where(i == nq - 1, 3, 2)))
    kv_specs = [pl.BlockSpec((None, A_WINDOW, kw), lambda b, i: (b, clamp(i * per - 1), 0)),
                pl.BlockSpec((None, WQ, kw), lambda b, i: (b, i, 0)),
                pl.BlockSpec((None, A_WINDOW, kw), lambda b, i: (b, clamp((i + 1) * per), 0)),
                pl.BlockSpec((None, ctx_len, kw), lambda b, i: (b, 0, 0))]
    return pl.pallas_call(
        _win_attn_body,
        grid=(bsz, nq),
        in_specs=[pl.BlockSpec(memory_space=pltpu.SMEM),
                  pl.BlockSpec((None,) + bias.shape[1:], lambda b, i: (variant(i), 0, 0)),
                  pl.BlockSpec((None, WQ, qw), lambda b, i: (b, i, 0))] + kv_specs,
        out_specs=pl.BlockSpec((None, WQ, qw), lambda b, i: (b, i, 0)),
        out_shape=jax.ShapeDtypeStruct((bsz, s, qw), BF16),
        compiler_params=_cparams("parallel", "arbitrary"),
        name="window_attention",
    )(sink, bias, q, kv, kv, kv, kv)


def _bwd_chunk(j, ctx_chunks, n_chunks):
    return jnp.where(j < ctx_chunks, ctx_chunks - 1 - j, n_chunks + ctx_chunks - 1 - j)


def _scan_rows(dirs):
    return [(bb, d, tuple(r.at[bb] for r in refs)) for bb in range(SCAN_ROWS) for d, refs in enumerate(dirs)]


def _tri_masks(d):
    row = lax.broadcasted_iota(jnp.int32, (CHUNK, CHUNK), 0)
    col = lax.broadcasted_iota(jnp.int32, (CHUNK, CHUNK), 1)
    keep = (col <= row) if d == 0 else (col >= row)
    return keep, jnp.where(keep, 1.0, 0.0).astype(BF16)


def _mlstm_body(qkf_ref, vf_ref, gf_ref, qkb_ref, vb_ref, gb_ref, bias_ref, hf_ref, hb_ref, c_scr, m_scr):
    L = CHUNK
    dk = HEAD_DIM
    dv = LANES

    @pl.when(pl.program_id(1) == 0)
    def _():
        c_scr[...] = jnp.zeros_like(c_scr)
        m_scr[...] = jnp.zeros_like(m_scr)

    lane = lax.broadcasted_iota(jnp.int32, (L, LANES), 1)
    srow = lax.broadcasted_iota(jnp.int32, (LANES, 2 * dv), 0)
    ones_v = jnp.ones((L, dv), BF16)
    dirs = ((qkf_ref, vf_ref, gf_ref, hf_ref), (qkb_ref, vb_ref, gb_ref, hb_ref))
    for bb, d, (qk_ref, v_ref, g_ref, o_ref) in _scan_rows(dirs):
        keep, tri = _tri_masks(d)
        gates = g_ref[...] + bias_ref[...]
        gates_t = gates.T[0:16, :]
        b_col = _cumsum_cols(tri, _log_sigmoid(gates))
        b_row = _cumsum_rows(_log_sigmoid(gates_t), tri)
        k_pairs = [qk_ref[:, B_HEADS * dk + p * LANES:B_HEADS * dk + (p + 1) * LANES] for p in range(B_HEADS // 2)]
        kt_pairs = [kp.astype(F32).T.astype(BF16) for kp in k_pairs]
        for h in range(B_HEADS):
            ci = d * B_HEADS + h
            cf = 2 * B_HEADS + ci
            r = bb * 2 * B_HEADS + ci
            li_c = jnp.broadcast_to(gates[:, ci:ci + 1], (L, LANES))
            li_r = gates_t[ci:ci + 1, :]
            bc = jnp.broadcast_to(b_col[:, cf:cf + 1], (L, LANES))
            br = b_row[cf:cf + 1, :]
            b_last = bc[L - 1:L, :] if d == 0 else bc[0:1, :]
            m_prev = m_scr[r:r + 1, :]
            pair = (h // 2) * LANES
            half = (lane >= (h % 2) * dk) & (lane < (h % 2 + 1) * dk)
            q = jnp.where(half, qk_ref[:, pair:pair + LANES] * (dk ** -0.5), 0.0).astype(BF16)
            k = k_pairs[h // 2].astype(BF16)
            v = v_ref[:, h * dv:(h + 1) * dv]
            dm = jnp.where(keep, bc - br + li_r, NEG)
            g_in = bc + m_prev
            m_t = jnp.maximum(jnp.broadcast_to(jnp.max(dm, axis=1, keepdims=True), (L, LANES)), g_in)
            sm = (_dot_nt(q, k) * jnp.exp(dm - m_t)).astype(BF16)
            vaug = jnp.concatenate([v, ones_v], axis=1)
            c_prev = c_scr[r]
            intra = _dot(sm, vaug)
            inter = _dot(q, c_prev.astype(BF16))
            a_in = jnp.exp(g_in - m_t)
            den = jnp.maximum(jnp.abs(intra[:, dv:] + a_in * inter[:, dv:]), jnp.exp(-m_t))
            o_ref[:, h * dv:(h + 1) * dv] = ((intra[:, :dv] + a_in * inter[:, :dv]) / den).astype(o_ref.dtype)
            w_c = b_last - bc + li_c
            m_loc = jnp.max(w_c, axis=0, keepdims=True)
            e = jnp.exp(w_c - m_loc)
            ev = jnp.concatenate([e * v.astype(F32), e], axis=1).astype(BF16)
            c_loc = _dot(kt_pairs[h // 2], ev)
            own = (srow >= (h % 2) * dk) & (srow < (h % 2 + 1) * dk)
            m_new = jnp.maximum(b_last + m_prev, m_loc)
            keep_old = jnp.exp(b_last + m_prev - m_new)
            take_new = jnp.exp(m_loc - m_new)
            c_scr[r] = (jnp.concatenate([keep_old, keep_old], axis=1) * c_prev
                        + jnp.where(own, jnp.concatenate([take_new, take_new], axis=1) * c_loc, 0.0))
            m_scr[r:r + 1, :] = m_new


def _scan_specs(width, ctx_chunks, n_chunks):
    fwd = pl.BlockSpec((SCAN_ROWS, CHUNK, width), lambda b, j: (b, j, 0))
    bwd = pl.BlockSpec((SCAN_ROWS, CHUNK, width), lambda b, j: (b, _bwd_chunk(j, ctx_chunks, n_chunks), 0))
    return fwd, bwd


def _mlstm_call(qk, v, gates, bias, ctx_len):
    bsz, s, w = v.shape
    nc = s // CHUNK
    cc = ctx_len // CHUNK
    qf, qb = _scan_specs(qk.shape[-1], cc, nc)
    vf, vb = _scan_specs(w, cc, nc)
    gf, gb = _scan_specs(LANES, cc, nc)
    return pl.pallas_call(
        _mlstm_body,
        grid=(bsz // SCAN_ROWS, nc),
        in_specs=[qf, vf, gf, qb, vb, gb, pl.BlockSpec((1, LANES), lambda b, j: (0, 0))],
        out_specs=[vf, vb],
        out_shape=[jax.ShapeDtypeStruct((bsz, s, w), BF16)] * 2,
        scratch_shapes=[pltpu.VMEM((SCAN_ROWS * 2 * B_HEADS, LANES, 2 * LANES), F32),
                        pltpu.VMEM((SCAN_ROWS * 2 * B_HEADS, LANES), F32)],
        compiler_params=_cparams("parallel", "arbitrary"),
        name="mlstm_scan",
    )(qk, v, gates, qk, v, gates, bias)


def _ssd_body(xf_ref, dtf_ref, xb_ref, dtb_ref, dtbias_ref, nega_ref, skip_ref, yf_ref, yb_ref, s_scr):
    L = CHUNK
    hd = HEAD_DIM
    inner = 4 * D_GROUPS * hd
    hpg = 4

    @pl.when(pl.program_id(1) == 0)
    def _():
        s_scr[...] = jnp.zeros_like(s_scr)

    lane = lax.broadcasted_iota(jnp.int32, (L, LANES), 1)
    lo = lane < hd
    lo1 = lo[0:1, :]
    dirs = ((xf_ref, dtf_ref, yf_ref), (xb_ref, dtb_ref, yb_ref))
    for bb, d, (x_ref, dt_ref, y_ref) in _scan_rows(dirs):
        keep, tri = _tri_masks(d)
        dt = _softplus(dt_ref[...] + dtbias_ref[...])
        a = dt * nega_ref[...]
        dt_t = dt.T[0:16, :]
        a_t = a.T[0:16, :]
        ac_col = _cumsum_cols(tri, a)
        ac_row = _cumsum_rows(a_t, tri)
        for g in range(D_GROUPS):
            bg_f = x_ref[:, inner + g * LANES:inner + (g + 1) * LANES]
            bg = bg_f.astype(BF16)
            bg_t = bg_f.T.astype(BF16)
            cg = x_ref[:, inner + (D_GROUPS + g) * LANES:inner + (D_GROUPS + g + 1) * LANES].astype(BF16)
            cb = _dot_nt(cg, bg)
            for pr in range(2):
                c0 = g * hpg * hd + pr * LANES
                xp = x_ref[:, c0:c0 + LANES]
                xpb = xp.astype(BF16)
                ys, eas, wsts, als = [], [], [], []
                for hh in range(2):
                    col = d * D_GROUPS * hpg + g * hpg + pr * 2 + hh
                    acc = jnp.broadcast_to(ac_col[:, col:col + 1], (L, LANES))
                    acr = ac_row[col:col + 1, :]
                    seg = jnp.where(keep, acc - acr, NEG)
                    mix = (cb * jnp.exp(seg) * dt_t[col:col + 1, :]).astype(BF16)
                    ys.append(_dot(mix, xpb))
                    a_last = acc[L - 1:L, :] if d == 0 else acc[0:1, :]
                    eas.append(jnp.exp(acc))
                    wsts.append(jnp.exp(a_last - acc) * jnp.broadcast_to(dt[:, col:col + 1], (L, LANES)))
                    als.append(jnp.exp(a_last))
                sidx = (bb * 2 + d) * 2 * D_GROUPS + g * 2 + pr
                st = s_scr[sidx]
                y = jnp.where(lo, ys[0], ys[1]) + _dot(cg, st.astype(BF16)) * jnp.where(lo, eas[0], eas[1])
                if d == 0:
                    y = y + skip_ref[:, c0:c0 + LANES] * xp
                y_ref[:, c0:c0 + LANES] = y.astype(y_ref.dtype)
                xw = (xp * jnp.where(lo, wsts[0], wsts[1])).astype(BF16)
                s_scr[sidx] = jnp.where(lo1, als[0], als[1]) * st + _dot(bg_t, xw)


def _ssd_call(xbc, dt, dt_bias, neg_a, skip, ctx_len):
    bsz, s, w = xbc.shape
    nc = s // CHUNK
    cc = ctx_len // CHUNK
    inner = skip.shape[-1]
    xf, xb = _scan_specs(w, cc, nc)
    df, db = _scan_specs(LANES, cc, nc)
    yf, yb = _scan_specs(inner, cc, nc)
    vec = lambda n: pl.BlockSpec((1, n), lambda b, j: (0, 0))
    return pl.pallas_call(
        _ssd_body,
        grid=(bsz // SCAN_ROWS, nc),
        in_specs=[xf, df, xb, db, vec(LANES), vec(LANES), vec(inner)],
        out_specs=[yf, yb],
        out_shape=[jax.ShapeDtypeStruct((bsz, s, inner), BF16)] * 2,
        scratch_shapes=[pltpu.VMEM((SCAN_ROWS * 4 * D_GROUPS, LANES, LANES), F32)],
        compiler_params=_cparams("parallel", "arbitrary"),
        name="ssd_scan",
    )(xbc, dt, xbc, dt, dt_bias, neg_a, skip)


def _diff_attn_body(lam_ref, q_ref, k_ref, v_ref, o_ref, kn_scr, *, lam_init):
    nheads = q_ref.shape[1] // LANES
    r_i = lax.broadcasted_iota(jnp.int32, (LANES, LANES), 0)
    c_i = lax.broadcasted_iota(jnp.int32, (LANES, LANES), 1)
    ind = jnp.where(c_i == r_i // HEAD_DIM, 1.0, 0.0).astype(BF16)

    def max_norm2(x):
        xf = x.astype(F32)
        return jnp.max(_dot((xf * xf).astype(BF16), ind), axis=0, keepdims=True)

    @pl.when(pl.program_id(1) == 0)
    def _():
        for h in range(nheads):
            kn_scr[h:h + 1, :] = max_norm2(k_ref[:, h * LANES:(h + 1) * LANES])

    bound2 = max_norm2(q_ref[:, 0:LANES]) * kn_scr[0:1, :]
    for h in range(1, nheads):
        bound2 = jnp.maximum(bound2, max_norm2(q_ref[:, h * LANES:(h + 1) * LANES]) * kn_scr[h:h + 1, :])
    worst = jnp.max(bound2, axis=1, keepdims=True)[0, 0] * NORM_SLACK
    small = worst < SAFE_LOGIT * SAFE_LOGIT

    @pl.when(small)
    def _():
        _diff_attn_heads(lam_ref, q_ref, k_ref, v_ref, o_ref, lam_init, stabilise=False)

    @pl.when(jnp.logical_not(small))
    def _():
        _diff_attn_heads(lam_ref, q_ref, k_ref, v_ref, o_ref, lam_init, stabilise=True)


def _diff_attn_heads(lam_ref, q_ref, k_ref, v_ref, o_ref, lam_init, stabilise):
    lv = lam_ref[...]
    lam = (jnp.exp(jnp.sum(lv[0:1] * lv[1:2], axis=1, keepdims=True))
           - jnp.exp(jnp.sum(lv[2:3] * lv[3:4], axis=1, keepdims=True)) + lam_init)
    tq = q_ref.shape[0]
    lane = lax.broadcasted_iota(jnp.int32, (tq, LANES), 1)
    nheads = q_ref.shape[1] // LANES

    def logits(h):
        q = q_ref[:, h * LANES:(h + 1) * LANES]
        k = k_ref[:, h * LANES:(h + 1) * LANES]
        return [_dot_nt(jnp.where((lane >= m * HEAD_DIM) & (lane < (m + 1) * HEAD_DIM), q, jnp.zeros((), BF16)), k)
                for m in range(2)]

    s_next = logits(0)
    for h in range(nheads):
        s_cur = s_next
        if h + 1 < nheads:
            s_next = logits(h + 1)
        ps, ls = [], []
        for s in s_cur:
            p = jnp.exp2(s - jnp.max(s, axis=1, keepdims=True)) if stabilise else jnp.exp2(s)
            ps.append(p)
            ls.append(jnp.sum(p, axis=1, keepdims=True))
        a = (ps[0] - (lam * ls[0] / ls[1]) * ps[1]).astype(BF16)
        o_ref[:, h * LANES:(h + 1) * LANES] = _dot(a, v_ref[:, h * LANES:(h + 1) * LANES]) / ls[0]


def _diff_attn_call(q, k, v, lam_vecs, lam_init, ctx_len):
    bsz, s, w = q.shape
    t = s - ctx_len
    off = ctx_len // TQ
    return pl.pallas_call(
        functools.partial(_diff_attn_body, lam_init=lam_init),
        grid=(bsz, t // TQ),
        in_specs=[pl.BlockSpec(lam_vecs.shape, lambda b, i: (0, 0)),
                  pl.BlockSpec((None, TQ, w), lambda b, i: (b, i + off, 0)),
                  pl.BlockSpec((None, s, w), lambda b, i: (b, 0, 0)),
                  pl.BlockSpec((None, s, w), lambda b, i: (b, 0, 0))],
        out_specs=pl.BlockSpec((None, TQ, w), lambda b, i: (b, i, 0)),
        out_shape=jax.ShapeDtypeStruct((bsz, t, w), F32),
        scratch_shapes=[pltpu.VMEM((HALO, LANES), F32)],
        compiler_params=_cparams("parallel", "arbitrary"),
        name="diff_attention",
    )(lam_vecs, q, k, v)


def _group_rms(x, width):
    return jnp.concatenate([_rms(x[:, a:a + width]) for a in range(0, x.shape[1], width)], axis=1)


def _outproj_even_body(ya_ref, hf_ref, hb_ref, og_ref, ng_ref, w_ref, g_ref, out_ref):
    half = ya_ref.shape[1]
    f32 = lambda r: r[...].astype(F32)
    hn = _group_rms(f32(hf_ref) + f32(hb_ref), LANES) * ng_ref[...] * _sigmoid(f32(og_ref))
    y = _dot(ya_ref[...], w_ref[0:half, :]) + _dot(hn.astype(BF16), w_ref[half:, :])
    out_ref[...] = (_rms(y) * g_ref[...]).astype(out_ref.dtype)


def _outproj_odd_body(oa_ref, yf_ref, yb_ref, z_ref, cg_ref, dg_ref, w_ref, g_ref, out_ref, *, out_scale):
    half = oa_ref.shape[1]
    on = _group_rms(oa_ref[...], LANES) * cg_ref[...] * out_scale
    f32 = lambda r: r[...].astype(F32)
    yz = (f32(yf_ref) + f32(yb_ref)) * _silu(f32(z_ref))
    sn = _group_rms(yz, half // D_GROUPS) * dg_ref[...]
    y = _dot(on.astype(BF16), w_ref[0:half, :]) + _dot(sn.astype(BF16), w_ref[half:, :])
    out_ref[...] = (_rms(y) * g_ref[...]).astype(out_ref.dtype)


def _outproj_call(body, acts, act_offs, vecs, w, g, n_t):
    bsz = acts[0].shape[0]
    d = w.shape[1]

    def act_spec(arr, off):
        return pl.BlockSpec((None, TM, arr.shape[-1]), lambda b, t: (b, t + off, 0))

    in_specs = ([act_spec(a_, o_) for a_, o_ in zip(acts, act_offs)]
                + [pl.BlockSpec(v_.shape, lambda b, t: (0, 0)) for v_ in vecs]
                + [pl.BlockSpec(w.shape, lambda b, t: (0, 0)),
                   pl.BlockSpec((1, d), lambda b, t: (0, 0))])
    return pl.pallas_call(
        body, grid=(bsz, n_t), in_specs=in_specs,
        out_specs=pl.BlockSpec((None, TM, d), lambda b, t: (b, t, 0)),
        out_shape=jax.ShapeDtypeStruct((bsz, n_t * TM, d), BF16),
        compiler_params=_cparams("parallel", "arbitrary"),
        name="out_proj",
    )(*acts, *vecs, w, g)


def _ffn_body(*refs, n_src, ctx_tiles, t_off, nt, seg_starts):
    mains, prevs, nexts = refs[:n_src], refs[n_src:2 * n_src], refs[2 * n_src:3 * n_src]
    (ym_ref, yp_ref, yn_ref, gate1_ref, g2_ref, sh_ref, sc_ref, wg_ref, wu_ref, cw_ref, wd_ref, g3_ref, gate_ref,
     out_ref, gscr, uscr) = refs[3 * n_src:]
    t = pl.program_id(1)
    prev_ok, nxt_ok = _halo_flags(t, nt, seg_starts)

    def resid(h_refs, y):
        return _load_rows(h_refs, ctx_tiles, t_off) + gate1_ref[...] * y

    def normmod(h):
        return (_rms(h) * g2_ref[...]) * (1.0 + sc_ref[...]) + sh_ref[...]

    hm = resid(mains, ym_ref[...].astype(F32))
    um = normmod(hm)
    up = jnp.where(prev_ok, normmod(resid(prevs, yp_ref[...].astype(F32)[Y_HALO - HALO:, :])), 0.0)
    un = jnp.where(nxt_ok, normmod(resid(nexts, yn_ref[...].astype(F32)[:HALO, :])), 0.0)
    ucat = jnp.concatenate([up, um, un], axis=0).astype(BF16)
    umb = um.astype(BF16)

    nchunk = wg_ref.shape[1] // FFN_CHUNK
    cols = lambda c: slice(c * FFN_CHUNK, (c + 1) * FFN_CHUNK)

    def front(c):
        gscr[c % 2] = _dot(ucat, wg_ref[:, cols(c)])
        uscr[c % 2] = _dot(umb, wu_ref[:, cols(c)])

    front(0)
    y = None
    for c in range(nchunk):
        if c + 1 < nchunk:
            front(c + 1)
        gbuf = gscr.at[c % 2]
        gc = (gbuf[pl.ds(HALO - 1, TM), :] * cw_ref[0:1, cols(c)] + gbuf[pl.ds(HALO, TM), :] * cw_ref[1:2, cols(c)]
              + gbuf[pl.ds(HALO + 1, TM), :] * cw_ref[2:3, cols(c)])
        hid = (_silu(gc) * uscr[c % 2]).astype(BF16)
        part = _dot(hid, wd_ref[cols(c), :])
        y = part if y is None else y + part
    out_ref[...] = hm + gate_ref[...] * (_rms(y) * g3_ref[...])


def _ffn_call(hs, yn, gate1, g2, shift, scale, wg, wu, cw, wd, g3, gate2, ctx_tiles, t_off, seg_starts):
    bsz, rows, d = yn.shape
    n_t = rows // TM
    per = TM // Y_HALO
    last = rows // Y_HALO - 1
    const = lambda arr: pl.BlockSpec(arr.shape, lambda b, t: (0, 0))
    vec = pl.BlockSpec((1, d), lambda b, t: (0, 0))
    mod = pl.BlockSpec((None, None, 1, d), _seg_map(ctx_tiles, t_off))
    y_specs = [pl.BlockSpec((None, TM, d), lambda b, t: (b, t, 0)),
               pl.BlockSpec((None, Y_HALO, d), lambda b, t: (b, jnp.maximum(t * per - 1, 0), 0)),
               pl.BlockSpec((None, Y_HALO, d), lambda b, t: (b, jnp.minimum((t + 1) * per, last), 0))]
    body = functools.partial(_ffn_body, n_src=len(hs), ctx_tiles=ctx_tiles, t_off=t_off, nt=n_t, seg_starts=seg_starts)
    return pl.pallas_call(
        body, grid=(bsz, n_t),
        in_specs=(_row_specs(hs, ctx_tiles, t_off) + _halo_row_specs(hs, ctx_tiles, t_off) + y_specs
                  + [mod, vec, mod, mod, const(wg), const(wu), const(cw), const(wd), vec, mod]),
        out_specs=pl.BlockSpec((None, TM, d), lambda b, t: (b, t, 0)),
        out_shape=jax.ShapeDtypeStruct((bsz, rows, d), F32),
        scratch_shapes=[pltpu.VMEM((2, TM + 2 * HALO, FFN_CHUNK), F32), pltpu.VMEM((2, TM, FFN_CHUNK), F32)],
        compiler_params=_cparams("parallel", "arbitrary"),
        name="conv_ffn",
    )(*hs, *hs, *hs, yn, yn, yn, gate1, g2, shift, scale, wg, wu, cw, wd, g3, gate2)


def _rope_tables(ctx_len, t):
    pos = jnp.arange(t)
    row = (pos // GRID_W).astype(F32)
    col = (pos % GRID_W).astype(F32)
    nf = HEAD_DIM // 4
    inv = ROPE_BASE ** (-jnp.arange(nf, dtype=F32) / nf)
    ar = row[:, None] * inv
    ac = col[:, None] * inv
    cos = jnp.concatenate([jnp.cos(ar), jnp.cos(ar), jnp.cos(ac), jnp.cos(ac)], axis=1)
    sin = jnp.concatenate([-jnp.sin(ar), jnp.sin(ar), -jnp.sin(ac), jnp.sin(ac)], axis=1)
    cos = jnp.concatenate([jnp.ones((ctx_len, HEAD_DIM), F32), cos], axis=0)
    sin = jnp.concatenate([jnp.zeros((ctx_len, HEAD_DIM), F32), sin], axis=0)
    return jnp.tile(cos, (1, LANES // HEAD_DIM)), jnp.tile(sin, (1, LANES // HEAD_DIM))


def _pad_cols(w, n):
    return jnp.pad(w, ((0, 0), (0, n - w.shape[1])))


def _gqa_perm():
    idx = [(h * A_GROUP + g) * HEAD_DIM + dd
           for g in range(A_GROUP) for h in range(A_KV_HEADS) for dd in range(HEAD_DIM)]
    return jnp.array(idx, dtype=jnp.int32)


def _ffn_weights(w_gate, w_up, conv, w_down):
    return w_gate.astype(BF16), w_up.astype(BF16), conv, w_down.astype(BF16)


def kernel(x, c, ctx, c_ctx, mod_w, mod_b, norm_g, ffn_w_gate, ffn_w_up, ffn_conv, ffn_w_down, ev_w_in, ev_w_out, a_sink, b_conv, b_gate_b, b_norm_g, od_w_in, od_w_out, c_lambda, c_norm_g, d_conv, d_conv_b, d_dt_bias, d_a_log, d_skip, d_norm_g):
    bsz, t, d = x.shape
    ctx_len = ctx.shape[1]
    depth = mod_w.shape[0]
    s = ctx_len + t
    half = d // 2
    assert ctx_len % TM == 0 and t % TM == 0 and t % GRID_W == 0 and ffn_w_gate.shape[-1] % FFN_CHUNK == 0
    assert bsz % SCAN_ROWS == 0
    n_t = s // TM
    ctx_tiles = ctx_len // TM

    mod_rows = -(-(bsz + 1) // HALO) * HALO
    cc = jnp.concatenate([c, c_ctx[None, :], jnp.zeros((mod_rows - bsz - 1, d), F32)], axis=0)
    mods = _mod_call(cc, mod_w, mod_b).reshape(depth, mod_rows, 6, d)

    def layer_mods(l):
        lat = mods[l, :bsz]
        cx = jnp.broadcast_to(mods[l, bsz][None], (bsz, 6, d))
        m = jnp.stack([cx, lat], axis=1)
        return [m[:, :, i][:, :, None, :] for i in range(6)]

    cos, sin = _rope_tables(ctx_len, t)
    hs = (ctx, x)

    for l in range(depth):
        last = l == depth - 1
        j = l // 2
        m = layer_mods(l)
        g = norm_g[l]
        wg, wu, cw, wd = _ffn_weights(ffn_w_gate[l], ffn_w_up[l], ffn_conv[l], ffn_w_down[l])
        if l % 2 == 0:
            w_in = ev_w_in[j]
            perm = _gqa_perm()
            akv = A_KV_HEADS * HEAD_DIM
            o1 = half + 2 * akv
            w_cat = jnp.concatenate([w_in[:, :half][:, perm], w_in[:, half:o1 + 3 * half],
                                     _pad_cols(w_in[:, o1 + 3 * half:], LANES)], axis=1).astype(BF16)
            groups = ((0, half, half, HEAD_DIM ** -0.5 * LOG2E), (half, 2 * akv, akv, 1.0),
                      (o1, half, 0, 1.0), (o1 + half, half, 0, 1.0), (o1 + 2 * half, half, 0, 1.0),
                      (o1 + 3 * half, LANES, 0, 1.0))
            q, kv, qkc, mv, mo, gates = _inproj_call(
                hs, g[0:1], m[0], m[1], cos, sin, w_cat, groups, (BF16, BF16, BF16, BF16, BF16, F32), ctx_tiles,
                2, b_conv[j], jnp.zeros((1, half), F32))
            ya = _win_attn_call(q, kv, a_sink[j], ctx_len)
            hf, hb = _mlstm_call(qkc, mv, gates, _pad_cols(b_gate_b[j][None, :], LANES), ctx_len)
            w_out = ev_w_out[j]
            w_out = jnp.concatenate([w_out[:half][perm], w_out[half:]], axis=0).astype(BF16)
            acts, vecs, body = (ya, hf, hb, mo), (b_norm_g[j][None, :],), _outproj_even_body
        else:
            w_in = od_w_in[j]
            xbc_w = 2 * half
            w_cat = jnp.concatenate([w_in[:, :4 * half + xbc_w],
                                     _pad_cols(w_in[:, 4 * half + xbc_w:], LANES)], axis=1).astype(BF16)
            groups = ((0, half, half, HEAD_DIM ** -0.5 * LOG2E), (half, half, half, 1.0), (2 * half, half, 0, 1.0),
                      (3 * half, half, 0, 1.0), (4 * half, xbc_w, 0, 1.0), (4 * half + xbc_w, LANES, 0, 1.0))
            q, k, v, z, xbcc, dt = _inproj_call(
                hs, g[0:1], m[0], m[1], cos, sin, w_cat, groups, (BF16, BF16, BF16, BF16, F32, F32), ctx_tiles,
                4, d_conv[j], d_conv_b[j][None, :])
            lam_init = 0.8 - 0.6 * math.exp(-LAM_DEPTH_RATE * l)
            oa = _diff_attn_call(q, k, v, c_lambda[j], lam_init, ctx_len)
            dt_bias = _pad_cols(d_dt_bias[j].reshape(1, -1), LANES)
            neg_a = _pad_cols(-jnp.exp(d_a_log[j].astype(F32)).reshape(1, -1), LANES)
            skip = jnp.repeat(d_skip[j].astype(F32), HEAD_DIM)[None, :]
            yf, yb = _ssd_call(xbcc, dt, dt_bias, neg_a, skip, ctx_len)
            w_out = od_w_out[j].astype(BF16)
            acts, vecs = (oa, yf, yb, z), (c_norm_g[j][None, :], d_norm_g[j][None, :])
            body = functools.partial(_outproj_odd_body, out_scale=1.0 - lam_init)
        t_off = ctx_tiles if last else 0
        if l % 2 == 0:
            act_offs = (t_off,) * 4
        else:
            assert last, "differential attention is only computed for latent queries"
            act_offs = (0,) + (t_off,) * 3
        yn = _outproj_call(body, acts, act_offs, vecs, w_out, g[1:2], n_t - t_off)
        seg_starts = (0,) if last else (0, ctx_tiles)
        hs = (_ffn_call(hs, yn, m[2], g[2:3], m[3], m[4], wg, wu, cw, wd, g[3:4], m[5], ctx_tiles, t_off,
                        seg_starts),)
    return hs[0]
```

```python
import functools
import math

import jax
import jax.numpy as jnp
from jax import lax
from jax.experimental import pallas as pl
from jax.experimental.pallas import tpu as pltpu

F32 = jnp.float32
BF16 = jnp.bfloat16

EPS = 1e-6
ROPE_BASE = 10000.0
GRID_W = 64
HEAD_DIM = 64
A_KV_HEADS = 2
A_GROUP = 4
A_WINDOW = 128
B_HEADS = 4
D_GROUPS = 2
LAM_DEPTH_RATE = 0.3

LANES = 128
MXU_COLS = 256
HALO = 8
Y_HALO = 16
TM = 256
CHUNK = 128
SCAN_ROWS = 2
FFN_CHUNK = 256
TQ = 256
WQ = 256
NEG = -1e30
SAFE_LOGIT = 40.0
NORM_SLACK = 1.05
LOG2E = math.log2(math.e)
VMEM_LIMIT = 56 * 1024 * 1024


def _cparams(*sem):
    return pltpu.CompilerParams(dimension_semantics=sem, vmem_limit_bytes=VMEM_LIMIT)


def _sigmoid(x):
    return 0.5 * jnp.tanh(0.5 * x) + 0.5


def _silu(x):
    return x * _sigmoid(x)


def _softplus(x):
    return jnp.maximum(x, 0.0) + jnp.log(1.0 + jnp.exp(-jnp.abs(x)))


def _log_sigmoid(x):
    return -_softplus(-x)


def _rms(x):
    return x * lax.rsqrt(jnp.mean(x * x, axis=-1, keepdims=True) + EPS)


def _dot(a, b):
    return jnp.dot(a, b, preferred_element_type=F32)


def _dot_nt(a, b):
    return lax.dot_general(a, b, (((1,), (1,)), ((), ())), preferred_element_type=F32)


def _dot_tn(a, b):
    return lax.dot_general(a, b, (((0,), (0,)), ((), ())), preferred_element_type=F32)


def _split3(x):
    x1 = x.astype(BF16)
    r1 = x - x1.astype(F32)
    x2 = r1.astype(BF16)
    x3 = (r1 - x2.astype(F32)).astype(BF16)
    return x1, x2, x3


def _cumsum_cols(tri, x):
    x1, x2, x3 = _split3(x)
    return _dot(tri, x1) + _dot(tri, x2) + _dot(tri, x3)


def _cumsum_rows(x, tri):
    x1, x2, x3 = _split3(x)
    return _dot_nt(x1, tri) + _dot_nt(x2, tri) + _dot_nt(x3, tri)


def _mod_body(c_ref, w_ref, b_ref, o_ref):
    a = _silu(c_ref[...]).astype(BF16)
    o_ref[...] = _dot(a, w_ref[...].astype(BF16)) + b_ref[...]


def _mod_call(cc, mod_w, mod_b):
    depth, d, n6 = mod_w.shape
    rows = cc.shape[0]
    tn = 1024
    return pl.pallas_call(
        _mod_body,
        grid=(depth, n6 // tn),
        in_specs=[pl.BlockSpec((rows, d), lambda l, j: (0, 0)),
                  pl.BlockSpec((None, d, tn), lambda l, j: (l, 0, j)),
                  pl.BlockSpec((None, 1, tn), lambda l, j: (l, 0, j))],
        out_specs=pl.BlockSpec((None, rows, tn), lambda l, j: (l, 0, j)),
        out_shape=jax.ShapeDtypeStruct((depth, rows, n6), F32),
        compiler_params=_cparams("arbitrary", "arbitrary"),
        name="modulation",
    )(cc, mod_w, mod_b.reshape(depth, 1, n6))


def _rope_block(y, cos, sin, lo):
    partner = jnp.where(lo, pltpu.roll(y, LANES - 16, 1), pltpu.roll(y, 16, 1))
    return y * cos + partner * sin


def _grid_bt(b, t):
    return b, t


def _row_specs(hs, ctx_tiles, t_off, bt=_grid_bt):
    d = hs[0].shape[-1]

    def spec(tile_of):
        def index_map(*g):
            b, t = bt(*g)
            return b, tile_of(t + t_off), 0
        return pl.BlockSpec((None, TM, d), index_map)

    if len(hs) == 1:
        return [spec(lambda t: t)]
    return [spec(lambda t: jnp.minimum(t, ctx_tiles - 1)), spec(lambda t: jnp.maximum(t - ctx_tiles, 0))]


def _load_rows(h_refs, ctx_tiles, t_off, t=None):
    if len(h_refs) == 1:
        return h_refs[0][...]
    t = pl.program_id(1) if t is None else t
    return jnp.where(t + t_off < ctx_tiles, h_refs[0][...], h_refs[1][...])


def _halo_row_specs(hs, ctx_tiles, t_off=0, bt=_grid_bt):
    d = hs[0].shape[-1]
    per = TM // HALO
    offs = (0,) if len(hs) == 1 else (0, ctx_tiles)

    def spec(off, last, first_row_block):
        def index_map(*g):
            b, t = bt(*g)
            return b, jnp.clip(first_row_block(t + t_off - off), 0, last), 0
        return pl.BlockSpec((None, HALO, d), index_map)

    lasts = [h_.shape[1] // HALO - 1 for h_ in hs]
    prevs = [spec(off, last, lambda t: t * per - 1) for off, last in zip(offs, lasts)]
    nexts = [spec(off, last, lambda t: (t + 1) * per) for off, last in zip(offs, lasts)]
    return prevs + nexts


def _inproj_body(*refs, groups, conv_idx, n_src, ctx_tiles, nt, seg_starts):
    mains, prevs, nexts = refs[:n_src], refs[n_src:2 * n_src], refs[2 * n_src:3 * n_src]
    g_ref, sh_ref, sc_ref, cos_ref, sin_ref, w_ref, cw_ref, cb_ref = refs[3 * n_src:3 * n_src + 8]
    out_refs = refs[3 * n_src + 8:-2]
    scr, ubuf = refs[-2:]

    def normmod(h):
        return (_rms(h) * g_ref[...]) * (1.0 + sc_ref[...]) + sh_ref[...]

    ubuf[0:TM, :] = normmod(_load_rows(mains, ctx_tiles, 0)).astype(BF16)
    prev_ok, nxt_ok = _halo_flags(pl.program_id(1), nt, seg_starts)
    up = jnp.where(prev_ok, normmod(_load_rows(prevs, ctx_tiles, 0)), 0.0)
    un = jnp.where(nxt_ok, normmod(_load_rows(nexts, ctx_tiles, 0)), 0.0)
    ubuf[TM:TM + 2 * HALO, :] = jnp.concatenate([up, un], axis=0).astype(BF16)
    cos = cos_ref[...]
    sin = sin_ref[...]
    lane = lax.broadcasted_iota(jnp.int32, cos.shape, 1)
    lo = (lane & 31) < 16
    c_start, c_width = groups[conv_idx][:2]
    c_out = out_refs[conv_idx]
    taps = cw_ref.shape[0]
    for a in range(0, c_width, MXU_COLS):
        proj = _dot(ubuf[...], w_ref[:, c_start + a:c_start + a + MXU_COLS])
        scr[0:HALO, a:a + MXU_COLS] = proj[TM:TM + HALO]
        scr[HALO:HALO + TM, a:a + MXU_COLS] = proj[0:TM]
        scr[HALO + TM:, a:a + MXU_COLS] = proj[TM + HALO:]

    def conv_piece(a):
        cols = slice(a, a + MXU_COLS)
        y = scr[pl.ds(HALO - taps // 2, TM), cols] * cw_ref[0:1, cols]
        for j in range(1, taps):
            y = y + scr[pl.ds(HALO - taps // 2 + j, TM), cols] * cw_ref[j:j + 1, cols]
        c_out[:, cols] = _silu(y + cb_ref[:, cols]).astype(c_out.dtype)

    def project_piece(gi, a, step):
        start, _, rope_cols, qscale = groups[gi]
        y = _dot(ubuf[0:TM, :], w_ref[:, start + a:start + a + step])
        for r in range(0, step, LANES):
            yr = y[:, r:r + LANES]
            if a + r < rope_cols:
                yr = _rope_block(yr, cos, sin, lo)
            if qscale != 1.0:
                yr = yr * qscale
            out_refs[gi][:, a + r:a + r + LANES] = yr.astype(out_refs[gi].dtype)

    conv_pieces = [functools.partial(conv_piece, a) for a in range(0, c_width, MXU_COLS)]
    proj_pieces = []
    for gi, (_, width, rope_cols, _) in enumerate(groups):
        if gi != conv_idx:
            step = min(width, MXU_COLS if rope_cols else 2 * MXU_COLS)
            proj_pieces += [functools.partial(project_piece, gi, a, step) for a in range(0, width, step)]
    for i in range(max(len(conv_pieces), len(proj_pieces))):
        if i < len(proj_pieces):
            proj_pieces[i]()
        if i < len(conv_pieces):
            conv_pieces[i]()


def _seg_map(ctx_tiles, off, bt=_grid_bt):
    def index_map(*g):
        b, t = bt(*g)
        return b, jnp.where(t + off >= ctx_tiles, 1, 0), 0, 0
    return index_map


def _inproj_call(hs, g, shift, scale, cos, sin, w, groups, dtypes, ctx_tiles, conv_idx, conv_w, conv_b):
    bsz, _, d = hs[0].shape
    s = sum(h_.shape[1] for h_ in hs)
    nt = s // TM
    seg = _seg_map(ctx_tiles, 0)
    const = lambda arr: pl.BlockSpec(arr.shape, lambda b, t: (0, 0))
    in_specs = _row_specs(hs, ctx_tiles, 0) + _halo_row_specs(hs, ctx_tiles) + [
                pl.BlockSpec((1, d), lambda b, t: (0, 0)),
                pl.BlockSpec((None, None, 1, d), seg),
                pl.BlockSpec((None, None, 1, d), seg),
                pl.BlockSpec((TM, LANES), lambda b, t: (t, 0)),
                pl.BlockSpec((TM, LANES), lambda b, t: (t, 0)),
                const(w), const(conv_w), const(conv_b)]
    out_specs = [pl.BlockSpec((None, TM, gr[1]), lambda b, t: (b, t, 0)) for gr in groups]
    out_shape = [jax.ShapeDtypeStruct((bsz, s, gr[1]), dt) for gr, dt in zip(groups, dtypes)]
    body = functools.partial(_inproj_body, groups=groups, conv_idx=conv_idx, n_src=len(hs), ctx_tiles=ctx_tiles,
                             nt=nt, seg_starts=(0, ctx_tiles))
    return pl.pallas_call(
        body, grid=(bsz, nt), in_specs=in_specs, out_specs=out_specs, out_shape=out_shape,
        scratch_shapes=[pltpu.VMEM((TM + 2 * HALO, groups[conv_idx][1]), F32),
                        pltpu.VMEM((TM + 2 * HALO, d), BF16)],
        compiler_params=_cparams("parallel", "arbitrary"),
        name="in_proj",
    )(*hs, *hs, *hs, g, shift, scale, cos, sin, w, conv_w, conv_b)


def _halo_flags(t, nt, seg_starts):
    prev_ok = t >= 0
    nxt_ok = (t + 1) < nt
    for s0 in seg_starts:
        prev_ok = jnp.logical_and(prev_ok, t != s0)
        nxt_ok = jnp.logical_and(nxt_ok, (t + 1) != s0)
    return prev_ok, nxt_ok


def _max_half_norm2():
    r_i = lax.broadcasted_iota(jnp.int32, (LANES, LANES), 0)
    c_i = lax.broadcasted_iota(jnp.int32, (LANES, LANES), 1)
    ind = jnp.where(c_i == r_i // HEAD_DIM, 1.0, 0.0).astype(BF16)

    def max_norm2(x):
        xf = x.astype(F32)
        return jnp.max(_dot((xf * xf).astype(BF16), ind), axis=0, keepdims=True)
    return max_norm2


def _win_attn_body(sink_ref, bias_ref, q_ref, kvp_ref, kvc_ref, kvn_ref, kvx_ref, o_ref):
    max_norm2 = _max_half_norm2()
    qn2 = max_norm2(q_ref[:, 0:LANES])
    for g in range(1, A_GROUP):
        qn2 = jnp.maximum(qn2, max_norm2(q_ref[:, g * LANES:(g + 1) * LANES]))
    kn2 = max_norm2(kvc_ref[:, 0:LANES])
    for r in (kvp_ref, kvn_ref, kvx_ref):
        kn2 = jnp.maximum(kn2, max_norm2(r[:, 0:LANES]))
    worst = jnp.max(qn2 * kn2, axis=1, keepdims=True)[0, 0] * NORM_SLACK
    sink_max = jnp.abs(sink_ref[0])
    for i in range(1, A_KV_HEADS * A_GROUP):
        sink_max = jnp.maximum(sink_max, jnp.abs(sink_ref[i]))
    small = jnp.logical_and(worst < SAFE_LOGIT * SAFE_LOGIT, sink_max * LOG2E < SAFE_LOGIT)
    refs = (sink_ref, bias_ref, q_ref, kvp_ref, kvc_ref, kvn_ref, kvx_ref, o_ref)

    @pl.when(small)
    def _():
        _win_attn_heads(*refs, stabilise=False)

    @pl.when(jnp.logical_not(small))
    def _():
        _win_attn_heads(*refs, stabilise=True)


def _win_attn_heads(sink_ref, bias_ref, q_ref, kvp_ref, kvc_ref, kvn_ref, kvx_ref, o_ref, stabilise):
    blk = q_ref.shape[0]
    kvcat = jnp.concatenate([kvp_ref[...], kvc_ref[...], kvn_ref[...], kvx_ref[...]], axis=0)
    kcat = kvcat[:, :LANES]
    vcat = kvcat[:, LANES:]
    nkeys = kvcat.shape[0]
    rows = A_GROUP * blk
    bias = bias_ref[...]
    lane = lax.broadcasted_iota(jnp.int32, (blk, LANES), 1)
    rowg = lax.broadcasted_iota(jnp.int32, (rows, 1), 0) // blk
    q = q_ref[...]

    def logits(h):
        half = (lane >= h * HEAD_DIM) & (lane < (h + 1) * HEAD_DIM)
        qh = jnp.concatenate(
            [jnp.where(half, q[:, g * LANES:(g + 1) * LANES], jnp.zeros((), BF16)) for g in range(A_GROUP)], axis=0)
        return _dot_nt(qh, kcat)

    raw = [logits(h) for h in range(A_KV_HEADS)]
    outs = []
    for h in range(A_KV_HEADS):
        s = (raw[h].reshape(A_GROUP, blk, nkeys) + bias[None]).reshape(rows, nkeys)
        sink = jnp.zeros((rows, 1), F32)
        for g in range(A_GROUP):
            sink = jnp.where(rowg == g, sink_ref[h * A_GROUP + g] * LOG2E, sink)
        if stabilise:
            m = jnp.maximum(jnp.max(s, axis=1, keepdims=True), sink)
            p = jnp.exp2(s - m)
            den = jnp.sum(p, axis=1, keepdims=True) + jnp.exp2(sink - m)
        else:
            p = jnp.exp2(s)
            den = jnp.sum(p, axis=1, keepdims=True) + jnp.exp2(sink)
        outs.append(_dot(p.astype(BF16), vcat) / den)
    for g in range(A_GROUP):
        o_ref[:, g * LANES:(g + 1) * LANES] = jnp.where(
            lane < HEAD_DIM, outs[0][g * blk:(g + 1) * blk], outs[1][g * blk:(g + 1) * blk]).astype(o_ref.dtype)


def _win_attn_bias(ctx_len):
    qi = jnp.arange(WQ)[:, None]
    kj = jnp.arange(WQ + 2 * A_WINDOW)[None, :]
    band = jnp.abs(kj - A_WINDOW - qi) <= A_WINDOW
    variants = [jnp.zeros_like(band), band & (kj >= A_WINDOW), band, band & (kj < A_WINDOW + WQ)]
    win = jnp.stack([jnp.where(v_, 0.0, NEG) for v_ in variants]).astype(F32)
    return jnp.concatenate([win, jnp.zeros((4, WQ, ctx_len), F32)], axis=2)


def _win_attn_call(q, kv, sink, ctx_len):
    bsz, s, qw = q.shape
    kw = kv.shape[-1]
    nq = s // WQ
    cq = ctx_len // WQ
    per = WQ // A_WINDOW
    nblk = s // A_WINDOW
    assert ctx_len % WQ == 0 and nq - cq >= 2
    bias = _win_attn_bias(ctx_len)
    clamp = lambda j: jnp.clip(j, cq * per, nblk - 1)
    variant = lambda i: jnp.where(i < cq, 0, jnp.where(i == cq, 1, jnp.where(i == nq - 1, 3, 2)))
    kv_specs = [pl.BlockSpec((None, A_WINDOW, kw), lambda b, i: (b, clamp(i * per - 1), 0)),
                pl.BlockSpec((None, WQ, kw), lambda b, i: (b, i, 0)),
                pl.BlockSpec((None, A_WINDOW, kw), lambda b, i: (b, clamp((i + 1) * per), 0)),
                pl.BlockSpec((None, ctx_len, kw), lambda b, i: (b, 0, 0))]
    return pl.pallas_call(
        _win_attn_body,
        grid=(bsz, nq),
        in_specs=[pl.BlockSpec(memory_space=pltpu.SMEM),
                  pl.BlockSpec((None,) + bias.shape[1:], lambda b, i: (variant(i), 0, 0)),
                  pl.BlockSpec((None, WQ, qw), lambda b, i: (b, i, 0))] + kv_specs,
        out_specs=pl.BlockSpec((None, WQ, qw), lambda b, i: (b, i, 0)),
        out_shape=jax.ShapeDtypeStruct((bsz, s, qw), BF16),
        compiler_params=_cparams("parallel", "arbitrary"),
        name="window_attention",
    )(sink, bias, q, kv, kv, kv, kv)


def _bwd_chunk(j, ctx_chunks, n_chunks):
    return jnp.where(j < ctx_chunks, ctx_chunks - 1 - j, n_chunks + ctx_chunks - 1 - j)


def _scan_rows(dirs):
    return [(bb, d, tuple(r.at[bb] for r in refs)) for bb in range(SCAN_ROWS) for d, refs in enumerate(dirs)]


def _tri_masks(d):
    row = lax.broadcasted_iota(jnp.int32, (CHUNK, CHUNK), 0)
    col = lax.broadcasted_iota(jnp.int32, (CHUNK, CHUNK), 1)
    keep = (col <= row) if d == 0 else (col >= row)
    return keep, jnp.where(keep, 1.0, 0.0).astype(BF16)


def _mlstm_body(qkf_ref, vf_ref, gf_ref, qkb_ref, vb_ref, gb_ref, bias_ref, hf_ref, hb_ref, c_scr, m_scr):
    L = CHUNK
    dk = HEAD_DIM
    dv = LANES

    @pl.when(pl.program_id(1) == 0)
    def _():
        c_scr[...] = jnp.zeros_like(c_scr)
        m_scr[...] = jnp.zeros_like(m_scr)

    lane = lax.broadcasted_iota(jnp.int32, (L, LANES), 1)
    srow = lax.broadcasted_iota(jnp.int32, (LANES, 2 * dv), 0)
    ones_v = jnp.ones((L, dv), BF16)
    dirs = ((qkf_ref, vf_ref, gf_ref, hf_ref), (qkb_ref, vb_ref, gb_ref, hb_ref))
    for bb, d, (qk_ref, v_ref, g_ref, o_ref) in _scan_rows(dirs):
        keep, tri = _tri_masks(d)
        gates = g_ref[...] + bias_ref[...]
        gates_t = gates.T[0:16, :]
        b_col = _cumsum_cols(tri, _log_sigmoid(gates))
        b_row = _cumsum_rows(_log_sigmoid(gates_t), tri)
        k_pairs = [qk_ref[:, B_HEADS * dk + p * LANES:B_HEADS * dk + (p + 1) * LANES] for p in range(B_HEADS // 2)]
        kt_pairs = [kp.astype(F32).T.astype(BF16) for kp in k_pairs]
        for h in range(B_HEADS):
            ci = d * B_HEADS + h
            cf = 2 * B_HEADS + ci
            r = bb * 2 * B_HEADS + ci
            li_c = jnp.broadcast_to(gates[:, ci:ci + 1], (L, LANES))
            li_r = gates_t[ci:ci + 1, :]
            bc = jnp.broadcast_to(b_col[:, cf:cf + 1], (L, LANES))
            br = b_row[cf:cf + 1, :]
            b_last = bc[L - 1:L, :] if d == 0 else bc[0:1, :]
            m_prev = m_scr[r:r + 1, :]
            pair = (h // 2) * LANES
            half = (lane >= (h % 2) * dk) & (lane < (h % 2 + 1) * dk)
            q = jnp.where(half, qk_ref[:, pair:pair + LANES] * (dk ** -0.5), 0.0).astype(BF16)
            k = k_pairs[h // 2].astype(BF16)
            v = v_ref[:, h * dv:(h + 1) * dv]
            dm = jnp.where(keep, bc - br + li_r, NEG)
            g_in = bc + m_prev
            m_t = jnp.maximum(jnp.broadcast_to(jnp.max(dm, axis=1, keepdims=True), (L, LANES)), g_in)
            sm = (_dot_nt(q, k) * jnp.exp(dm - m_t)).astype(BF16)
            vaug = jnp.concatenate([v, ones_v], axis=1)
            c_prev = c_scr[r]
            intra = _dot(sm, vaug)
            inter = _dot(q, c_prev.astype(BF16))
            a_in = jnp.exp(g_in - m_t)
            den = jnp.maximum(jnp.abs(intra[:, dv:] + a_in * inter[:, dv:]), jnp.exp(-m_t))
            o_ref[:, h * dv:(h + 1) * dv] = ((intra[:, :dv] + a_in * inter[:, :dv]) / den).astype(o_ref.dtype)
            w_c = b_last - bc + li_c
            m_loc = jnp.max(w_c, axis=0, keepdims=True)
            e = jnp.exp(w_c - m_loc)
            ev = jnp.concatenate([e * v.astype(F32), e], axis=1).astype(BF16)
            c_loc = _dot(kt_pairs[h // 2], ev)
            own = (srow >= (h % 2) * dk) & (srow < (h % 2 + 1) * dk)
            m_new = jnp.maximum(b_last + m_prev, m_loc)
            keep_old = jnp.exp(b_last + m_prev - m_new)
            take_new = jnp.exp(m_loc - m_new)
            c_scr[r] = (jnp.concatenate([keep_old, keep_old], axis=1) * c_prev
                        + jnp.where(own, jnp.concatenate([take_new, take_new], axis=1) * c_loc, 0.0))
            m_scr[r:r + 1, :] = m_new


def _scan_specs(width, ctx_chunks, n_chunks):
    fwd = pl.BlockSpec((SCAN_ROWS, CHUNK, width), lambda b, j: (b, j, 0))
    bwd = pl.BlockSpec((SCAN_ROWS, CHUNK, width), lambda b, j: (b, _bwd_chunk(j, ctx_chunks, n_chunks), 0))
    return fwd, bwd


def _mlstm_call(qk, v, gates, bias, ctx_len):
    bsz, s, w = v.shape
    nc = s // CHUNK
    cc = ctx_len // CHUNK
    qf, qb = _scan_specs(qk.shape[-1], cc, nc)
    vf, vb = _scan_specs(w, cc, nc)
    gf, gb = _scan_specs(LANES, cc, nc)
    return pl.pallas_call(
        _mlstm_body,
        grid=(bsz // SCAN_ROWS, nc),
        in_specs=[qf, vf, gf, qb, vb, gb, pl.BlockSpec((1, LANES), lambda b, j: (0, 0))],
        out_specs=[vf, vb],
        out_shape=[jax.ShapeDtypeStruct((bsz, s, w), BF16)] * 2,
        scratch_shapes=[pltpu.VMEM((SCAN_ROWS * 2 * B_HEADS, LANES, 2 * LANES), F32),
                        pltpu.VMEM((SCAN_ROWS * 2 * B_HEADS, LANES), F32)],
        compiler_params=_cparams("parallel", "arbitrary"),
        name="mlstm_scan",
    )(qk, v, gates, qk, v, gates, bias)


def _ssd_body(xf_ref, dtf_ref, xb_ref, dtb_ref, dtbias_ref, nega_ref, skip_ref, yf_ref, yb_ref, s_scr):
    L = CHUNK
    hd = HEAD_DIM
    inner = 4 * D_GROUPS * hd
    hpg = 4

    @pl.when(pl.program_id(1) == 0)
    def _():
        s_scr[...] = jnp.zeros_like(s_scr)

    lane = lax.broadcasted_iota(jnp.int32, (L, LANES), 1)
    lo = lane < hd
    lo1 = lo[0:1, :]
    dirs = ((xf_ref, dtf_ref, yf_ref), (xb_ref, dtb_ref, yb_ref))
    for bb, d, (x_ref, dt_ref, y_ref) in _scan_rows(dirs):
        keep, tri = _tri_masks(d)
        dt = _softplus(dt_ref[...] + dtbias_ref[...])
        a = dt * nega_ref[...]
        dt_t = dt.T[0:16, :]
        a_t = a.T[0:16, :]
        ac_col = _cumsum_cols(tri, a)
        ac_row = _cumsum_rows(a_t, tri)
        for g in range(D_GROUPS):
            bg_f = x_ref[:, inner + g * LANES:inner + (g + 1) * LANES]
            bg = bg_f.astype(BF16)
            bg_t = bg_f.T.astype(BF16)
            cg = x_ref[:, inner + (D_GROUPS + g) * LANES:inner + (D_GROUPS + g + 1) * LANES].astype(BF16)
            cb = _dot_nt(cg, bg)
            for pr in range(2):
                c0 = g * hpg * hd + pr * LANES
                xp = x_ref[:, c0:c0 + LANES]
                xpb = xp.astype(BF16)
                ys, eas, wsts, als = [], [], [], []
                for hh in range(2):
                    col = d * D_GROUPS * hpg + g * hpg + pr * 2 + hh
                    acc = jnp.broadcast_to(ac_col[:, col:col + 1], (L, LANES))
                    acr = ac_row[col:col + 1, :]
                    seg = jnp.where(keep, acc - acr, NEG)
                    mix = (cb * jnp.exp(seg) * dt_t[col:col + 1, :]).astype(BF16)
                    ys.append(_dot(mix, xpb))
                    a_last = acc[L - 1:L, :] if d == 0 else acc[0:1, :]
                    eas.append(jnp.exp(acc))
                    wsts.append(jnp.exp(a_last - acc) * jnp.broadcast_to(dt[:, col:col + 1], (L, LANES)))
                    als.append(jnp.exp(a_last))
                sidx = (bb * 2 + d) * 2 * D_GROUPS + g * 2 + pr
                st = s_scr[sidx]
                y = jnp.where(lo, ys[0], ys[1]) + _dot(cg, st.astype(BF16)) * jnp.where(lo, eas[0], eas[1])
                if d == 0:
                    y = y + skip_ref[:, c0:c0 + LANES] * xp
                y_ref[:, c0:c0 + LANES] = y.astype(y_ref.dtype)
                xw = (xp * jnp.where(lo, wsts[0], wsts[1])).astype(BF16)
                s_scr[sidx] = jnp.where(lo1, als[0], als[1]) * st + _dot(bg_t, xw)


def _ssd_call(xbc, dt, dt_bias, neg_a, skip, ctx_len):
    bsz, s, w = xbc.shape
    nc = s // CHUNK
    cc = ctx_len // CHUNK
    inner = skip.shape[-1]
    xf, xb = _scan_specs(w, cc, nc)
    df, db = _scan_specs(LANES, cc, nc)
    yf, yb = _scan_specs(inner, cc, nc)
    vec = lambda n: pl.BlockSpec((1, n), lambda b, j: (0, 0))
    return pl.pallas_call(
        _ssd_body,
        grid=(bsz // SCAN_ROWS, nc),
        in_specs=[xf, df, xb, db, vec(LANES), vec(LANES), vec(inner)],
        out_specs=[yf, yb],
        out_shape=[jax.ShapeDtypeStruct((bsz, s, inner), BF16)] * 2,
        scratch_shapes=[pltpu.VMEM((SCAN_ROWS * 4 * D_GROUPS, LANES, LANES), F32)],
        compiler_params=_cparams("parallel", "arbitrary"),
        name="ssd_scan",
    )(xbc, dt, xbc, dt, dt_bias, neg_a, skip)


def _diff_attn_body(lam_ref, q_ref, k_ref, v_ref, o_ref, kn_scr, *, lam_init):
    nheads = q_ref.shape[1] // LANES
    max_norm2 = _max_half_norm2()

    @pl.when(pl.program_id(1) == 0)
    def _():
        for h in range(nheads):
            kn_scr[h:h + 1, :] = max_norm2(k_ref[:, h * LANES:(h + 1) * LANES])

    bound2 = max_norm2(q_ref[:, 0:LANES]) * kn_scr[0:1, :]
    for h in range(1, nheads):
        bound2 = jnp.maximum(bound2, max_norm2(q_ref[:, h * LANES:(h + 1) * LANES]) * kn_scr[h:h + 1, :])
    worst = jnp.max(bound2, axis=1, keepdims=True)[0, 0] * NORM_SLACK
    small = worst < SAFE_LOGIT * SAFE_LOGIT

    @pl.when(small)
    def _():
        _diff_attn_heads(lam_ref, q_ref, k_ref, v_ref, o_ref, lam_init, stabilise=False)

    @pl.when(jnp.logical_not(small))
    def _():
        _diff_attn_heads(lam_ref, q_ref, k_ref, v_ref, o_ref, lam_init, stabilise=True)


def _diff_attn_heads(lam_ref, q_ref, k_ref, v_ref, o_ref, lam_init, stabilise):
    lv = lam_ref[...]
    lam = (jnp.exp(jnp.sum(lv[0:1] * lv[1:2], axis=1, keepdims=True))
           - jnp.exp(jnp.sum(lv[2:3] * lv[3:4], axis=1, keepdims=True)) + lam_init)
    tq = q_ref.shape[0]
    lane = lax.broadcasted_iota(jnp.int32, (tq, LANES), 1)
    nheads = q_ref.shape[1] // LANES

    def logits(h):
        q = q_ref[:, h * LANES:(h + 1) * LANES]
        k = k_ref[:, h * LANES:(h + 1) * LANES]
        return [_dot_nt(jnp.where((lane >= m * HEAD_DIM) & (lane < (m + 1) * HEAD_DIM), q, jnp.zeros((), BF16)), k)
                for m in range(2)]

    s_next = logits(0)
    for h in range(nheads):
        s_cur = s_next
        if h + 1 < nheads:
            s_next = logits(h + 1)
        ps, ls = [], []
        for s in s_cur:
            p = jnp.exp2(s - jnp.max(s, axis=1, keepdims=True)) if stabilise else jnp.exp2(s)
            ps.append(p)
            ls.append(jnp.sum(p, axis=1, keepdims=True))
        a = (ps[0] - (lam * ls[0] / ls[1]) * ps[1]).astype(BF16)
        o_ref[:, h * LANES:(h + 1) * LANES] = _dot(a, v_ref[:, h * LANES:(h + 1) * LANES]) / ls[0]


def _diff_attn_call(q, k, v, lam_vecs, lam_init, ctx_len):
    bsz, s, w = q.shape
    t = s - ctx_len
    off = ctx_len // TQ
    return pl.pallas_call(
        functools.partial(_diff_attn_body, lam_init=lam_init),
        grid=(bsz, t // TQ),
        in_specs=[pl.BlockSpec(lam_vecs.shape, lambda b, i: (0, 0)),
                  pl.BlockSpec((None, TQ, w), lambda b, i: (b, i + off, 0)),
                  pl.BlockSpec((None, s, w), lambda b, i: (b, 0, 0)),
                  pl.BlockSpec((None, s, w), lambda b, i: (b, 0, 0))],
        out_specs=pl.BlockSpec((None, TQ, w), lambda b, i: (b, i, 0)),
        out_shape=jax.ShapeDtypeStruct((bsz, t, w), F32),
        scratch_shapes=[pltpu.VMEM((HALO, LANES), F32)],
        compiler_params=_cparams("parallel", "arbitrary"),
        name="diff_attention",
    )(lam_vecs, q, k, v)


def _group_rms(x, width):
    return jnp.concatenate([_rms(x[:, a:a + width]) for a in range(0, x.shape[1], width)], axis=1)


def _outproj_even_body(ya_ref, hf_ref, hb_ref, og_ref, ng_ref, w_ref, g_ref, out_ref):
    half = ya_ref.shape[1]
    f32 = lambda r: r[...].astype(F32)
    hn = _group_rms(f32(hf_ref) + f32(hb_ref), LANES) * ng_ref[...] * _sigmoid(f32(og_ref))
    y = _dot(ya_ref[...], w_ref[0:half, :]) + _dot(hn.astype(BF16), w_ref[half:, :])
    out_ref[...] = (_rms(y) * g_ref[...]).astype(out_ref.dtype)


def _outproj_odd_body(oa_ref, yf_ref, yb_ref, z_ref, cg_ref, dg_ref, w_ref, g_ref, out_ref, *, out_scale):
    half = oa_ref.shape[1]
    on = _group_rms(oa_ref[...], LANES) * cg_ref[...] * out_scale
    f32 = lambda r: r[...].astype(F32)
    yz = (f32(yf_ref) + f32(yb_ref)) * _silu(f32(z_ref))
    sn = _group_rms(yz, half // D_GROUPS) * dg_ref[...]
    y = _dot(on.astype(BF16), w_ref[0:half, :]) + _dot(sn.astype(BF16), w_ref[half:, :])
    out_ref[...] = (_rms(y) * g_ref[...]).astype(out_ref.dtype)


def _outproj_call(body, acts, act_offs, vecs, w, g, n_t):
    bsz = acts[0].shape[0]
    d = w.shape[1]

    def act_spec(arr, off):
        return pl.BlockSpec((None, TM, arr.shape[-1]), lambda b, t: (b, t + off, 0))

    in_specs = ([act_spec(a_, o_) for a_, o_ in zip(acts, act_offs)]
                + [pl.BlockSpec(v_.shape, lambda b, t: (0, 0)) for v_ in vecs]
                + [pl.BlockSpec(w.shape, lambda b, t: (0, 0)),
                   pl.BlockSpec((1, d), lambda b, t: (0, 0))])
    return pl.pallas_call(
        body, grid=(bsz, n_t), in_specs=in_specs,
        out_specs=pl.BlockSpec((None, TM, d), lambda b, t: (b, t, 0)),
        out_shape=jax.ShapeDtypeStruct((bsz, n_t * TM, d), BF16),
        compiler_params=_cparams("parallel", "arbitrary"),
        name="out_proj",
    )(*acts, *vecs, w, g)


def _ffn_body(*refs, n_src, ctx_tiles, t_off, nt, seg_starts):
    mains, prevs, nexts = refs[:n_src], refs[n_src:2 * n_src], refs[2 * n_src:3 * n_src]
    (ym_ref, yp_ref, yn_ref, gate1_ref, g2_ref, sh_ref, sc_ref, wg_ref, wu_ref, cw_ref, wd_ref, g3_ref, gate_ref,
     out_ref, gscr, uscr) = refs[3 * n_src:]
    t = pl.program_id(1)
    prev_ok, nxt_ok = _halo_flags(t, nt, seg_starts)

    def resid(h_refs, y):
        return _load_rows(h_refs, ctx_tiles, t_off) + gate1_ref[...] * y

    def normmod(h):
        return (_rms(h) * g2_ref[...]) * (1.0 + sc_ref[...]) + sh_ref[...]

    hm = resid(mains, ym_ref[...].astype(F32))
    um = normmod(hm)
    up = jnp.where(prev_ok, normmod(resid(prevs, yp_ref[...].astype(F32)[Y_HALO - HALO:, :])), 0.0)
    un = jnp.where(nxt_ok, normmod(resid(nexts, yn_ref[...].astype(F32)[:HALO, :])), 0.0)
    ucat = jnp.concatenate([up, um, un], axis=0).astype(BF16)
    umb = um.astype(BF16)

    nchunk = wg_ref.shape[1] // FFN_CHUNK
    cols = lambda c: slice(c * FFN_CHUNK, (c + 1) * FFN_CHUNK)

    def front(c):
        gscr[c % 2] = _dot(ucat, wg_ref[:, cols(c)])
        uscr[c % 2] = _dot(umb, wu_ref[:, cols(c)])

    front(0)
    y = None
    for c in range(nchunk):
        if c + 1 < nchunk:
            front(c + 1)
        gbuf = gscr.at[c % 2]
        gc = (gbuf[pl.ds(HALO - 1, TM), :] * cw_ref[0:1, cols(c)] + gbuf[pl.ds(HALO, TM), :] * cw_ref[1:2, cols(c)]
              + gbuf[pl.ds(HALO + 1, TM), :] * cw_ref[2:3, cols(c)])
        hid = (_silu(gc) * uscr[c % 2]).astype(BF16)
        part = _dot(hid, wd_ref[cols(c), :])
        y = part if y is None else y + part
    out_ref[...] = hm + gate_ref[...] * (_rms(y) * g3_ref[...])


def _ffn_call(hs, yn, gate1, g2, shift, scale, wg, wu, cw, wd, g3, gate2, ctx_tiles, t_off, seg_starts):
    bsz, rows, d = yn.shape
    n_t = rows // TM
    per = TM // Y_HALO
    last = rows // Y_HALO - 1
    const = lambda arr: pl.BlockSpec(arr.shape, lambda b, t: (0, 0))
    vec = pl.BlockSpec((1, d), lambda b, t: (0, 0))
    mod = pl.BlockSpec((None, None, 1, d), _seg_map(ctx_tiles, t_off))
    y_specs = [pl.BlockSpec((None, TM, d), lambda b, t: (b, t, 0)),
               pl.BlockSpec((None, Y_HALO, d), lambda b, t: (b, jnp.maximum(t * per - 1, 0), 0)),
               pl.BlockSpec((None, Y_HALO, d), lambda b, t: (b, jnp.minimum((t + 1) * per, last), 0))]
    body = functools.partial(_ffn_body, n_src=len(hs), ctx_tiles=ctx_tiles, t_off=t_off, nt=n_t, seg_starts=seg_starts)
    return pl.pallas_call(
        body, grid=(bsz, n_t),
        in_specs=(_row_specs(hs, ctx_tiles, t_off) + _halo_row_specs(hs, ctx_tiles, t_off) + y_specs
                  + [mod, vec, mod, mod, const(wg), const(wu), const(cw), const(wd), vec, mod]),
        out_specs=pl.BlockSpec((None, TM, d), lambda b, t: (b, t, 0)),
        out_shape=jax.ShapeDtypeStruct((bsz, rows, d), F32),
        scratch_shapes=[pltpu.VMEM((2, TM + 2 * HALO, FFN_CHUNK), F32), pltpu.VMEM((2, TM, FFN_CHUNK), F32)],
        compiler_params=_cparams("parallel", "arbitrary"),
        name="conv_ffn",
    )(*hs, *hs, *hs, yn, yn, yn, gate1, g2, shift, scale, wg, wu, cw, wd, g3, gate2)


def _rope_tables(ctx_len, t):
    pos = jnp.arange(t)
    row = (pos // GRID_W).astype(F32)
    col = (pos % GRID_W).astype(F32)
    nf = HEAD_DIM // 4
    inv = ROPE_BASE ** (-jnp.arange(nf, dtype=F32) / nf)
    ar = row[:, None] * inv
    ac = col[:, None] * inv
    cos = jnp.concatenate([jnp.cos(ar), jnp.cos(ar), jnp.cos(ac), jnp.cos(ac)], axis=1)
    sin = jnp.concatenate([-jnp.sin(ar), jnp.sin(ar), -jnp.sin(ac), jnp.sin(ac)], axis=1)
    cos = jnp.concatenate([jnp.ones((ctx_len, HEAD_DIM), F32), cos], axis=0)
    sin = jnp.concatenate([jnp.zeros((ctx_len, HEAD_DIM), F32), sin], axis=0)
    return jnp.tile(cos, (1, LANES // HEAD_DIM)), jnp.tile(sin, (1, LANES // HEAD_DIM))


def _pad_cols(w, n):
    return jnp.pad(w, ((0, 0), (0, n - w.shape[1])))


def _gqa_perm():
    idx = [(h * A_GROUP + g) * HEAD_DIM + dd
           for g in range(A_GROUP) for h in range(A_KV_HEADS) for dd in range(HEAD_DIM)]
    return jnp.array(idx, dtype=jnp.int32)


def _ffn_weights(w_gate, w_up, conv, w_down):
    return w_gate.astype(BF16), w_up.astype(BF16), conv, w_down.astype(BF16)


def kernel(x, c, ctx, c_ctx, mod_w, mod_b, norm_g, ffn_w_gate, ffn_w_up, ffn_conv, ffn_w_down, ev_w_in, ev_w_out, a_sink, b_conv, b_gate_b, b_norm_g, od_w_in, od_w_out, c_lambda, c_norm_g, d_conv, d_conv_b, d_dt_bias, d_a_log, d_skip, d_norm_g):
    bsz, t, d = x.shape
    ctx_len = ctx.shape[1]
    depth = mod_w.shape[0]
    s = ctx_len + t
    half = d // 2
    assert ctx_len % TM == 0 and t % TM == 0 and t % GRID_W == 0 and ffn_w_gate.shape[-1] % FFN_CHUNK == 0
    assert bsz % SCAN_ROWS == 0
    n_t = s // TM
    ctx_tiles = ctx_len // TM

    mod_rows = -(-(bsz + 1) // HALO) * HALO
    cc = jnp.concatenate([c, c_ctx[None, :], jnp.zeros((mod_rows - bsz - 1, d), F32)], axis=0)
    mods = _mod_call(cc, mod_w, mod_b).reshape(depth, mod_rows, 6, d)

    def layer_mods(l):
        lat = mods[l, :bsz]
        cx = jnp.broadcast_to(mods[l, bsz][None], (bsz, 6, d))
        m = jnp.stack([cx, lat], axis=1)
        return [m[:, :, i][:, :, None, :] for i in range(6)]

    cos, sin = _rope_tables(ctx_len, t)
    hs = (ctx, x)

    for l in range(depth):
        last = l == depth - 1
        j = l // 2
        m = layer_mods(l)
        g = norm_g[l]
        wg, wu, cw, wd = _ffn_weights(ffn_w_gate[l], ffn_w_up[l], ffn_conv[l], ffn_w_down[l])
        if l % 2 == 0:
            w_in = ev_w_in[j]
            perm = _gqa_perm()
            akv = A_KV_HEADS * HEAD_DIM
            o1 = half + 2 * akv
            w_cat = jnp.concatenate([w_in[:, :half][:, perm], w_in[:, half:o1 + 3 * half],
                                     _pad_cols(w_in[:, o1 + 3 * half:], LANES)], axis=1).astype(BF16)
            groups = ((0, half, half, HEAD_DIM ** -0.5 * LOG2E), (half, 2 * akv, akv, 1.0),
                      (o1, half, 0, 1.0), (o1 + half, half, 0, 1.0), (o1 + 2 * half, half, 0, 1.0),
                      (o1 + 3 * half, LANES, 0, 1.0))
            q, kv, qkc, mv, mo, gates = _inproj_call(
                hs, g[0:1], m[0], m[1], cos, sin, w_cat, groups, (BF16, BF16, BF16, BF16, BF16, F32), ctx_tiles,
                2, b_conv[j], jnp.zeros((1, half), F32))
            ya = _win_attn_call(q, kv, a_sink[j], ctx_len)
            hf, hb = _mlstm_call(qkc, mv, gates, _pad_cols(b_gate_b[j][None, :], LANES), ctx_len)
            w_out = ev_w_out[j]
            w_out = jnp.concatenate([w_out[:half][perm], w_out[half:]], axis=0).astype(BF16)
            acts, vecs, body = (ya, hf, hb, mo), (b_norm_g[j][None, :],), _outproj_even_body
        else:
            w_in = od_w_in[j]
            xbc_w = 2 * half
            w_cat = jnp.concatenate([w_in[:, :4 * half + xbc_w],
                                     _pad_cols(w_in[:, 4 * half + xbc_w:], LANES)], axis=1).astype(BF16)
            groups = ((0, half, half, HEAD_DIM ** -0.5 * LOG2E), (half, half, half, 1.0), (2 * half, half, 0, 1.0),
                      (3 * half, half, 0, 1.0), (4 * half, xbc_w, 0, 1.0), (4 * half + xbc_w, LANES, 0, 1.0))
            q, k, v, z, xbcc, dt = _inproj_call(
                hs, g[0:1], m[0], m[1], cos, sin, w_cat, groups, (BF16, BF16, BF16, BF16, F32, F32), ctx_tiles,
                4, d_conv[j], d_conv_b[j][None, :])
            lam_init = 0.8 - 0.6 * math.exp(-LAM_DEPTH_RATE * l)
            oa = _diff_attn_call(q, k, v, c_lambda[j], lam_init, ctx_len)
            dt_bias = _pad_cols(d_dt_bias[j].reshape(1, -1), LANES)
            neg_a = _pad_cols(-jnp.exp(d_a_log[j].astype(F32)).reshape(1, -1), LANES)
            skip = jnp.repeat(d_skip[j].astype(F32), HEAD_DIM)[None, :]
            yf, yb = _ssd_call(xbcc, dt, dt_bias, neg_a, skip, ctx_len)
            w_out = od_w_out[j].astype(BF16)
            acts, vecs = (oa, yf, yb, z), (c_norm_g[j][None, :], d_norm_g[j][None, :])
            body = functools.partial(_outproj_odd_body, out_scale=1.0 - lam_init)
        t_off = ctx_tiles if last else 0
        if l % 2 == 0:
            act_offs = (t_off,) * 4
        else:
            assert last, "differential attention is only computed for latent queries"
            act_offs = (0,) + (t_off,) * 3
        yn = _outproj_call(body, acts, act_offs, vecs, w_out, g[1:2], n_t - t_off)
        seg_starts = (0,) if last else (0, ctx_tiles)
        hs = (_ffn_call(hs, yn, m[2], g[2:3], m[3], m[4], wg, wu, cw, wd, g[3:4], m[5], ctx_tiles, t_off,
                        seg_starts),)
    return hs[0]
```

```python
import functools
import math

import jax
import jax.numpy as jnp
from jax import lax
from jax.experimental import pallas as pl
from jax.experimental.pallas import tpu as pltpu

F32 = jnp.float32
BF16 = jnp.bfloat16

EPS = 1e-6
ROPE_BASE = 10000.0
GRID_W = 64
HEAD_DIM = 64
A_KV_HEADS = 2
A_GROUP = 4
A_WINDOW = 128
B_HEADS = 4
D_GROUPS = 2
LAM_DEPTH_RATE = 0.3

LANES = 128
MXU_COLS = 256
HALO = 8
Y_HALO = 16
TM = 256
CHUNK = 128
SCAN_ROWS = 4
FFN_CHUNK = 256
TQ = 256
WQ = 256
NEG = -1e30
SAFE_LOGIT = 40.0
NORM_SLACK = 1.05
LOG2E = math.log2(math.e)
VMEM_LIMIT = 56 * 1024 * 1024


def _cparams(*sem):
    return pltpu.CompilerParams(dimension_semantics=sem, vmem_limit_bytes=VMEM_LIMIT)


def _sigmoid(x):
    return 0.5 * jnp.tanh(0.5 * x) + 0.5


def _silu(x):
    return x * _sigmoid(x)


def _softplus(x):
    return jnp.maximum(x, 0.0) + jnp.log(1.0 + jnp.exp(-jnp.abs(x)))


def _log_sigmoid(x):
    return -_softplus(-x)


def _rms(x):
    return x * lax.rsqrt(jnp.mean(x * x, axis=-1, keepdims=True) + EPS)


def _dot(a, b):
    return jnp.dot(a, b, preferred_element_type=F32)


def _dot_nt(a, b):
    return lax.dot_general(a, b, (((1,), (1,)), ((), ())), preferred_element_type=F32)


def _dot_tn(a, b):
    return lax.dot_general(a, b, (((0,), (0,)), ((), ())), preferred_element_type=F32)


def _split3(x):
    x1 = x.astype(BF16)
    r1 = x - x1.astype(F32)
    x2 = r1.astype(BF16)
    x3 = (r1 - x2.astype(F32)).astype(BF16)
    return x1, x2, x3


def _cumsum_cols(tri, x):
    x1, x2, x3 = _split3(x)
    return _dot(tri, x1) + _dot(tri, x2) + _dot(tri, x3)


def _cumsum_rows(x, tri):
    x1, x2, x3 = _split3(x)
    return _dot_nt(x1, tri) + _dot_nt(x2, tri) + _dot_nt(x3, tri)


def _mod_body(c_ref, w_ref, b_ref, o_ref):
    a = _silu(c_ref[...]).astype(BF16)
    o_ref[...] = _dot(a, w_ref[...].astype(BF16)) + b_ref[...]


def _mod_call(cc, mod_w, mod_b):
    depth, d, n6 = mod_w.shape
    rows = cc.shape[0]
    tn = 1024
    return pl.pallas_call(
        _mod_body,
        grid=(depth, n6 // tn),
        in_specs=[pl.BlockSpec((rows, d), lambda l, j: (0, 0)),
                  pl.BlockSpec((None, d, tn), lambda l, j: (l, 0, j)),
                  pl.BlockSpec((None, 1, tn), lambda l, j: (l, 0, j))],
        out_specs=pl.BlockSpec((None, rows, tn), lambda l, j: (l, 0, j)),
        out_shape=jax.ShapeDtypeStruct((depth, rows, n6), F32),
        compiler_params=_cparams("arbitrary", "arbitrary"),
        name="modulation",
    )(cc, mod_w, mod_b.reshape(depth, 1, n6))


def _rope_block(y, cos, sin, lo):
    partner = jnp.where(lo, pltpu.roll(y, LANES - 16, 1), pltpu.roll(y, 16, 1))
    return y * cos + partner * sin


def _grid_bt(b, t):
    return b, t


def _row_specs(hs, ctx_tiles, t_off, bt=_grid_bt):
    d = hs[0].shape[-1]

    def spec(tile_of):
        def index_map(*g):
            b, t = bt(*g)
            return b, tile_of(t + t_off), 0
        return pl.BlockSpec((None, TM, d), index_map)

    if len(hs) == 1:
        return [spec(lambda t: t)]
    return [spec(lambda t: jnp.minimum(t, ctx_tiles - 1)), spec(lambda t: jnp.maximum(t - ctx_tiles, 0))]


def _load_rows(h_refs, ctx_tiles, t_off, t=None):
    if len(h_refs) == 1:
        return h_refs[0][...]
    t = pl.program_id(1) if t is None else t
    return jnp.where(t + t_off < ctx_tiles, h_refs[0][...], h_refs[1][...])


def _halo_row_specs(hs, ctx_tiles, t_off=0, bt=_grid_bt):
    d = hs[0].shape[-1]
    per = TM // HALO
    offs = (0,) if len(hs) == 1 else (0, ctx_tiles)

    def spec(off, last, first_row_block):
        def index_map(*g):
            b, t = bt(*g)
            return b, jnp.clip(first_row_block(t + t_off - off), 0, last), 0
        return pl.BlockSpec((None, HALO, d), index_map)

    lasts = [h_.shape[1] // HALO - 1 for h_ in hs]
    prevs = [spec(off, last, lambda t: t * per - 1) for off, last in zip(offs, lasts)]
    nexts = [spec(off, last, lambda t: (t + 1) * per) for off, last in zip(offs, lasts)]
    return prevs + nexts


def _inproj_body(*refs, groups, conv_idx, n_src, ctx_tiles, nt, seg_starts):
    mains, prevs, nexts = refs[:n_src], refs[n_src:2 * n_src], refs[2 * n_src:3 * n_src]
    g_ref, sh_ref, sc_ref, cos_ref, sin_ref, w_ref, cw_ref, cb_ref = refs[3 * n_src:3 * n_src + 8]
    out_refs = refs[3 * n_src + 8:-2]
    scr, ubuf = refs[-2:]

    def normmod(h):
        return (_rms(h) * g_ref[...]) * (1.0 + sc_ref[...]) + sh_ref[...]

    ubuf[0:TM, :] = normmod(_load_rows(mains, ctx_tiles, 0)).astype(BF16)
    prev_ok, nxt_ok = _halo_flags(pl.program_id(1), nt, seg_starts)
    up = jnp.where(prev_ok, normmod(_load_rows(prevs, ctx_tiles, 0)), 0.0)
    un = jnp.where(nxt_ok, normmod(_load_rows(nexts, ctx_tiles, 0)), 0.0)
    ubuf[TM:TM + 2 * HALO, :] = jnp.concatenate([up, un], axis=0).astype(BF16)
    cos = cos_ref[...]
    sin = sin_ref[...]
    lane = lax.broadcasted_iota(jnp.int32, cos.shape, 1)
    lo = (lane & 31) < 16
    c_start, c_width = groups[conv_idx][:2]
    c_out = out_refs[conv_idx]
    taps = cw_ref.shape[0]
    for a in range(0, c_width, MXU_COLS):
        proj = _dot(ubuf[...], w_ref[:, c_start + a:c_start + a + MXU_COLS])
        scr[0:HALO, a:a + MXU_COLS] = proj[TM:TM + HALO]
        scr[HALO:HALO + TM, a:a + MXU_COLS] = proj[0:TM]
        scr[HALO + TM:, a:a + MXU_COLS] = proj[TM + HALO:]

    def conv_piece(a):
        cols = slice(a, a + MXU_COLS)
        rows = scr[:, cols]
        nrow = rows.shape[0]

        def tap(j):
            off = j - taps // 2
            shifted = rows if off == 0 else pltpu.roll(rows, (-off) % nrow, 0)
            return shifted[HALO:HALO + TM] * cw_ref[j:j + 1, cols]

        y = tap(0)
        for j in range(1, taps):
            y = y + tap(j)
        c_out[:, cols] = _silu(y + cb_ref[:, cols]).astype(c_out.dtype)

    def project_piece(gi, a, step):
        start, _, rope_cols, qscale = groups[gi]
        y = _dot(ubuf[0:TM, :], w_ref[:, start + a:start + a + step])
        for r in range(0, step, LANES):
            yr = y[:, r:r + LANES]
            if a + r < rope_cols:
                yr = _rope_block(yr, cos, sin, lo)
            if qscale != 1.0:
                yr = yr * qscale
            out_refs[gi][:, a + r:a + r + LANES] = yr.astype(out_refs[gi].dtype)

    conv_pieces = [functools.partial(conv_piece, a) for a in range(0, c_width, MXU_COLS)]
    proj_pieces = []
    for gi, (_, width, rope_cols, _) in enumerate(groups):
        if gi != conv_idx:
            step = min(width, MXU_COLS if rope_cols else 2 * MXU_COLS)
            proj_pieces += [functools.partial(project_piece, gi, a, step) for a in range(0, width, step)]
    for i in range(max(len(conv_pieces), len(proj_pieces))):
        if i < len(proj_pieces):
            proj_pieces[i]()
        if i < len(conv_pieces):
            conv_pieces[i]()


def _seg_map(ctx_tiles, off, bt=_grid_bt):
    def index_map(*g):
        b, t = bt(*g)
        return b, jnp.where(t + off >= ctx_tiles, 1, 0), 0, 0
    return index_map


def _inproj_call(hs, g, shift, scale, cos, sin, w, groups, dtypes, ctx_tiles, conv_idx, conv_w, conv_b):
    bsz, _, d = hs[0].shape
    s = sum(h_.shape[1] for h_ in hs)
    nt = s // TM
    seg = _seg_map(ctx_tiles, 0)
    const = lambda arr: pl.BlockSpec(arr.shape, lambda b, t: (0, 0))
    in_specs = _row_specs(hs, ctx_tiles, 0) + _halo_row_specs(hs, ctx_tiles) + [
                pl.BlockSpec((1, d), lambda b, t: (0, 0)),
                pl.BlockSpec((None, None, 1, d), seg),
                pl.BlockSpec((None, None, 1, d), seg),
                pl.BlockSpec((TM, LANES), lambda b, t: (t, 0)),
                pl.BlockSpec((TM, LANES), lambda b, t: (t, 0)),
                const(w), const(conv_w), const(conv_b)]
    out_specs = [pl.BlockSpec((None, TM, gr[1]), lambda b, t: (b, t, 0)) for gr in groups]
    out_shape = [jax.ShapeDtypeStruct((bsz, s, gr[1]), dt) for gr, dt in zip(groups, dtypes)]
    body = functools.partial(_inproj_body, groups=groups, conv_idx=conv_idx, n_src=len(hs), ctx_tiles=ctx_tiles,
                             nt=nt, seg_starts=(0, ctx_tiles))
    return pl.pallas_call(
        body, grid=(bsz, nt), in_specs=in_specs, out_specs=out_specs, out_shape=out_shape,
        scratch_shapes=[pltpu.VMEM((TM + 2 * HALO, groups[conv_idx][1]), F32),
                        pltpu.VMEM((TM + 2 * HALO, d), BF16)],
        compiler_params=_cparams("parallel", "arbitrary"),
        name="in_proj",
    )(*hs, *hs, *hs, g, shift, scale, cos, sin, w, conv_w, conv_b)


def _halo_flags(t, nt, seg_starts):
    prev_ok = t >= 0
    nxt_ok = (t + 1) < nt
    for s0 in seg_starts:
        prev_ok = jnp.logical_and(prev_ok, t != s0)
        nxt_ok = jnp.logical_and(nxt_ok, (t + 1) != s0)
    return prev_ok, nxt_ok


def _max_half_norm2():
    r_i = lax.broadcasted_iota(jnp.int32, (LANES, LANES), 0)
    c_i = lax.broadcasted_iota(jnp.int32, (LANES, LANES), 1)
    ind = jnp.where(c_i == r_i // HEAD_DIM, 1.0, 0.0).astype(BF16)

    def max_norm2(x):
        xf = x.astype(F32)
        return jnp.max(_dot((xf * xf).astype(BF16), ind), axis=0, keepdims=True)
    return max_norm2


def _win_attn_body(sink_ref, bias_ref, q_ref, kvp_ref, kvc_ref, kvn_ref, kvx_ref, o_ref):
    max_norm2 = _max_half_norm2()
    qn2 = max_norm2(q_ref[:, 0:LANES])
    for g in range(1, A_GROUP):
        qn2 = jnp.maximum(qn2, max_norm2(q_ref[:, g * LANES:(g + 1) * LANES]))
    kn2 = max_norm2(kvc_ref[:, 0:LANES])
    for r in (kvp_ref, kvn_ref, kvx_ref):
        kn2 = jnp.maximum(kn2, max_norm2(r[:, 0:LANES]))
    worst = jnp.max(qn2 * kn2, axis=1, keepdims=True)[0, 0] * NORM_SLACK
    sink_max = jnp.abs(sink_ref[0])
    for i in range(1, A_KV_HEADS * A_GROUP):
        sink_max = jnp.maximum(sink_max, jnp.abs(sink_ref[i]))
    small = jnp.logical_and(worst < SAFE_LOGIT * SAFE_LOGIT, sink_max * LOG2E < SAFE_LOGIT)
    refs = (sink_ref, bias_ref, q_ref, kvp_ref, kvc_ref, kvn_ref, kvx_ref, o_ref)

    @pl.when(small)
    def _():
        _win_attn_heads(*refs, stabilise=False)

    @pl.when(jnp.logical_not(small))
    def _():
        _win_attn_heads(*refs, stabilise=True)


def _win_attn_heads(sink_ref, bias_ref, q_ref, kvp_ref, kvc_ref, kvn_ref, kvx_ref, o_ref, stabilise):
    blk = q_ref.shape[0]
    kvcat = jnp.concatenate([kvp_ref[...], kvc_ref[...], kvn_ref[...], kvx_ref[...]], axis=0)
    kcat = kvcat[:, :LANES]
    vcat = kvcat[:, LANES:]
    nkeys = kvcat.shape[0]
    rows = A_GROUP * blk
    bias = bias_ref[...]
    lane = lax.broadcasted_iota(jnp.int32, (blk, LANES), 1)
    rowg = lax.broadcasted_iota(jnp.int32, (rows, 1), 0) // blk
    q = q_ref[...]

    def logits(h):
        half = (lane >= h * HEAD_DIM) & (lane < (h + 1) * HEAD_DIM)
        qh = jnp.concatenate(
            [jnp.where(half, q[:, g * LANES:(g + 1) * LANES], jnp.zeros((), BF16)) for g in range(A_GROUP)], axis=0)
        return _dot_nt(qh, kcat)

    raw = [logits(h) for h in range(A_KV_HEADS)]
    outs = []
    for h in range(A_KV_HEADS):
        s = (raw[h].reshape(A_GROUP, blk, nkeys) + bias[None]).reshape(rows, nkeys)
        sink = jnp.zeros((rows, 1), F32)
        for g in range(A_GROUP):
            sink = jnp.where(rowg == g, sink_ref[h * A_GROUP + g] * LOG2E, sink)
        if stabilise:
            m = jnp.maximum(jnp.max(s, axis=1, keepdims=True), sink)
            p = jnp.exp2(s - m)
            den = jnp.sum(p, axis=1, keepdims=True) + jnp.exp2(sink - m)
        else:
            p = jnp.exp2(s)
            den = jnp.sum(p, axis=1, keepdims=True) + jnp.exp2(sink)
        outs.append(_dot(p.astype(BF16), vcat) / den)
    for g in range(A_GROUP):
        o_ref[:, g * LANES:(g + 1) * LANES] = jnp.where(
            lane < HEAD_DIM, outs[0][g * blk:(g + 1) * blk], outs[1][g * blk:(g + 1) * blk]).astype(o_ref.dtype)


def _win_attn_bias(ctx_len):
    qi = jnp.arange(WQ)[:, None]
    kj = jnp.arange(WQ + 2 * A_WINDOW)[None, :]
    band = jnp.abs(kj - A_WINDOW - qi) <= A_WINDOW
    variants = [jnp.zeros_like(band), band & (kj >= A_WINDOW), band, band & (kj < A_WINDOW + WQ)]
    win = jnp.stack([jnp.where(v_, 0.0, NEG) for v_ in variants]).astype(F32)
    return jnp.concatenate([win, jnp.zeros((4, WQ, ctx_len), F32)], axis=2)


def _win_attn_call(q, kv, sink, ctx_len):
    bsz, s, qw = q.shape
    kw = kv.shape[-1]
    nq = s // WQ
    cq = ctx_len // WQ
    per = WQ // A_WINDOW
    nblk = s // A_WINDOW
    assert ctx_len % WQ == 0 and nq - cq >= 2
    bias = _win_attn_bias(ctx_len)
    clamp = lambda j: jnp.clip(j, cq * per, nblk - 1)
    variant = lambda i: jnp.where(i < cq, 0, jnp.where(i == cq, 1, jnp.where(i == nq - 1, 3, 2)))
    kv_specs = [pl.BlockSpec((None, A_WINDOW, kw), lambda b, i: (b, clamp(i * per - 1), 0)),
                pl.BlockSpec((None, WQ, kw), lambda b, i: (b, i, 0)),
                pl.BlockSpec((None, A_WINDOW, kw), lambda b, i: (b, clamp((i + 1) * per), 0)),
                pl.BlockSpec((None, ctx_len, kw), lambda b, i: (b, 0, 0))]
    return pl.pallas_call(
        _win_attn_body,
        grid=(bsz, nq),
        in_specs=[pl.BlockSpec(memory_space=pltpu.SMEM),
                  pl.BlockSpec((None,) + bias.shape[1:], lambda b, i: (variant(i), 0, 0)),
                  pl.BlockSpec((None, WQ, qw), lambda b, i: (b, i, 0))] + kv_specs,
        out_specs=pl.BlockSpec((None, WQ, qw), lambda b, i: (b, i, 0)),
        out_shape=jax.ShapeDtypeStruct((bsz, s, qw), BF16),
        compiler_params=_cparams("parallel", "arbitrary"),
        name="window_attention",
    )(sink, bias, q, kv, kv, kv, kv)


def _bwd_chunk(j, ctx_chunks, n_chunks):
    return jnp.where(j < ctx_chunks, ctx_chunks - 1 - j, n_chunks + ctx_chunks - 1 - j)


def _scan_rows(dirs):
    return [(bb, d, tuple(r.at[bb] for r in refs)) for bb in range(SCAN_ROWS) for d, refs in enumerate(dirs)]


def _tri_masks(d):
    row = lax.broadcasted_iota(jnp.int32, (CHUNK, CHUNK), 0)
    col = lax.broadcasted_iota(jnp.int32, (CHUNK, CHUNK), 1)
    keep = (col <= row) if d == 0 else (col >= row)
    return keep, jnp.where(keep, 1.0, 0.0).astype(BF16)


def _mlstm_body(qkf_ref, vf_ref, gf_ref, qkb_ref, vb_ref, gb_ref, bias_ref, hf_ref, hb_ref, c_scr, m_scr):
    L = CHUNK
    dk = HEAD_DIM
    dv = LANES

    @pl.when(pl.program_id(1) == 0)
    def _():
        c_scr[...] = jnp.zeros_like(c_scr)
        m_scr[...] = jnp.zeros_like(m_scr)

    lane = lax.broadcasted_iota(jnp.int32, (L, LANES), 1)
    srow = lax.broadcasted_iota(jnp.int32, (LANES, 2 * dv), 0)
    ones_v = jnp.ones((L, dv), BF16)
    dirs = ((qkf_ref, vf_ref, gf_ref, hf_ref), (qkb_ref, vb_ref, gb_ref, hb_ref))
    for bb, d, (qk_ref, v_ref, g_ref, o_ref) in _scan_rows(dirs):
        keep, tri = _tri_masks(d)
        gates = g_ref[...] + bias_ref[...]
        gates_t = gates.T[0:16, :]
        b_col = _cumsum_cols(tri, _log_sigmoid(gates))
        b_row = _cumsum_rows(_log_sigmoid(gates_t), tri)
        k_pairs = [qk_ref[:, B_HEADS * dk + p * LANES:B_HEADS * dk + (p + 1) * LANES] for p in range(B_HEADS // 2)]
        kt_pairs = [kp.astype(F32).T.astype(BF16) for kp in k_pairs]
        for h in range(B_HEADS):
            ci = d * B_HEADS + h
            cf = 2 * B_HEADS + ci
            r = bb * 2 * B_HEADS + ci
            li_c = jnp.broadcast_to(gates[:, ci:ci + 1], (L, LANES))
            li_r = gates_t[ci:ci + 1, :]
            bc = jnp.broadcast_to(b_col[:, cf:cf + 1], (L, LANES))
            br = b_row[cf:cf + 1, :]
            b_last = bc[L - 1:L, :] if d == 0 else bc[0:1, :]
            m_prev = m_scr[r:r + 1, :]
            pair = (h // 2) * LANES
            half = (lane >= (h % 2) * dk) & (lane < (h % 2 + 1) * dk)
            q = jnp.where(half, qk_ref[:, pair:pair + LANES] * (dk ** -0.5), 0.0).astype(BF16)
            k = k_pairs[h // 2].astype(BF16)
            v = v_ref[:, h * dv:(h + 1) * dv]
            dm = jnp.where(keep, bc - br + li_r, NEG)
            g_in = bc + m_prev
            m_t = jnp.maximum(jnp.broadcast_to(jnp.max(dm, axis=1, keepdims=True), (L, LANES)), g_in)
            sm = (_dot_nt(q, k) * jnp.exp(dm - m_t)).astype(BF16)
            vaug = jnp.concatenate([v, ones_v], axis=1)
            c_prev = c_scr[r]
            intra = _dot(sm, vaug)
            inter = _dot(q, c_prev.astype(BF16))
            a_in = jnp.exp(g_in - m_t)
            den = jnp.maximum(jnp.abs(intra[:, dv:] + a_in * inter[:, dv:]), jnp.exp(-m_t))
            o_ref[:, h * dv:(h + 1) * dv] = ((intra[:, :dv] + a_in * inter[:, :dv]) / den).astype(o_ref.dtype)
            w_c = b_last - bc + li_c
            m_loc = jnp.max(w_c, axis=0, keepdims=True)
            e = jnp.exp(w_c - m_loc)
            ev = jnp.concatenate([e * v.astype(F32), e], axis=1).astype(BF16)
            c_loc = _dot(kt_pairs[h // 2], ev)
            own = (srow >= (h % 2) * dk) & (srow < (h % 2 + 1) * dk)
            m_new = jnp.maximum(b_last + m_prev, m_loc)
            keep_old = jnp.exp(b_last + m_prev - m_new)
            take_new = jnp.exp(m_loc - m_new)
            c_scr[r] = (jnp.concatenate([keep_old, keep_old], axis=1) * c_prev
                        + jnp.where(own, jnp.concatenate([take_new, take_new], axis=1) * c_loc, 0.0))
            m_scr[r:r + 1, :] = m_new


def _scan_specs(width, ctx_chunks, n_chunks):
    fwd = pl.BlockSpec((SCAN_ROWS, CHUNK, width), lambda b, j: (b, j, 0))
    bwd = pl.BlockSpec((SCAN_ROWS, CHUNK, width), lambda b, j: (b, _bwd_chunk(j, ctx_chunks, n_chunks), 0))
    return fwd, bwd


def _mlstm_call(qk, v, gates, bias, ctx_len):
    bsz, s, w = v.shape
    nc = s // CHUNK
    cc = ctx_len // CHUNK
    qf, qb = _scan_specs(qk.shape[-1], cc, nc)
    vf, vb = _scan_specs(w, cc, nc)
    gf, gb = _scan_specs(LANES, cc, nc)
    return pl.pallas_call(
        _mlstm_body,
        grid=(bsz // SCAN_ROWS, nc),
        in_specs=[qf, vf, gf, qb, vb, gb, pl.BlockSpec((1, LANES), lambda b, j: (0, 0))],
        out_specs=[vf, vb],
        out_shape=[jax.ShapeDtypeStruct((bsz, s, w), BF16)] * 2,
        scratch_shapes=[pltpu.VMEM((SCAN_ROWS * 2 * B_HEADS, LANES, 2 * LANES), F32),
                        pltpu.VMEM((SCAN_ROWS * 2 * B_HEADS, LANES), F32)],
        compiler_params=_cparams("parallel", "arbitrary"),
        name="mlstm_scan",
    )(qk, v, gates, qk, v, gates, bias)


def _ssd_body(xf_ref, dtf_ref, xb_ref, dtb_ref, dtbias_ref, nega_ref, skip_ref, yf_ref, yb_ref, s_scr):
    L = CHUNK
    hd = HEAD_DIM
    inner = 4 * D_GROUPS * hd
    hpg = 4

    @pl.when(pl.program_id(1) == 0)
    def _():
        s_scr[...] = jnp.zeros_like(s_scr)

    lane = lax.broadcasted_iota(jnp.int32, (L, LANES), 1)
    lo = lane < hd
    lo1 = lo[0:1, :]
    dirs = ((xf_ref, dtf_ref, yf_ref), (xb_ref, dtb_ref, yb_ref))
    for bb, d, (x_ref, dt_ref, y_ref) in _scan_rows(dirs):
        keep, tri = _tri_masks(d)
        dt = _softplus(dt_ref[...] + dtbias_ref[...])
        a = dt * nega_ref[...]
        dt_t = dt.T[0:16, :]
        a_t = a.T[0:16, :]
        ac_col = _cumsum_cols(tri, a)
        ac_row = _cumsum_rows(a_t, tri)
        for g in range(D_GROUPS):
            bg_f = x_ref[:, inner + g * LANES:inner + (g + 1) * LANES]
            bg = bg_f.astype(BF16)
            bg_t = bg_f.T.astype(BF16)
            cg = x_ref[:, inner + (D_GROUPS + g) * LANES:inner + (D_GROUPS + g + 1) * LANES].astype(BF16)
            cb = _dot_nt(cg, bg)
            for pr in range(2):
                c0 = g * hpg * hd + pr * LANES
                xp = x_ref[:, c0:c0 + LANES]
                xpb = xp.astype(BF16)
                ys, eas, wsts, als = [], [], [], []
                for hh in range(2):
                    col = d * D_GROUPS * hpg + g * hpg + pr * 2 + hh
                    acc = jnp.broadcast_to(ac_col[:, col:col + 1], (L, LANES))
                    acr = ac_row[col:col + 1, :]
                    seg = jnp.where(keep, acc - acr, NEG)
                    mix = (cb * jnp.exp(seg) * dt_t[col:col + 1, :]).astype(BF16)
                    ys.append(_dot(mix, xpb))
                    a_last = acc[L - 1:L, :] if d == 0 else acc[0:1, :]
                    eas.append(jnp.exp(acc))
                    wsts.append(jnp.exp(a_last - acc) * jnp.broadcast_to(dt[:, col:col + 1], (L, LANES)))
                    als.append(jnp.exp(a_last))
                sidx = (bb * 2 + d) * 2 * D_GROUPS + g * 2 + pr
                st = s_scr[sidx]
                y = jnp.where(lo, ys[0], ys[1]) + _dot(cg, st.astype(BF16)) * jnp.where(lo, eas[0], eas[1])
                if d == 0:
                    y = y + skip_ref[:, c0:c0 + LANES] * xp
                y_ref[:, c0:c0 + LANES] = y.astype(y_ref.dtype)
                xw = (xp * jnp.where(lo, wsts[0], wsts[1])).astype(BF16)
                s_scr[sidx] = jnp.where(lo1, als[0], als[1]) * st + _dot(bg_t, xw)


def _ssd_call(xbc, dt, dt_bias, neg_a, skip, ctx_len):
    bsz, s, w = xbc.shape
    nc = s // CHUNK
    cc = ctx_len // CHUNK
    inner = skip.shape[-1]
    xf, xb = _scan_specs(w, cc, nc)
    df, db = _scan_specs(LANES, cc, nc)
    yf, yb = _scan_specs(inner, cc, nc)
    vec = lambda n: pl.BlockSpec((1, n), lambda b, j: (0, 0))
    return pl.pallas_call(
        _ssd_body,
        grid=(bsz // SCAN_ROWS, nc),
        in_specs=[xf, df, xb, db, vec(LANES), vec(LANES), vec(inner)],
        out_specs=[yf, yb],
        out_shape=[jax.ShapeDtypeStruct((bsz, s, inner), BF16)] * 2,
        scratch_shapes=[pltpu.VMEM((SCAN_ROWS * 4 * D_GROUPS, LANES, LANES), F32)],
        compiler_params=_cparams("parallel", "arbitrary"),
        name="ssd_scan",
    )(xbc, dt, xbc, dt, dt_bias, neg_a, skip)


def _diff_attn_body(lam_ref, q_ref, k_ref, v_ref, o_ref, kn_scr, *, lam_init):
    nheads = q_ref.shape[1] // LANES
    max_norm2 = _max_half_norm2()

    @pl.when(pl.program_id(1) == 0)
    def _():
        for h in range(nheads):
            kn_scr[h:h + 1, :] = max_norm2(k_ref[:, h * LANES:(h + 1) * LANES])

    bound2 = max_norm2(q_ref[:, 0:LANES]) * kn_scr[0:1, :]
    for h in range(1, nheads):
        bound2 = jnp.maximum(bound2, max_norm2(q_ref[:, h * LANES:(h + 1) * LANES]) * kn_scr[h:h + 1, :])
    worst = jnp.max(bound2, axis=1, keepdims=True)[0, 0] * NORM_SLACK
    small = worst < SAFE_LOGIT * SAFE_LOGIT

    @pl.when(small)
    def _():
        _diff_attn_heads(lam_ref, q_ref, k_ref, v_ref, o_ref, lam_init, stabilise=False)

    @pl.when(jnp.logical_not(small))
    def _():
        _diff_attn_heads(lam_ref, q_ref, k_ref, v_ref, o_ref, lam_init, stabilise=True)


def _diff_attn_heads(lam_ref, q_ref, k_ref, v_ref, o_ref, lam_init, stabilise):
    lv = lam_ref[...]
    lam = (jnp.exp(jnp.sum(lv[0:1] * lv[1:2], axis=1, keepdims=True))
           - jnp.exp(jnp.sum(lv[2:3] * lv[3:4], axis=1, keepdims=True)) + lam_init)
    tq = q_ref.shape[0]
    lane = lax.broadcasted_iota(jnp.int32, (tq, LANES), 1)
    nheads = q_ref.shape[1] // LANES

    def logits(h):
        q = q_ref[:, h * LANES:(h + 1) * LANES]
        k = k_ref[:, h * LANES:(h + 1) * LANES]
        return [_dot_nt(jnp.where((lane >= m * HEAD_DIM) & (lane < (m + 1) * HEAD_DIM), q, jnp.zeros((), BF16)), k)
                for m in range(2)]

    s_next = logits(0)
    for h in range(nheads):
        s_cur = s_next
        if h + 1 < nheads:
            s_next = logits(h + 1)
        ps, ls = [], []
        for s in s_cur:
            p = jnp.exp2(s - jnp.max(s, axis=1, keepdims=True)) if stabilise else jnp.exp2(s)
            ps.append(p)
            ls.append(jnp.sum(p, axis=1, keepdims=True))
        a = (ps[0] - (lam * ls[0] / ls[1]) * ps[1]).astype(BF16)
        o_ref[:, h * LANES:(h + 1) * LANES] = _dot(a, v_ref[:, h * LANES:(h + 1) * LANES]) / ls[0]


def _diff_attn_call(q, k, v, lam_vecs, lam_init, ctx_len):
    bsz, s, w = q.shape
    t = s - ctx_len
    off = ctx_len // TQ
    return pl.pallas_call(
        functools.partial(_diff_attn_body, lam_init=lam_init),
        grid=(bsz, t // TQ),
        in_specs=[pl.BlockSpec(lam_vecs.shape, lambda b, i: (0, 0)),
                  pl.BlockSpec((None, TQ, w), lambda b, i: (b, i + off, 0)),
                  pl.BlockSpec((None, s, w), lambda b, i: (b, 0, 0)),
                  pl.BlockSpec((None, s, w), lambda b, i: (b, 0, 0))],
        out_specs=pl.BlockSpec((None, TQ, w), lambda b, i: (b, i, 0)),
        out_shape=jax.ShapeDtypeStruct((bsz, t, w), F32),
        scratch_shapes=[pltpu.VMEM((HALO, LANES), F32)],
        compiler_params=_cparams("parallel", "arbitrary"),
        name="diff_attention",
    )(lam_vecs, q, k, v)


def _group_rms(x, width):
    return jnp.concatenate([_rms(x[:, a:a + width]) for a in range(0, x.shape[1], width)], axis=1)


def _outproj_even_body(ya_ref, hf_ref, hb_ref, og_ref, ng_ref, w_ref, g_ref, out_ref):
    half = ya_ref.shape[1]
    f32 = lambda r: r[...].astype(F32)
    hn = _group_rms(f32(hf_ref) + f32(hb_ref), LANES) * ng_ref[...] * _sigmoid(f32(og_ref))
    y = _dot(ya_ref[...], w_ref[0:half, :]) + _dot(hn.astype(BF16), w_ref[half:, :])
    out_ref[...] = (_rms(y) * g_ref[...]).astype(out_ref.dtype)


def _outproj_odd_body(oa_ref, yf_ref, yb_ref, z_ref, cg_ref, dg_ref, w_ref, g_ref, out_ref, *, out_scale):
    half = oa_ref.shape[1]
    on = _group_rms(oa_ref[...], LANES) * cg_ref[...] * out_scale
    f32 = lambda r: r[...].astype(F32)
    yz = (f32(yf_ref) + f32(yb_ref)) * _silu(f32(z_ref))
    sn = _group_rms(yz, half // D_GROUPS) * dg_ref[...]
    y = _dot(on.astype(BF16), w_ref[0:half, :]) + _dot(sn.astype(BF16), w_ref[half:, :])
    out_ref[...] = (_rms(y) * g_ref[...]).astype(out_ref.dtype)


def _outproj_call(body, acts, act_offs, vecs, w, g, n_t):
    bsz = acts[0].shape[0]
    d = w.shape[1]

    def act_spec(arr, off):
        return pl.BlockSpec((None, TM, arr.shape[-1]), lambda b, t: (b, t + off, 0))

    in_specs = ([act_spec(a_, o_) for a_, o_ in zip(acts, act_offs)]
                + [pl.BlockSpec(v_.shape, lambda b, t: (0, 0)) for v_ in vecs]
                + [pl.BlockSpec(w.shape, lambda b, t: (0, 0)),
                   pl.BlockSpec((1, d), lambda b, t: (0, 0))])
    return pl.pallas_call(
        body, grid=(bsz, n_t), in_specs=in_specs,
        out_specs=pl.BlockSpec((None, TM, d), lambda b, t: (b, t, 0)),
        out_shape=jax.ShapeDtypeStruct((bsz, n_t * TM, d), BF16),
        compiler_params=_cparams("parallel", "arbitrary"),
        name="out_proj",
    )(*acts, *vecs, w, g)


def _ffn_body(*refs, n_src, ctx_tiles, t_off, nt, seg_starts):
    mains, prevs, nexts = refs[:n_src], refs[n_src:2 * n_src], refs[2 * n_src:3 * n_src]
    (ym_ref, yp_ref, yn_ref, gate1_ref, g2_ref, sh_ref, sc_ref, wg_ref, wu_ref, cw_ref, wd_ref, g3_ref, gate_ref,
     out_ref, gscr, uscr) = refs[3 * n_src:]
    t = pl.program_id(1)
    prev_ok, nxt_ok = _halo_flags(t, nt, seg_starts)

    def resid(h_refs, y):
        return _load_rows(h_refs, ctx_tiles, t_off) + gate1_ref[...] * y

    def normmod(h):
        return (_rms(h) * g2_ref[...]) * (1.0 + sc_ref[...]) + sh_ref[...]

    hm = resid(mains, ym_ref[...].astype(F32))
    um = normmod(hm)
    up = jnp.where(prev_ok, normmod(resid(prevs, yp_ref[...].astype(F32)[Y_HALO - HALO:, :])), 0.0)
    un = jnp.where(nxt_ok, normmod(resid(nexts, yn_ref[...].astype(F32)[:HALO, :])), 0.0)
    ucat = jnp.concatenate([up, um, un], axis=0).astype(BF16)
    umb = um.astype(BF16)

    nchunk = wg_ref.shape[1] // FFN_CHUNK
    cols = lambda c: slice(c * FFN_CHUNK, (c + 1) * FFN_CHUNK)

    def front(c):
        gscr[c % 2] = _dot(ucat, wg_ref[:, cols(c)])
        uscr[c % 2] = _dot(umb, wu_ref[:, cols(c)])

    front(0)
    y = None
    for c in range(nchunk):
        if c + 1 < nchunk:
            front(c + 1)
        gbuf = gscr.at[c % 2]
        gc = (gbuf[pl.ds(HALO - 1, TM), :] * cw_ref[0:1, cols(c)] + gbuf[pl.ds(HALO, TM), :] * cw_ref[1:2, cols(c)]
              + gbuf[pl.ds(HALO + 1, TM), :] * cw_ref[2:3, cols(c)])
        hid = (_silu(gc) * uscr[c % 2]).astype(BF16)
        part = _dot(hid, wd_ref[cols(c), :])
        y = part if y is None else y + part
    out_ref[...] = hm + gate_ref[...] * (_rms(y) * g3_ref[...])


def _ffn_call(hs, yn, gate1, g2, shift, scale, wg, wu, cw, wd, g3, gate2, ctx_tiles, t_off, seg_starts):
    bsz, rows, d = yn.shape
    n_t = rows // TM
    per = TM // Y_HALO
    last = rows // Y_HALO - 1
    const = lambda arr: pl.BlockSpec(arr.shape, lambda b, t: (0, 0))
    vec = pl.BlockSpec((1, d), lambda b, t: (0, 0))
    mod = pl.BlockSpec((None, None, 1, d), _seg_map(ctx_tiles, t_off))
    y_specs = [pl.BlockSpec((None, TM, d), lambda b, t: (b, t, 0)),
               pl.BlockSpec((None, Y_HALO, d), lambda b, t: (b, jnp.maximum(t * per - 1, 0), 0)),
               pl.BlockSpec((None, Y_HALO, d), lambda b, t: (b, jnp.minimum((t + 1) * per, last), 0))]
    body = functools.partial(_ffn_body, n_src=len(hs), ctx_tiles=ctx_tiles, t_off=t_off, nt=n_t, seg_starts=seg_starts)
    return pl.pallas_call(
        body, grid=(bsz, n_t),
        in_specs=(_row_specs(hs, ctx_tiles, t_off) + _halo_row_specs(hs, ctx_tiles, t_off) + y_specs
                  + [mod, vec, mod, mod, const(wg), const(wu), const(cw), const(wd), vec, mod]),
        out_specs=pl.BlockSpec((None, TM, d), lambda b, t: (b, t, 0)),
        out_shape=jax.ShapeDtypeStruct((bsz, rows, d), F32),
        scratch_shapes=[pltpu.VMEM((2, TM + 2 * HALO, FFN_CHUNK), F32), pltpu.VMEM((2, TM, FFN_CHUNK), F32)],
        compiler_params=_cparams("parallel", "arbitrary"),
        name="conv_ffn",
    )(*hs, *hs, *hs, yn, yn, yn, gate1, g2, shift, scale, wg, wu, cw, wd, g3, gate2)


def _rope_tables(ctx_len, t):
    pos = jnp.arange(t)
    row = (pos // GRID_W).astype(F32)
    col = (pos % GRID_W).astype(F32)
    nf = HEAD_DIM // 4
    inv = ROPE_BASE ** (-jnp.arange(nf, dtype=F32) / nf)
    ar = row[:, None] * inv
    ac = col[:, None] * inv
    cos = jnp.concatenate([jnp.cos(ar), jnp.cos(ar), jnp.cos(ac), jnp.cos(ac)], axis=1)
    sin = jnp.concatenate([-jnp.sin(ar), jnp.sin(ar), -jnp.sin(ac), jnp.sin(ac)], axis=1)
    cos = jnp.concatenate([jnp.ones((ctx_len, HEAD_DIM), F32), cos], axis=0)
    sin = jnp.concatenate([jnp.zeros((ctx_len, HEAD_DIM), F32), sin], axis=0)
    return jnp.tile(cos, (1, LANES // HEAD_DIM)), jnp.tile(sin, (1, LANES // HEAD_DIM))


def _pad_cols(w, n):
    return jnp.pad(w, ((0, 0), (0, n - w.shape[1])))


def _gqa_perm():
    idx = [(h * A_GROUP + g) * HEAD_DIM + dd
           for g in range(A_GROUP) for h in range(A_KV_HEADS) for dd in range(HEAD_DIM)]
    return jnp.array(idx, dtype=jnp.int32)


def _ffn_weights(w_gate, w_up, conv, w_down):
    return w_gate.astype(BF16), w_up.astype(BF16), conv, w_down.astype(BF16)


def kernel(x, c, ctx, c_ctx, mod_w, mod_b, norm_g, ffn_w_gate, ffn_w_up, ffn_conv, ffn_w_down, ev_w_in, ev_w_out, a_sink, b_conv, b_gate_b, b_norm_g, od_w_in, od_w_out, c_lambda, c_norm_g, d_conv, d_conv_b, d_dt_bias, d_a_log, d_skip, d_norm_g):
    bsz, t, d = x.shape
    ctx_len = ctx.shape[1]
    depth = mod_w.shape[0]
    s = ctx_len + t
    half = d // 2
    assert ctx_len % TM == 0 and t % TM == 0 and t % GRID_W == 0 and ffn_w_gate.shape[-1] % FFN_CHUNK == 0
    assert bsz % SCAN_ROWS == 0
    n_t = s // TM
    ctx_tiles = ctx_len // TM

    mod_rows = -(-(bsz + 1) // HALO) * HALO
    cc = jnp.concatenate([c, c_ctx[None, :], jnp.zeros((mod_rows - bsz - 1, d), F32)], axis=0)
    mods = _mod_call(cc, mod_w, mod_b).reshape(depth, mod_rows, 6, d)

    def layer_mods(l):
        lat = mods[l, :bsz]
        cx = jnp.broadcast_to(mods[l, bsz][None], (bsz, 6, d))
        m = jnp.stack([cx, lat], axis=1)
        return [m[:, :, i][:, :, None, :] for i in range(6)]

    cos, sin = _rope_tables(ctx_len, t)
    hs = (ctx, x)

    for l in range(depth):
        last = l == depth - 1
        j = l // 2
        m = layer_mods(l)
        g = norm_g[l]
        wg, wu, cw, wd = _ffn_weights(ffn_w_gate[l], ffn_w_up[l], ffn_conv[l], ffn_w_down[l])
        if l % 2 == 0:
            w_in = ev_w_in[j]
            perm = _gqa_perm()
            akv = A_KV_HEADS * HEAD_DIM
            o1 = half + 2 * akv
            w_cat = jnp.concatenate([w_in[:, :half][:, perm], w_in[:, half:o1 + 3 * half],
                                     _pad_cols(w_in[:, o1 + 3 * half:], LANES)], axis=1).astype(BF16)
            groups = ((0, half, half, HEAD_DIM ** -0.5 * LOG2E), (half, 2 * akv, akv, 1.0),
                      (o1, half, 0, 1.0), (o1 + half, half, 0, 1.0), (o1 + 2 * half, half, 0, 1.0),
                      (o1 + 3 * half, LANES, 0, 1.0))
            q, kv, qkc, mv, mo, gates = _inproj_call(
                hs, g[0:1], m[0], m[1], cos, sin, w_cat, groups, (BF16, BF16, BF16, BF16, BF16, F32), ctx_tiles,
                2, b_conv[j], jnp.zeros((1, half), F32))
            ya = _win_attn_call(q, kv, a_sink[j], ctx_len)
            hf, hb = _mlstm_call(qkc, mv, gates, _pad_cols(b_gate_b[j][None, :], LANES), ctx_len)
            w_out = ev_w_out[j]
            w_out = jnp.concatenate([w_out[:half][perm], w_out[half:]], axis=0).astype(BF16)
            acts, vecs, body = (ya, hf, hb, mo), (b_norm_g[j][None, :],), _outproj_even_body
        else:
            w_in = od_w_in[j]
            xbc_w = 2 * half
            w_cat = jnp.concatenate([w_in[:, :4 * half + xbc_w],
                                     _pad_cols(w_in[:, 4 * half + xbc_w:], LANES)], axis=1).astype(BF16)
            groups = ((0, half, half, HEAD_DIM ** -0.5 * LOG2E), (half, half, half, 1.0), (2 * half, half, 0, 1.0),
                      (3 * half, half, 0, 1.0), (4 * half, xbc_w, 0, 1.0), (4 * half + xbc_w, LANES, 0, 1.0))
            q, k, v, z, xbcc, dt = _inproj_call(
                hs, g[0:1], m[0], m[1], cos, sin, w_cat, groups, (BF16, BF16, BF16, BF16, F32, F32), ctx_tiles,
                4, d_conv[j], d_conv_b[j][None, :])
            lam_init = 0.8 - 0.6 * math.exp(-LAM_DEPTH_RATE * l)
            oa = _diff_attn_call(q, k, v, c_lambda[j], lam_init, ctx_len)
            dt_bias = _pad_cols(d_dt_bias[j].reshape(1, -1), LANES)
            neg_a = _pad_cols(-jnp.exp(d_a_log[j].astype(F32)).reshape(1, -1), LANES)
            skip = jnp.repeat(d_skip[j].astype(F32), HEAD_DIM)[None, :]
            yf, yb = _ssd_call(xbcc, dt, dt_bias, neg_a, skip, ctx_len)
            w_out = od_w_out[j].astype(BF16)
            acts, vecs = (oa, yf, yb, z), (c_norm_g[j][None, :], d_norm_g[j][None, :])
            body = functools.partial(_outproj_odd_body, out_scale=1.0 - lam_init)
        t_off = ctx_tiles if last else 0
        if l % 2 == 0:
            act_offs = (t_off,) * 4
        else:
            assert last, "differential attention is only computed for latent queries"
            act_offs = (0,) + (t_off,) * 3
        yn = _outproj_call(body, acts, act_offs, vecs, w_out, g[1:2], n_t - t_off)
        seg_starts = (0,) if last else (0, ctx_tiles)
        hs = (_ffn_call(hs, yn, m[2], g[2:3], m[3], m[4], wg, wu, cw, wd, g[3:4], m[5], ctx_tiles, t_off,
                        seg_starts),)
    return hs[0]
```

```python
import functools
import math

import jax
import jax.numpy as jnp
from jax import lax
from jax.experimental import pallas as pl
from jax.experimental.pallas import tpu as pltpu

F32 = jnp.float32
BF16 = jnp.bfloat16

EPS = 1e-6
ROPE_BASE = 10000.0
GRID_W = 64
HEAD_DIM = 64
A_KV_HEADS = 2
A_GROUP = 4
A_WINDOW = 128
B_HEADS = 4
D_GROUPS = 2
LAM_DEPTH_RATE = 0.3

LANES = 128
MXU_COLS = 256
HALO = 8
Y_HALO = 16
TM = 256
CHUNK = 128
SCAN_ROWS = 4
FFN_CHUNK = 256
TQ = 256
WQ = 256
NEG = -1e30
SAFE_LOGIT = 40.0
NORM_SLACK = 1.05
LOG2E = math.log2(math.e)
VMEM_LIMIT = 56 * 1024 * 1024


def _cparams(*sem):
    return pltpu.CompilerParams(dimension_semantics=sem, vmem_limit_bytes=VMEM_LIMIT)


def _sigmoid(x):
    return 0.5 * jnp.tanh(0.5 * x) + 0.5


def _silu(x):
    return x * _sigmoid(x)


def _softplus(x):
    return jnp.maximum(x, 0.0) + jnp.log(1.0 + jnp.exp(-jnp.abs(x)))


def _log_sigmoid(x):
    return -_softplus(-x)


def _rms(x):
    return x * lax.rsqrt(jnp.mean(x * x, axis=-1, keepdims=True) + EPS)


def _dot(a, b):
    return jnp.dot(a, b, preferred_element_type=F32)


def _dot_nt(a, b):
    return lax.dot_general(a, b, (((1,), (1,)), ((), ())), preferred_element_type=F32)


def _dot_tn(a, b):
    return lax.dot_general(a, b, (((0,), (0,)), ((), ())), preferred_element_type=F32)


def _split3(x):
    x1 = x.astype(BF16)
    r1 = x - x1.astype(F32)
    x2 = r1.astype(BF16)
    x3 = (r1 - x2.astype(F32)).astype(BF16)
    return x1, x2, x3


def _cumsum_cols(tri, x):
    x1, x2, x3 = _split3(x)
    return _dot(tri, x1) + _dot(tri, x2) + _dot(tri, x3)


def _cumsum_rows(x, tri):
    x1, x2, x3 = _split3(x)
    return _dot_nt(x1, tri) + _dot_nt(x2, tri) + _dot_nt(x3, tri)


def _mod_body(c_ref, w_ref, b_ref, o_ref):
    a = _silu(c_ref[...]).astype(BF16)
    o_ref[...] = _dot(a, w_ref[...].astype(BF16)) + b_ref[...]


def _mod_call(cc, mod_w, mod_b):
    depth, d, n6 = mod_w.shape
    rows = cc.shape[0]
    tn = 1024
    return pl.pallas_call(
        _mod_body,
        grid=(depth, n6 // tn),
        in_specs=[pl.BlockSpec((rows, d), lambda l, j: (0, 0)),
                  pl.BlockSpec((None, d, tn), lambda l, j: (l, 0, j)),
                  pl.BlockSpec((None, 1, tn), lambda l, j: (l, 0, j))],
        out_specs=pl.BlockSpec((None, rows, tn), lambda l, j: (l, 0, j)),
        out_shape=jax.ShapeDtypeStruct((depth, rows, n6), F32),
        compiler_params=_cparams("arbitrary", "arbitrary"),
        name="modulation",
    )(cc, mod_w, mod_b.reshape(depth, 1, n6))


def _rope_block(y, cos, sin, lo):
    partner = jnp.where(lo, pltpu.roll(y, LANES - 16, 1), pltpu.roll(y, 16, 1))
    return y * cos + partner * sin


def _grid_bt(b, t):
    return b, t


def _row_specs(hs, ctx_tiles, t_off, bt=_grid_bt):
    d = hs[0].shape[-1]

    def spec(tile_of):
        def index_map(*g):
            b, t = bt(*g)
            return b, tile_of(t + t_off), 0
        return pl.BlockSpec((None, TM, d), index_map)

    if len(hs) == 1:
        return [spec(lambda t: t)]
    return [spec(lambda t: jnp.minimum(t, ctx_tiles - 1)), spec(lambda t: jnp.maximum(t - ctx_tiles, 0))]


def _load_rows(h_refs, ctx_tiles, t_off, t=None):
    if len(h_refs) == 1:
        return h_refs[0][...]
    t = pl.program_id(1) if t is None else t
    return jnp.where(t + t_off < ctx_tiles, h_refs[0][...], h_refs[1][...])


def _halo_row_specs(hs, ctx_tiles, t_off=0, bt=_grid_bt):
    d = hs[0].shape[-1]
    per = TM // HALO
    offs = (0,) if len(hs) == 1 else (0, ctx_tiles)

    def spec(off, last, first_row_block):
        def index_map(*g):
            b, t = bt(*g)
            return b, jnp.clip(first_row_block(t + t_off - off), 0, last), 0
        return pl.BlockSpec((None, HALO, d), index_map)

    lasts = [h_.shape[1] // HALO - 1 for h_ in hs]
    prevs = [spec(off, last, lambda t: t * per - 1) for off, last in zip(offs, lasts)]
    nexts = [spec(off, last, lambda t: (t + 1) * per) for off, last in zip(offs, lasts)]
    return prevs + nexts


def _inproj_body(*refs, groups, conv_idx, n_src, ctx_tiles, nt, seg_starts):
    mains, prevs, nexts = refs[:n_src], refs[n_src:2 * n_src], refs[2 * n_src:3 * n_src]
    g_ref, sh_ref, sc_ref, cos_ref, sin_ref, w_ref, cw_ref, cb_ref = refs[3 * n_src:3 * n_src + 8]
    out_refs = refs[3 * n_src + 8:-2]
    scr, ubuf = refs[-2:]

    def normmod(h):
        return (_rms(h) * g_ref[...]) * (1.0 + sc_ref[...]) + sh_ref[...]

    ubuf[0:TM, :] = normmod(_load_rows(mains, ctx_tiles, 0)).astype(BF16)
    prev_ok, nxt_ok = _halo_flags(pl.program_id(1), nt, seg_starts)
    up = jnp.where(prev_ok, normmod(_load_rows(prevs, ctx_tiles, 0)), 0.0)
    un = jnp.where(nxt_ok, normmod(_load_rows(nexts, ctx_tiles, 0)), 0.0)
    ubuf[TM:TM + 2 * HALO, :] = jnp.concatenate([up, un], axis=0).astype(BF16)
    cos = cos_ref[...]
    sin = sin_ref[...]
    lane = lax.broadcasted_iota(jnp.int32, cos.shape, 1)
    lo = (lane & 31) < 16
    c_start, c_width = groups[conv_idx][:2]
    c_out = out_refs[conv_idx]
    taps = cw_ref.shape[0]
    for a in range(0, c_width, MXU_COLS):
        proj = _dot(ubuf[...], w_ref[:, c_start + a:c_start + a + MXU_COLS])
        scr[0:HALO, a:a + MXU_COLS] = proj[TM:TM + HALO]
        scr[HALO:HALO + TM, a:a + MXU_COLS] = proj[0:TM]
        scr[HALO + TM:, a:a + MXU_COLS] = proj[TM + HALO:]

    def conv_piece(a):
        cols = slice(a, a + MXU_COLS)
        rows = scr[:, cols]
        nrow = rows.shape[0]

        def tap(j):
            off = j - taps // 2
            shifted = rows if off == 0 else pltpu.roll(rows, (-off) % nrow, 0)
            return shifted[HALO:HALO + TM] * cw_ref[j:j + 1, cols]

        y = tap(0)
        for j in range(1, taps):
            y = y + tap(j)
        c_out[:, cols] = _silu(y + cb_ref[:, cols]).astype(c_out.dtype)

    def project_piece(gi, a, step):
        start, _, rope_cols, qscale = groups[gi]
        y = _dot(ubuf[0:TM, :], w_ref[:, start + a:start + a + step])
        for r in range(0, step, LANES):
            yr = y[:, r:r + LANES]
            if a + r < rope_cols:
                yr = _rope_block(yr, cos, sin, lo)
            if qscale != 1.0:
                yr = yr * qscale
            out_refs[gi][:, a + r:a + r + LANES] = yr.astype(out_refs[gi].dtype)

    conv_pieces = [functools.partial(conv_piece, a) for a in range(0, c_width, MXU_COLS)]
    proj_pieces = []
    for gi, (_, width, rope_cols, _) in enumerate(groups):
        if gi != conv_idx:
            step = min(width, MXU_COLS if rope_cols else 2 * MXU_COLS)
            proj_pieces += [functools.partial(project_piece, gi, a, step) for a in range(0, width, step)]
    for i in range(max(len(conv_pieces), len(proj_pieces))):
        if i < len(proj_pieces):
            proj_pieces[i]()
        if i < len(conv_pieces):
            conv_pieces[i]()


def _seg_map(ctx_tiles, off, bt=_grid_bt):
    def index_map(*g):
        b, t = bt(*g)
        return b, jnp.where(t + off >= ctx_tiles, 1, 0), 0, 0
    return index_map


def _inproj_call(hs, g, shift, scale, cos, sin, w, groups, dtypes, ctx_tiles, conv_idx, conv_w, conv_b):
    bsz, _, d = hs[0].shape
    s = sum(h_.shape[1] for h_ in hs)
    nt = s // TM
    seg = _seg_map(ctx_tiles, 0)
    const = lambda arr: pl.BlockSpec(arr.shape, lambda b, t: (0, 0))
    in_specs = _row_specs(hs, ctx_tiles, 0) + _halo_row_specs(hs, ctx_tiles) + [
                pl.BlockSpec((1, d), lambda b, t: (0, 0)),
                pl.BlockSpec((None, None, 1, d), seg),
                pl.BlockSpec((None, None, 1, d), seg),
                pl.BlockSpec((TM, LANES), lambda b, t: (t, 0)),
                pl.BlockSpec((TM, LANES), lambda b, t: (t, 0)),
                const(w), const(conv_w), const(conv_b)]
    out_specs = [pl.BlockSpec((None, TM, gr[1]), lambda b, t: (b, t, 0)) for gr in groups]
    out_shape = [jax.ShapeDtypeStruct((bsz, s, gr[1]), dt) for gr, dt in zip(groups, dtypes)]
    body = functools.partial(_inproj_body, groups=groups, conv_idx=conv_idx, n_src=len(hs), ctx_tiles=ctx_tiles,
                             nt=nt, seg_starts=(0, ctx_tiles))
    return pl.pallas_call(
        body, grid=(bsz, nt), in_specs=in_specs, out_specs=out_specs, out_shape=out_shape,
        scratch_shapes=[pltpu.VMEM((TM + 2 * HALO, groups[conv_idx][1]), F32),
                        pltpu.VMEM((TM + 2 * HALO, d), BF16)],
        compiler_params=_cparams("parallel", "arbitrary"),
        name="in_proj",
    )(*hs, *hs, *hs, g, shift, scale, cos, sin, w, conv_w, conv_b)


def _halo_flags(t, nt, seg_starts):
    prev_ok = t >= 0
    nxt_ok = (t + 1) < nt
    for s0 in seg_starts:
        prev_ok = jnp.logical_and(prev_ok, t != s0)
        nxt_ok = jnp.logical_and(nxt_ok, (t + 1) != s0)
    return prev_ok, nxt_ok


def _max_half_norm2():
    r_i = lax.broadcasted_iota(jnp.int32, (LANES, LANES), 0)
    c_i = lax.broadcasted_iota(jnp.int32, (LANES, LANES), 1)
    ind = jnp.where(c_i == r_i // HEAD_DIM, 1.0, 0.0).astype(BF16)

    def max_norm2(x):
        xf = x.astype(F32)
        return jnp.max(_dot((xf * xf).astype(BF16), ind), axis=0, keepdims=True)
    return max_norm2


def _win_attn_body(sink_ref, bias_ref, q_ref, kvp_ref, kvc_ref, kvn_ref, kvx_ref, o_ref):
    max_norm2 = _max_half_norm2()
    qn2 = max_norm2(q_ref[:, 0:LANES])
    for g in range(1, A_GROUP):
        qn2 = jnp.maximum(qn2, max_norm2(q_ref[:, g * LANES:(g + 1) * LANES]))
    kn2 = max_norm2(kvc_ref[:, 0:LANES])
    for r in (kvp_ref, kvn_ref, kvx_ref):
        kn2 = jnp.maximum(kn2, max_norm2(r[:, 0:LANES]))
    worst = jnp.max(qn2 * kn2, axis=1, keepdims=True)[0, 0] * NORM_SLACK
    sink_max = jnp.abs(sink_ref[0])
    for i in range(1, A_KV_HEADS * A_GROUP):
        sink_max = jnp.maximum(sink_max, jnp.abs(sink_ref[i]))
    small = jnp.logical_and(worst < SAFE_LOGIT * SAFE_LOGIT, sink_max * LOG2E < SAFE_LOGIT)
    refs = (sink_ref, bias_ref, q_ref, kvp_ref, kvc_ref, kvn_ref, kvx_ref, o_ref)

    @pl.when(small)
    def _():
        _win_attn_heads(*refs, stabilise=False)

    @pl.when(jnp.logical_not(small))
    def _():
        _win_attn_heads(*refs, stabilise=True)


def _win_attn_heads(sink_ref, bias_ref, q_ref, kvp_ref, kvc_ref, kvn_ref, kvx_ref, o_ref, stabilise):
    blk = q_ref.shape[0]
    kvcat = jnp.concatenate([kvp_ref[...], kvc_ref[...], kvn_ref[...], kvx_ref[...]], axis=0)
    kcat = kvcat[:, :LANES]
    vcat = kvcat[:, LANES:]
    nkeys = kvcat.shape[0]
    rows = A_GROUP * blk
    bias = bias_ref[...]
    lane = lax.broadcasted_iota(jnp.int32, (blk, LANES), 1)
    rowg = lax.broadcasted_iota(jnp.int32, (rows, 1), 0) // blk
    q = q_ref[...]

    def logits(h):
        half = (lane >= h * HEAD_DIM) & (lane < (h + 1) * HEAD_DIM)
        qh = jnp.concatenate(
            [jnp.where(half, q[:, g * LANES:(g + 1) * LANES], jnp.zeros((), BF16)) for g in range(A_GROUP)], axis=0)
        return _dot_nt(qh, kcat)

    raw = [logits(h) for h in range(A_KV_HEADS)]
    outs = []
    for h in range(A_KV_HEADS):
        s = (raw[h].reshape(A_GROUP, blk, nkeys) + bias[None]).reshape(rows, nkeys)
        sink = jnp.zeros((rows, 1), F32)
        for g in range(A_GROUP):
            sink = jnp.where(rowg == g, sink_ref[h * A_GROUP + g] * LOG2E, sink)
        if stabilise:
            m = jnp.maximum(jnp.max(s, axis=1, keepdims=True), sink)
            p = jnp.exp2(s - m)
            den = jnp.sum(p, axis=1, keepdims=True) + jnp.exp2(sink - m)
        else:
            p = jnp.exp2(s)
            den = jnp.sum(p, axis=1, keepdims=True) + jnp.exp2(sink)
        outs.append(_dot(p.astype(BF16), vcat) / den)
    for g in range(A_GROUP):
        o_ref[:, g * LANES:(g + 1) * LANES] = jnp.where(
            lane < HEAD_DIM, outs[0][g * blk:(g + 1) * blk], outs[1][g * blk:(g + 1) * blk]).astype(o_ref.dtype)


def _win_attn_bias(ctx_len):
    qi = jnp.arange(WQ)[:, None]
    kj = jnp.arange(WQ + 2 * A_WINDOW)[None, :]
    band = jnp.abs(kj - A_WINDOW - qi) <= A_WINDOW
    variants = [jnp.zeros_like(band), band & (kj >= A_WINDOW), band, band & (kj < A_WINDOW + WQ)]
    win = jnp.stack([jnp.where(v_, 0.0, NEG) for v_ in variants]).astype(F32)
    return jnp.concatenate([win, jnp.zeros((4, WQ, ctx_len), F32)], axis=2)


def _win_attn_call(q, kv, sink, ctx_len):
    bsz, s, qw = q.shape
    kw = kv.shape[-1]
    nq = s // WQ
    cq = ctx_len // WQ
    per = WQ // A_WINDOW
    nblk = s // A_WINDOW
    assert ctx_len % WQ == 0 and nq - cq >= 2
    bias = _win_attn_bias(ctx_len)
    clamp = lambda j: jnp.clip(j, cq * per, nblk - 1)
    variant = lambda i: jnp.where(i < cq, 0, jnp.where(i == cq, 1, jnp.where(i == nq - 1, 3, 2)))
    kv_specs = [pl.BlockSpec((None, A_WINDOW, kw), lambda b, i: (b, clamp(i * per - 1), 0)),
                pl.BlockSpec((None, WQ, kw), lambda b, i: (b, i, 0)),
                pl.BlockSpec((None, A_WINDOW, kw), lambda b, i: (b, clamp((i + 1) * per), 0)),
                pl.BlockSpec((None, ctx_len, kw), lambda b, i: (b, 0, 0))]
    return pl.pallas_call(
        _win_attn_body,
        grid=(bsz, nq),
        in_specs=[pl.BlockSpec(memory_space=pltpu.SMEM),
                  pl.BlockSpec((None,) + bias.shape[1:], lambda b, i: (variant(i), 0, 0)),
                  pl.BlockSpec((None, WQ, qw), lambda b, i: (b, i, 0))] + kv_specs,
        out_specs=pl.BlockSpec((None, WQ, qw), lambda b, i: (b, i, 0)),
        out_shape=jax.ShapeDtypeStruct((bsz, s, qw), BF16),
        compiler_params=_cparams("parallel", "arbitrary"),
        name="window_attention",
    )(sink, bias, q, kv, kv, kv, kv)


def _bwd_chunk(j, ctx_chunks, n_chunks):
    return jnp.where(j < ctx_chunks, ctx_chunks - 1 - j, n_chunks + ctx_chunks - 1 - j)


def _scan_rows(dirs):
    return [(bb, d, tuple(r.at[bb] for r in refs)) for bb in range(SCAN_ROWS) for d, refs in enumerate(dirs)]


def _tri_masks(d):
    row = lax.broadcasted_iota(jnp.int32, (CHUNK, CHUNK), 0)
    col = lax.broadcasted_iota(jnp.int32, (CHUNK, CHUNK), 1)
    keep = (col <= row) if d == 0 else (col >= row)
    return keep, jnp.where(keep, 1.0, 0.0).astype(BF16)


def _mlstm_body(qkf_ref, vf_ref, gf_ref, qkb_ref, vb_ref, gb_ref, bias_ref, hf_ref, hb_ref, c_scr, m_scr):
    L = CHUNK
    dk = HEAD_DIM
    dv = LANES

    @pl.when(pl.program_id(1) == 0)
    def _():
        c_scr[...] = jnp.zeros_like(c_scr)
        m_scr[...] = jnp.zeros_like(m_scr)

    lane = lax.broadcasted_iota(jnp.int32, (L, LANES), 1)
    srow = lax.broadcasted_iota(jnp.int32, (LANES, 2 * dv), 0)
    ones_v = jnp.ones((L, dv), BF16)
    dirs = ((qkf_ref, vf_ref, gf_ref, hf_ref), (qkb_ref, vb_ref, gb_ref, hb_ref))
    for bb, d, (qk_ref, v_ref, g_ref, o_ref) in _scan_rows(dirs):
        keep, tri = _tri_masks(d)
        gates = g_ref[...] + bias_ref[...]
        gates_t = gates.T[0:16, :]
        b_col = _cumsum_cols(tri, _log_sigmoid(gates))
        b_row = _cumsum_rows(_log_sigmoid(gates_t), tri)
        k_pairs = [qk_ref[:, B_HEADS * dk + p * LANES:B_HEADS * dk + (p + 1) * LANES] for p in range(B_HEADS // 2)]
        kt_pairs = [kp.astype(F32).T.astype(BF16) for kp in k_pairs]
        for h in range(B_HEADS):
            ci = d * B_HEADS + h
            cf = 2 * B_HEADS + ci
            r = bb * 2 * B_HEADS + ci
            li_c = jnp.broadcast_to(gates[:, ci:ci + 1], (L, LANES))
            li_r = gates_t[ci:ci + 1, :]
            bc = jnp.broadcast_to(b_col[:, cf:cf + 1], (L, LANES))
            br = b_row[cf:cf + 1, :]
            b_last = bc[L - 1:L, :] if d == 0 else bc[0:1, :]
            m_prev = m_scr[r:r + 1, :]
            pair = (h // 2) * LANES
            half = (lane >= (h % 2) * dk) & (lane < (h % 2 + 1) * dk)
            q = jnp.where(half, qk_ref[:, pair:pair + LANES] * (dk ** -0.5), 0.0).astype(BF16)
            k = k_pairs[h // 2].astype(BF16)
            v = v_ref[:, h * dv:(h + 1) * dv]
            dm = jnp.where(keep, bc - br + li_r, NEG)
            g_in = bc + m_prev
            m_t = jnp.maximum(jnp.broadcast_to(jnp.max(dm, axis=1, keepdims=True), (L, LANES)), g_in)
            sm = (_dot_nt(q, k) * jnp.exp(dm - m_t)).astype(BF16)
            vaug = jnp.concatenate([v, ones_v], axis=1)
            c_prev = c_scr[r]
            intra = _dot(sm, vaug)
            inter = _dot(q, c_prev.astype(BF16))
            a_in = jnp.exp(g_in - m_t)
            den = jnp.maximum(jnp.abs(intra[:, dv:] + a_in * inter[:, dv:]), jnp.exp(-m_t))
            o_ref[:, h * dv:(h + 1) * dv] = ((intra[:, :dv] + a_in * inter[:, :dv]) / den).astype(o_ref.dtype)
            w_c = b_last - bc + li_c
            m_loc = jnp.max(w_c, axis=0, keepdims=True)
            e = jnp.exp(w_c - m_loc)
            ev = jnp.concatenate([e * v.astype(F32), e], axis=1).astype(BF16)
            c_loc = _dot(kt_pairs[h // 2], ev)
            own = (srow >= (h % 2) * dk) & (srow < (h % 2 + 1) * dk)
            m_new = jnp.maximum(b_last + m_prev, m_loc)
            keep_old = jnp.exp(b_last + m_prev - m_new)
            take_new = jnp.exp(m_loc - m_new)
            c_scr[r] = (jnp.concatenate([keep_old, keep_old], axis=1) * c_prev
                        + jnp.where(own, jnp.concatenate([take_new, take_new], axis=1) * c_loc, 0.0))
            m_scr[r:r + 1, :] = m_new


def _scan_specs(width, ctx_chunks, n_chunks):
    fwd = pl.BlockSpec((SCAN_ROWS, CHUNK, width), lambda b, j: (b, j, 0))
    bwd = pl.BlockSpec((SCAN_ROWS, CHUNK, width), lambda b, j: (b, _bwd_chunk(j, ctx_chunks, n_chunks), 0))
    return fwd, bwd


def _mlstm_call(qk, v, gates, bias, ctx_len):
    bsz, s, w = v.shape
    nc = s // CHUNK
    cc = ctx_len // CHUNK
    qf, qb = _scan_specs(qk.shape[-1], cc, nc)
    vf, vb = _scan_specs(w, cc, nc)
    gf, gb = _scan_specs(LANES, cc, nc)
    return pl.pallas_call(
        _mlstm_body,
        grid=(bsz // SCAN_ROWS, nc),
        in_specs=[qf, vf, gf, qb, vb, gb, pl.BlockSpec((1, LANES), lambda b, j: (0, 0))],
        out_specs=[vf, vb],
        out_shape=[jax.ShapeDtypeStruct((bsz, s, w), BF16)] * 2,
        scratch_shapes=[pltpu.VMEM((SCAN_ROWS * 2 * B_HEADS, LANES, 2 * LANES), F32),
                        pltpu.VMEM((SCAN_ROWS * 2 * B_HEADS, LANES), F32)],
        compiler_params=_cparams("parallel", "arbitrary"),
        name="mlstm_scan",
    )(qk, v, gates, qk, v, gates, bias)


def _ssd_body(xf_ref, dtf_ref, xb_ref, dtb_ref, dtbias_ref, nega_ref, skip_ref, yf_ref, yb_ref, s_scr):
    L = CHUNK
    hd = HEAD_DIM
    inner = 4 * D_GROUPS * hd
    hpg = 4

    @pl.when(pl.program_id(1) == 0)
    def _():
        s_scr[...] = jnp.zeros_like(s_scr)

    lane = lax.broadcasted_iota(jnp.int32, (L, LANES), 1)
    lo = lane < hd
    lo1 = lo[0:1, :]
    dirs = ((xf_ref, dtf_ref, yf_ref), (xb_ref, dtb_ref, yb_ref))
    for bb, d, (x_ref, dt_ref, y_ref) in _scan_rows(dirs):
        keep, tri = _tri_masks(d)
        dt = _softplus(dt_ref[...] + dtbias_ref[...])
        a = dt * nega_ref[...]
        dt_t = dt.T[0:16, :]
        a_t = a.T[0:16, :]
        ac_col = _cumsum_cols(tri, a)
        ac_row = _cumsum_rows(a_t, tri)
        for g in range(D_GROUPS):
            bg_f = x_ref[:, inner + g * LANES:inner + (g + 1) * LANES]
            bg = bg_f.astype(BF16)
            bg_t = bg_f.T.astype(BF16)
            cg = x_ref[:, inner + (D_GROUPS + g) * LANES:inner + (D_GROUPS + g + 1) * LANES].astype(BF16)
            cb = _dot_nt(cg, bg)
            for pr in range(2):
                c0 = g * hpg * hd + pr * LANES
                xp = x_ref[:, c0:c0 + LANES]
                xpb = xp.astype(BF16)
                ys, eas, wsts, als = [], [], [], []
                for hh in range(2):
                    col = d * D_GROUPS * hpg + g * hpg + pr * 2 + hh
                    acc = jnp.broadcast_to(ac_col[:, col:col + 1], (L, LANES))
                    acr = ac_row[col:col + 1, :]
                    seg = jnp.where(keep, acc - acr, NEG)
                    mix = (cb * jnp.exp(seg) * dt_t[col:col + 1, :]).astype(BF16)
                    ys.append(_dot(mix, xpb))
                    a_last = acc[L - 1:L, :] if d == 0 else acc[0:1, :]
                    eas.append(jnp.exp(acc))
                    wsts.append(jnp.exp(a_last - acc) * jnp.broadcast_to(dt[:, col:col + 1], (L, LANES)))
                    als.append(jnp.exp(a_last))
                sidx = (bb * 2 + d) * 2 * D_GROUPS + g * 2 + pr
                st = s_scr[sidx]
                y = jnp.where(lo, ys[0], ys[1]) + _dot(cg, st.astype(BF16)) * jnp.where(lo, eas[0], eas[1])
                if d == 0:
                    y = y + skip_ref[:, c0:c0 + LANES] * xp
                y_ref[:, c0:c0 + LANES] = y.astype(y_ref.dtype)
                xw = (xp * jnp.where(lo, wsts[0], wsts[1])).astype(BF16)
                s_scr[sidx] = jnp.where(lo1, als[0], als[1]) * st + _dot(bg_t, xw)


def _ssd_call(xbc, dt, dt_bias, neg_a, skip, ctx_len):
    bsz, s, w = xbc.shape
    nc = s // CHUNK
    cc = ctx_len // CHUNK
    inner = skip.shape[-1]
    xf, xb = _scan_specs(w, cc, nc)
    df, db = _scan_specs(LANES, cc, nc)
    yf, yb = _scan_specs(inner, cc, nc)
    vec = lambda n: pl.BlockSpec((1, n), lambda b, j: (0, 0))
    return pl.pallas_call(
        _ssd_body,
        grid=(bsz // SCAN_ROWS, nc),
        in_specs=[xf, df, xb, db, vec(LANES), vec(LANES), vec(inner)],
        out_specs=[yf, yb],
        out_shape=[jax.ShapeDtypeStruct((bsz, s, inner), BF16)] * 2,
        scratch_shapes=[pltpu.VMEM((SCAN_ROWS * 4 * D_GROUPS, LANES, LANES), F32)],
        compiler_params=_cparams("parallel", "arbitrary"),
        name="ssd_scan",
    )(xbc, dt, xbc, dt, dt_bias, neg_a, skip)


def _diff_attn_body(lam_ref, q_ref, k_ref, v_ref, o_ref, kn_scr, *, lam_init):
    nheads = q_ref.shape[1] // LANES
    max_norm2 = _max_half_norm2()

    @pl.when(pl.program_id(1) == 0)
    def _():
        for h in range(nheads):
            kn_scr[h:h + 1, :] = max_norm2(k_ref[:, h * LANES:(h + 1) * LANES])

    bound2 = max_norm2(q_ref[:, 0:LANES]) * kn_scr[0:1, :]
    for h in range(1, nheads):
        bound2 = jnp.maximum(bound2, max_norm2(q_ref[:, h * LANES:(h + 1) * LANES]) * kn_scr[h:h + 1, :])
    worst = jnp.max(bound2, axis=1, keepdims=True)[0, 0] * NORM_SLACK
    small = worst < SAFE_LOGIT * SAFE_LOGIT

    @pl.when(small)
    def _():
        _diff_attn_heads(lam_ref, q_ref, k_ref, v_ref, o_ref, lam_init, stabilise=False)

    @pl.when(jnp.logical_not(small))
    def _():
        _diff_attn_heads(lam_ref, q_ref, k_ref, v_ref, o_ref, lam_init, stabilise=True)


def _diff_attn_heads(lam_ref, q_ref, k_ref, v_ref, o_ref, lam_init, stabilise):
    lv = lam_ref[...]
    lam = (jnp.exp(jnp.sum(lv[0:1] * lv[1:2], axis=1, keepdims=True))
           - jnp.exp(jnp.sum(lv[2:3] * lv[3:4], axis=1, keepdims=True)) + lam_init)
    tq = q_ref.shape[0]
    lane = lax.broadcasted_iota(jnp.int32, (tq, LANES), 1)
    nheads = q_ref.shape[1] // LANES

    def logits(h):
        q = q_ref[:, h * LANES:(h + 1) * LANES]
        k = k_ref[:, h * LANES:(h + 1) * LANES]
        return [_dot_nt(jnp.where((lane >= m * HEAD_DIM) & (lane < (m + 1) * HEAD_DIM), q, jnp.zeros((), BF16)), k)
                for m in range(2)]

    s_next = logits(0)
    for h in range(nheads):
        s_cur = s_next
        if h + 1 < nheads:
            s_next = logits(h + 1)
        ps, ls = [], []
        for s in s_cur:
            p = jnp.exp2(s - jnp.max(s, axis=1, keepdims=True)) if stabilise else jnp.exp2(s)
            ps.append(p)
            ls.append(jnp.sum(p, axis=1, keepdims=True))
        a = (ps[0] - (lam * ls[0] / ls[1]) * ps[1]).astype(BF16)
        o_ref[:, h * LANES:(h + 1) * LANES] = _dot(a, v_ref[:, h * LANES:(h + 1) * LANES]) / ls[0]


def _diff_attn_call(q, k, v, lam_vecs, lam_init, ctx_len):
    bsz, s, w = q.shape
    t = s - ctx_len
    off = ctx_len // TQ
    return pl.pallas_call(
        functools.partial(_diff_attn_body, lam_init=lam_init),
        grid=(bsz, t // TQ),
        in_specs=[pl.BlockSpec(lam_vecs.shape, lambda b, i: (0, 0)),
                  pl.BlockSpec((None, TQ, w), lambda b, i: (b, i + off, 0)),
                  pl.BlockSpec((None, s, w), lambda b, i: (b, 0, 0)),
                  pl.BlockSpec((None, s, w), lambda b, i: (b, 0, 0))],
        out_specs=pl.BlockSpec((None, TQ, w), lambda b, i: (b, i, 0)),
        out_shape=jax.ShapeDtypeStruct((bsz, t, w), F32),
        scratch_shapes=[pltpu.VMEM((HALO, LANES), F32)],
        compiler_params=_cparams("parallel", "arbitrary"),
        name="diff_attention",
    )(lam_vecs, q, k, v)


def _group_rms(x, width):
    return jnp.concatenate([_rms(x[:, a:a + width]) for a in range(0, x.shape[1], width)], axis=1)


def _outproj_even_body(ya_ref, hf_ref, hb_ref, og_ref, ng_ref, w_ref, g_ref, out_ref):
    half = ya_ref.shape[1]
    f32 = lambda r: r[...].astype(F32)
    hn = _group_rms(f32(hf_ref) + f32(hb_ref), LANES) * ng_ref[...] * _sigmoid(f32(og_ref))
    y = _dot(ya_ref[...], w_ref[0:half, :]) + _dot(hn.astype(BF16), w_ref[half:, :])
    out_ref[...] = (_rms(y) * g_ref[...]).astype(out_ref.dtype)


def _outproj_odd_body(oa_ref, yf_ref, yb_ref, z_ref, cg_ref, dg_ref, w_ref, g_ref, out_ref, *, out_scale):
    half = oa_ref.shape[1]
    on = _group_rms(oa_ref[...], LANES) * cg_ref[...] * out_scale
    f32 = lambda r: r[...].astype(F32)
    yz = (f32(yf_ref) + f32(yb_ref)) * _silu(f32(z_ref))
    sn = _group_rms(yz, half // D_GROUPS) * dg_ref[...]
    y = _dot(on.astype(BF16), w_ref[0:half, :]) + _dot(sn.astype(BF16), w_ref[half:, :])
    out_ref[...] = (_rms(y) * g_ref[...]).astype(out_ref.dtype)


def _outproj_call(body, acts, act_offs, vecs, w, g, n_t):
    bsz = acts[0].shape[0]
    d = w.shape[1]

    def act_spec(arr, off):
        return pl.BlockSpec((None, TM, arr.shape[-1]), lambda b, t: (b, t + off, 0))

    in_specs = ([act_spec(a_, o_) for a_, o_ in zip(acts, act_offs)]
                + [pl.BlockSpec(v_.shape, lambda b, t: (0, 0)) for v_ in vecs]
                + [pl.BlockSpec(w.shape, lambda b, t: (0, 0)),
                   pl.BlockSpec((1, d), lambda b, t: (0, 0))])
    return pl.pallas_call(
        body, grid=(bsz, n_t), in_specs=in_specs,
        out_specs=pl.BlockSpec((None, TM, d), lambda b, t: (b, t, 0)),
        out_shape=jax.ShapeDtypeStruct((bsz, n_t * TM, d), BF16),
        compiler_params=_cparams("parallel", "arbitrary"),
        name="out_proj",
    )(*acts, *vecs, w, g)


def _ffn_body(*refs, n_src, ctx_tiles, t_off, nt, seg_starts):
    mains, prevs, nexts = refs[:n_src], refs[n_src:2 * n_src], refs[2 * n_src:3 * n_src]
    (ym_ref, yp_ref, yn_ref, gate1_ref, g2_ref, sh_ref, sc_ref, wg_ref, wu_ref, cw_ref, wd_ref, g3_ref, gate_ref,
     out_ref, gscr, uscr) = refs[3 * n_src:]
    t = pl.program_id(1)
    prev_ok, nxt_ok = _halo_flags(t, nt, seg_starts)

    def resid(h_refs, y):
        return _load_rows(h_refs, ctx_tiles, t_off) + gate1_ref[...] * y

    def normmod(h):
        return (_rms(h) * g2_ref[...]) * (1.0 + sc_ref[...]) + sh_ref[...]

    hm = resid(mains, ym_ref[...].astype(F32))
    um = normmod(hm)
    up = jnp.where(prev_ok, normmod(resid(prevs, yp_ref[...].astype(F32)[Y_HALO - HALO:, :])), 0.0)
    un = jnp.where(nxt_ok, normmod(resid(nexts, yn_ref[...].astype(F32)[:HALO, :])), 0.0)
    ucat = jnp.concatenate([up, um, un], axis=0).astype(BF16)
    umb = um.astype(BF16)

    nchunk = wg_ref.shape[1] // FFN_CHUNK
    cols = lambda c: slice(c * FFN_CHUNK, (c + 1) * FFN_CHUNK)
    nrow = TM + 2 * HALO

    def front(c):
        gscr[c % 2] = _dot(ucat, wg_ref[:, cols(c)])
        uscr[c % 2] = _dot(umb, wu_ref[:, cols(c)])

    front(0)
    y = None
    for c in range(nchunk):
        if c + 1 < nchunk:
            front(c + 1)
        g = gscr[c % 2]
        gc = (pltpu.roll(g, 1, 0)[HALO:HALO + TM] * cw_ref[0:1, cols(c)] + g[HALO:HALO + TM] * cw_ref[1:2, cols(c)]
              + pltpu.roll(g, nrow - 1, 0)[HALO:HALO + TM] * cw_ref[2:3, cols(c)])
        hid = (_silu(gc) * uscr[c % 2]).astype(BF16)
        part = _dot(hid, wd_ref[cols(c), :])
        y = part if y is None else y + part
    out_ref[...] = hm + gate_ref[...] * (_rms(y) * g3_ref[...])


def _ffn_call(hs, yn, gate1, g2, shift, scale, wg, wu, cw, wd, g3, gate2, ctx_tiles, t_off, seg_starts):
    bsz, rows, d = yn.shape
    n_t = rows // TM
    per = TM // Y_HALO
    last = rows // Y_HALO - 1
    const = lambda arr: pl.BlockSpec(arr.shape, lambda b, t: (0, 0))
    vec = pl.BlockSpec((1, d), lambda b, t: (0, 0))
    mod = pl.BlockSpec((None, None, 1, d), _seg_map(ctx_tiles, t_off))
    y_specs = [pl.BlockSpec((None, TM, d), lambda b, t: (b, t, 0)),
               pl.BlockSpec((None, Y_HALO, d), lambda b, t: (b, jnp.maximum(t * per - 1, 0), 0)),
               pl.BlockSpec((None, Y_HALO, d), lambda b, t: (b, jnp.minimum((t + 1) * per, last), 0))]
    body = functools.partial(_ffn_body, n_src=len(hs), ctx_tiles=ctx_tiles, t_off=t_off, nt=n_t, seg_starts=seg_starts)
    return pl.pallas_call(
        body, grid=(bsz, n_t),
        in_specs=(_row_specs(hs, ctx_tiles, t_off) + _halo_row_specs(hs, ctx_tiles, t_off) + y_specs
                  + [mod, vec, mod, mod, const(wg), const(wu), const(cw), const(wd), vec, mod]),
        out_specs=pl.BlockSpec((None, TM, d), lambda b, t: (b, t, 0)),
        out_shape=jax.ShapeDtypeStruct((bsz, rows, d), F32),
        scratch_shapes=[pltpu.VMEM((2, TM + 2 * HALO, FFN_CHUNK), F32), pltpu.VMEM((2, TM, FFN_CHUNK), F32)],
        compiler_params=_cparams("parallel", "arbitrary"),
        name="conv_ffn",
    )(*hs, *hs, *hs, yn, yn, yn, gate1, g2, shift, scale, wg, wu, cw, wd, g3, gate2)


def _rope_tables(ctx_len, t):
    pos = jnp.arange(t)
    row = (pos // GRID_W).astype(F32)
    col = (pos % GRID_W).astype(F32)
    nf = HEAD_DIM // 4
    inv = ROPE_BASE ** (-jnp.arange(nf, dtype=F32) / nf)
    ar = row[:, None] * inv
    ac = col[:, None] * inv
    cos = jnp.concatenate([jnp.cos(ar), jnp.cos(ar), jnp.cos(ac), jnp.cos(ac)], axis=1)
    sin = jnp.concatenate([-jnp.sin(ar), jnp.sin(ar), -jnp.sin(ac), jnp.sin(ac)], axis=1)
    cos = jnp.concatenate([jnp.ones((ctx_len, HEAD_DIM), F32), cos], axis=0)
    sin = jnp.concatenate([jnp.zeros((ctx_len, HEAD_DIM), F32), sin], axis=0)
    return jnp.tile(cos, (1, LANES // HEAD_DIM)), jnp.tile(sin, (1, LANES // HEAD_DIM))


def _pad_cols(w, n):
    return jnp.pad(w, ((0, 0), (0, n - w.shape[1])))


def _gqa_perm():
    idx = [(h * A_GROUP + g) * HEAD_DIM + dd
           for g in range(A_GROUP) for h in range(A_KV_HEADS) for dd in range(HEAD_DIM)]
    return jnp.array(idx, dtype=jnp.int32)


def _ffn_weights(w_gate, w_up, conv, w_down):
    return w_gate.astype(BF16), w_up.astype(BF16), conv, w_down.astype(BF16)


def kernel(x, c, ctx, c_ctx, mod_w, mod_b, norm_g, ffn_w_gate, ffn_w_up, ffn_conv, ffn_w_down, ev_w_in, ev_w_out, a_sink, b_conv, b_gate_b, b_norm_g, od_w_in, od_w_out, c_lambda, c_norm_g, d_conv, d_conv_b, d_dt_bias, d_a_log, d_skip, d_norm_g):
    bsz, t, d = x.shape
    ctx_len = ctx.shape[1]
    depth = mod_w.shape[0]
    s = ctx_len + t
    half = d // 2
    assert ctx_len % TM == 0 and t % TM == 0 and t % GRID_W == 0 and ffn_w_gate.shape[-1] % FFN_CHUNK == 0
    assert bsz % SCAN_ROWS == 0
    n_t = s // TM
    ctx_tiles = ctx_len // TM

    mod_rows = -(-(bsz + 1) // HALO) * HALO
    cc = jnp.concatenate([c, c_ctx[None, :], jnp.zeros((mod_rows - bsz - 1, d), F32)], axis=0)
    mods = _mod_call(cc, mod_w, mod_b).reshape(depth, mod_rows, 6, d)

    def layer_mods(l):
        lat = mods[l, :bsz]
        cx = jnp.broadcast_to(mods[l, bsz][None], (bsz, 6, d))
        m = jnp.stack([cx, lat], axis=1)
        return [m[:, :, i][:, :, None, :] for i in range(6)]

    cos, sin = _rope_tables(ctx_len, t)
    hs = (ctx, x)

    for l in range(depth):
        last = l == depth - 1
        j = l // 2
        m = layer_mods(l)
        g = norm_g[l]
        wg, wu, cw, wd = _ffn_weights(ffn_w_gate[l], ffn_w_up[l], ffn_conv[l], ffn_w_down[l])
        if l % 2 == 0:
            w_in = ev_w_in[j]
            perm = _gqa_perm()
            akv = A_KV_HEADS * HEAD_DIM
            o1 = half + 2 * akv
            w_cat = jnp.concatenate([w_in[:, :half][:, perm], w_in[:, half:o1 + 3 * half],
                                     _pad_cols(w_in[:, o1 + 3 * half:], LANES)], axis=1).astype(BF16)
            groups = ((0, half, half, HEAD_DIM ** -0.5 * LOG2E), (half, 2 * akv, akv, 1.0),
                      (o1, half, 0, 1.0), (o1 + half, half, 0, 1.0), (o1 + 2 * half, half, 0, 1.0),
                      (o1 + 3 * half, LANES, 0, 1.0))
            q, kv, qkc, mv, mo, gates = _inproj_call(
                hs, g[0:1], m[0], m[1], cos, sin, w_cat, groups, (BF16, BF16, BF16, BF16, BF16, F32), ctx_tiles,
                2, b_conv[j], jnp.zeros((1, half), F32))
            ya = _win_attn_call(q, kv, a_sink[j], ctx_len)
            hf, hb = _mlstm_call(qkc, mv, gates, _pad_cols(b_gate_b[j][None, :], LANES), ctx_len)
            w_out = ev_w_out[j]
            w_out = jnp.concatenate([w_out[:half][perm], w_out[half:]], axis=0).astype(BF16)
            acts, vecs, body = (ya, hf, hb, mo), (b_norm_g[j][None, :],), _outproj_even_body
        else:
            w_in = od_w_in[j]
            xbc_w = 2 * half
            w_cat = jnp.concatenate([w_in[:, :4 * half + xbc_w],
                                     _pad_cols(w_in[:, 4 * half + xbc_w:], LANES)], axis=1).astype(BF16)
            groups = ((0, half, half, HEAD_DIM ** -0.5 * LOG2E), (half, half, half, 1.0), (2 * half, half, 0, 1.0),
                      (3 * half, half, 0, 1.0), (4 * half, xbc_w, 0, 1.0), (4 * half + xbc_w, LANES, 0, 1.0))
            q, k, v, z, xbcc, dt = _inproj_call(
                hs, g[0:1], m[0], m[1], cos, sin, w_cat, groups, (BF16, BF16, BF16, BF16, F32, F32), ctx_tiles,
                4, d_conv[j], d_conv_b[j][None, :])
            lam_init = 0.8 - 0.6 * math.exp(-LAM_DEPTH_RATE * l)
            oa = _diff_attn_call(q, k, v, c_lambda[j], lam_init, ctx_len)
            dt_bias = _pad_cols(d_dt_bias[j].reshape(1, -1), LANES)
            neg_a = _pad_cols(-jnp.exp(d_a_log[j].astype(F32)).reshape(1, -1), LANES)
            skip = jnp.repeat(d_skip[j].astype(F32), HEAD_DIM)[None, :]
            yf, yb = _ssd_call(xbcc, dt, dt_bias, neg_a, skip, ctx_len)
            w_out = od_w_out[j].astype(BF16)
            acts, vecs = (oa, yf, yb, z), (c_norm_g[j][None, :], d_norm_g[j][None, :])
            body = functools.partial(_outproj_odd_body, out_scale=1.0 - lam_init)
        t_off = ctx_tiles if last else 0
        if l % 2 == 0:
            act_offs = (t_off,) * 4
        else:
            assert last, "differential attention is only computed for latent queries"
            act_offs = (0,) + (t_off,) * 3
        yn = _outproj_call(body, acts, act_offs, vecs, w_out, g[1:2], n_t - t_off)
        seg_starts = (0,) if last else (0, ctx_tiles)
        hs = (_ffn_call(hs, yn, m[2], g[2:3], m[3], m[4], wg, wu, cw, wd, g[3:4], m[5], ctx_tiles, t_off,
                        seg_starts),)
    return hs[0]
```

```python
import functools
import math

import jax
import jax.numpy as jnp
from jax import lax
from jax.experimental import pallas as pl
from jax.experimental.pallas import tpu as pltpu

F32 = jnp.float32
BF16 = jnp.bfloat16

EPS = 1e-6
ROPE_BASE = 10000.0
GRID_W = 64
HEAD_DIM = 64
A_KV_HEADS = 2
A_GROUP = 4
A_WINDOW = 128
B_HEADS = 4
D_GROUPS = 2
LAM_DEPTH_RATE = 0.3

LANES = 128
MXU_COLS = 256
HALO = 8
Y_HALO = 16
TM = 256
CHUNK = 128
SCAN_ROWS = 4
FFN_CHUNK = 256
TQ = 256
WQ = 256
NEG = -1e30
SAFE_LOGIT = 40.0
NORM_SLACK = 1.05
LOG2E = math.log2(math.e)
VMEM_LIMIT = 56 * 1024 * 1024


def _cparams(*sem):
    return pltpu.CompilerParams(dimension_semantics=sem, vmem_limit_bytes=VMEM_LIMIT)


def _sigmoid(x):
    return 0.5 * jnp.tanh(0.5 * x) + 0.5


def _silu(x):
    return x * _sigmoid(x)


def _softplus(x):
    return jnp.maximum(x, 0.0) + jnp.log(1.0 + jnp.exp(-jnp.abs(x)))


def _log_sigmoid(x):
    return -_softplus(-x)


def _rms(x):
    return x * lax.rsqrt(jnp.mean(x * x, axis=-1, keepdims=True) + EPS)


def _dot(a, b):
    return jnp.dot(a, b, preferred_element_type=F32)


def _dot_nt(a, b):
    return lax.dot_general(a, b, (((1,), (1,)), ((), ())), preferred_element_type=F32)


def _dot_tn(a, b):
    return lax.dot_general(a, b, (((0,), (0,)), ((), ())), preferred_element_type=F32)


def _split3(x):
    x1 = x.astype(BF16)
    r1 = x - x1.astype(F32)
    x2 = r1.astype(BF16)
    x3 = (r1 - x2.astype(F32)).astype(BF16)
    return x1, x2, x3


def _cumsum_cols(tri, x):
    x1, x2, x3 = _split3(x)
    return _dot(tri, x1) + _dot(tri, x2) + _dot(tri, x3)


def _cumsum_rows(x, tri):
    x1, x2, x3 = _split3(x)
    return _dot_nt(x1, tri) + _dot_nt(x2, tri) + _dot_nt(x3, tri)


def _mod_body(c_ref, w_ref, b_ref, o_ref):
    a = _silu(c_ref[...]).astype(BF16)
    o_ref[...] = _dot(a, w_ref[...].astype(BF16)) + b_ref[...]


def _mod_call(cc, mod_w, mod_b):
    depth, d, n6 = mod_w.shape
    rows = cc.shape[0]
    tn = 1024
    return pl.pallas_call(
        _mod_body,
        grid=(depth, n6 // tn),
        in_specs=[pl.BlockSpec((rows, d), lambda l, j: (0, 0)),
                  pl.BlockSpec((None, d, tn), lambda l, j: (l, 0, j)),
                  pl.BlockSpec((None, 1, tn), lambda l, j: (l, 0, j))],
        out_specs=pl.BlockSpec((None, rows, tn), lambda l, j: (l, 0, j)),
        out_shape=jax.ShapeDtypeStruct((depth, rows, n6), F32),
        compiler_params=_cparams("arbitrary", "arbitrary"),
        name="modulation",
    )(cc, mod_w, mod_b.reshape(depth, 1, n6))


def _rope_block(y, cos, sin, lo):
    partner = jnp.where(lo, pltpu.roll(y, LANES - 16, 1), pltpu.roll(y, 16, 1))
    return y * cos + partner * sin


def _grid_bt(b, t):
    return b, t


def _row_specs(hs, ctx_tiles, t_off, bt=_grid_bt):
    d = hs[0].shape[-1]

    def spec(tile_of):
        def index_map(*g):
            b, t = bt(*g)
            return b, tile_of(t + t_off), 0
        return pl.BlockSpec((None, TM, d), index_map)

    if len(hs) == 1:
        return [spec(lambda t: t)]
    return [spec(lambda t: jnp.minimum(t, ctx_tiles - 1)), spec(lambda t: jnp.maximum(t - ctx_tiles, 0))]


def _load_rows(h_refs, ctx_tiles, t_off, t=None):
    if len(h_refs) == 1:
        return h_refs[0][...]
    t = pl.program_id(1) if t is None else t
    return jnp.where(t + t_off < ctx_tiles, h_refs[0][...], h_refs[1][...])


def _halo_row_specs(hs, ctx_tiles, t_off=0, bt=_grid_bt):
    d = hs[0].shape[-1]
    per = TM // HALO
    offs = (0,) if len(hs) == 1 else (0, ctx_tiles)

    def spec(off, last, first_row_block):
        def index_map(*g):
            b, t = bt(*g)
            return b, jnp.clip(first_row_block(t + t_off - off), 0, last), 0
        return pl.BlockSpec((None, HALO, d), index_map)

    lasts = [h_.shape[1] // HALO - 1 for h_ in hs]
    prevs = [spec(off, last, lambda t: t * per - 1) for off, last in zip(offs, lasts)]
    nexts = [spec(off, last, lambda t: (t + 1) * per) for off, last in zip(offs, lasts)]
    return prevs + nexts


def _inproj_body(*refs, groups, conv_idx, n_src, ctx_tiles, nt, seg_starts):
    mains, prevs, nexts = refs[:n_src], refs[n_src:2 * n_src], refs[2 * n_src:3 * n_src]
    g_ref, sh_ref, sc_ref, cos_ref, sin_ref, w_ref, cw_ref, cb_ref = refs[3 * n_src:3 * n_src + 8]
    out_refs = refs[3 * n_src + 8:-2]
    scr, ubuf = refs[-2:]

    def normmod(h):
        return (_rms(h) * g_ref[...]) * (1.0 + sc_ref[...]) + sh_ref[...]

    ubuf[0:TM, :] = normmod(_load_rows(mains, ctx_tiles, 0)).astype(BF16)
    prev_ok, nxt_ok = _halo_flags(pl.program_id(1), nt, seg_starts)
    up = jnp.where(prev_ok, normmod(_load_rows(prevs, ctx_tiles, 0)), 0.0)
    un = jnp.where(nxt_ok, normmod(_load_rows(nexts, ctx_tiles, 0)), 0.0)
    ubuf[TM:TM + 2 * HALO, :] = jnp.concatenate([up, un], axis=0).astype(BF16)
    cos = cos_ref[...]
    sin = sin_ref[...]
    lane = lax.broadcasted_iota(jnp.int32, cos.shape, 1)
    lo = (lane & 31) < 16
    c_start, c_width = groups[conv_idx][:2]
    c_out = out_refs[conv_idx]
    taps = cw_ref.shape[0]
    for a in range(0, c_width, MXU_COLS):
        proj = _dot(ubuf[...], w_ref[:, c_start + a:c_start + a + MXU_COLS])
        scr[0:HALO, a:a + MXU_COLS] = proj[TM:TM + HALO]
        scr[HALO:HALO + TM, a:a + MXU_COLS] = proj[0:TM]
        scr[HALO + TM:, a:a + MXU_COLS] = proj[TM + HALO:]

    def conv_piece(a):
        cols = slice(a, a + MXU_COLS)
        rows = scr[:, cols]
        nrow = rows.shape[0]

        def tap(j):
            off = j - taps // 2
            shifted = rows if off == 0 else pltpu.roll(rows, (-off) % nrow, 0)
            return shifted[HALO:HALO + TM] * cw_ref[j:j + 1, cols]

        y = tap(0)
        for j in range(1, taps):
            y = y + tap(j)
        c_out[:, cols] = _silu(y + cb_ref[:, cols]).astype(c_out.dtype)

    def project_piece(gi, a, step):
        start, _, rope_cols, qscale = groups[gi]
        y = _dot(ubuf[0:TM, :], w_ref[:, start + a:start + a + step])
        for r in range(0, step, LANES):
            yr = y[:, r:r + LANES]
            if a + r < rope_cols:
                yr = _rope_block(yr, cos, sin, lo)
            if qscale != 1.0:
                yr = yr * qscale
            out_refs[gi][:, a + r:a + r + LANES] = yr.astype(out_refs[gi].dtype)

    conv_pieces = [functools.partial(conv_piece, a) for a in range(0, c_width, MXU_COLS)]
    proj_pieces = []
    for gi, (_, width, rope_cols, _) in enumerate(groups):
        if gi != conv_idx:
            step = min(width, MXU_COLS if rope_cols else 2 * MXU_COLS)
            proj_pieces += [functools.partial(project_piece, gi, a, step) for a in range(0, width, step)]
    for i in range(max(len(conv_pieces), len(proj_pieces))):
        if i < len(proj_pieces):
            proj_pieces[i]()
        if i < len(conv_pieces):
            conv_pieces[i]()


def _seg_map(ctx_tiles, off, bt=_grid_bt):
    def index_map(*g):
        b, t = bt(*g)
        return b, jnp.where(t + off >= ctx_tiles, 1, 0), 0, 0
    return index_map


def _inproj_call(hs, g, shift, scale, cos, sin, w, groups, dtypes, ctx_tiles, conv_idx, conv_w, conv_b):
    bsz, _, d = hs[0].shape
    s = sum(h_.shape[1] for h_ in hs)
    nt = s // TM
    seg = _seg_map(ctx_tiles, 0)
    const = lambda arr: pl.BlockSpec(arr.shape, lambda b, t: (0, 0))
    in_specs = _row_specs(hs, ctx_tiles, 0) + _halo_row_specs(hs, ctx_tiles) + [
                pl.BlockSpec((1, d), lambda b, t: (0, 0)),
                pl.BlockSpec((None, None, 1, d), seg),
                pl.BlockSpec((None, None, 1, d), seg),
                pl.BlockSpec((TM, LANES), lambda b, t: (t, 0)),
                pl.BlockSpec((TM, LANES), lambda b, t: (t, 0)),
                const(w), const(conv_w), const(conv_b)]
    out_specs = [pl.BlockSpec((None, TM, gr[1]), lambda b, t: (b, t, 0)) for gr in groups]
    out_shape = [jax.ShapeDtypeStruct((bsz, s, gr[1]), dt) for gr, dt in zip(groups, dtypes)]
    body = functools.partial(_inproj_body, groups=groups, conv_idx=conv_idx, n_src=len(hs), ctx_tiles=ctx_tiles,
                             nt=nt, seg_starts=(0, ctx_tiles))
    return pl.pallas_call(
        body, grid=(bsz, nt), in_specs=in_specs, out_specs=out_specs, out_shape=out_shape,
        scratch_shapes=[pltpu.VMEM((TM + 2 * HALO, groups[conv_idx][1]), F32),
                        pltpu.VMEM((TM + 2 * HALO, d), BF16)],
        compiler_params=_cparams("parallel", "arbitrary"),
        name="in_proj",
    )(*hs, *hs, *hs, g, shift, scale, cos, sin, w, conv_w, conv_b)


def _halo_flags(t, nt, seg_starts):
    prev_ok = t >= 0
    nxt_ok = (t + 1) < nt
    for s0 in seg_starts:
        prev_ok = jnp.logical_and(prev_ok, t != s0)
        nxt_ok = jnp.logical_and(nxt_ok, (t + 1) != s0)
    return prev_ok, nxt_ok


def _max_half_norm2():
    r_i = lax.broadcasted_iota(jnp.int32, (LANES, LANES), 0)
    c_i = lax.broadcasted_iota(jnp.int32, (LANES, LANES), 1)
    ind = jnp.where(c_i == r_i // HEAD_DIM, 1.0, 0.0).astype(BF16)

    def max_norm2(x):
        xf = x.astype(F32)
        return jnp.max(_dot((xf * xf).astype(BF16), ind), axis=0, keepdims=True)
    return max_norm2


def _win_attn_body(sink_ref, bias_ref, q_ref, kvp_ref, kvc_ref, kvn_ref, kvx_ref, o_ref):
    max_norm2 = _max_half_norm2()
    qn2 = max_norm2(q_ref[:, 0:LANES])
    for g in range(1, A_GROUP):
        qn2 = jnp.maximum(qn2, max_norm2(q_ref[:, g * LANES:(g + 1) * LANES]))
    kn2 = max_norm2(kvc_ref[:, 0:LANES])
    for r in (kvp_ref, kvn_ref, kvx_ref):
        kn2 = jnp.maximum(kn2, max_norm2(r[:, 0:LANES]))
    worst = jnp.max(qn2 * kn2, axis=1, keepdims=True)[0, 0] * NORM_SLACK
    sink_max = jnp.abs(sink_ref[0])
    for i in range(1, A_KV_HEADS * A_GROUP):
        sink_max = jnp.maximum(sink_max, jnp.abs(sink_ref[i]))
    small = jnp.logical_and(worst < SAFE_LOGIT * SAFE_LOGIT, sink_max * LOG2E < SAFE_LOGIT)
    refs = (sink_ref, bias_ref, q_ref, kvp_ref, kvc_ref, kvn_ref, kvx_ref, o_ref)

    @pl.when(small)
    def _():
        _win_attn_heads(*refs, stabilise=False)

    @pl.when(jnp.logical_not(small))
    def _():
        _win_attn_heads(*refs, stabilise=True)


def _win_attn_heads(sink_ref, bias_ref, q_ref, kvp_ref, kvc_ref, kvn_ref, kvx_ref, o_ref, stabilise):
    blk = q_ref.shape[0]
    kvcat = jnp.concatenate([kvp_ref[...], kvc_ref[...], kvn_ref[...], kvx_ref[...]], axis=0)
    kcat = kvcat[:, :LANES]
    vcat = kvcat[:, LANES:]
    nkeys = kvcat.shape[0]
    rows = A_GROUP * blk
    bias = bias_ref[...]
    lane = lax.broadcasted_iota(jnp.int32, (blk, LANES), 1)
    rowg = lax.broadcasted_iota(jnp.int32, (rows, 1), 0) // blk
    q = q_ref[...]

    def logits(h):
        half = (lane >= h * HEAD_DIM) & (lane < (h + 1) * HEAD_DIM)
        qh = jnp.concatenate(
            [jnp.where(half, q[:, g * LANES:(g + 1) * LANES], jnp.zeros((), BF16)) for g in range(A_GROUP)], axis=0)
        return _dot_nt(qh, kcat)

    raw = [logits(h) for h in range(A_KV_HEADS)]
    outs = []
    for h in range(A_KV_HEADS):
        s = (raw[h].reshape(A_GROUP, blk, nkeys) + bias[None]).reshape(rows, nkeys)
        sink = jnp.zeros((rows, 1), F32)
        for g in range(A_GROUP):
            sink = jnp.where(rowg == g, sink_ref[h * A_GROUP + g] * LOG2E, sink)
        if stabilise:
            m = jnp.maximum(jnp.max(s, axis=1, keepdims=True), sink)
            p = jnp.exp2(s - m)
            den = jnp.sum(p, axis=1, keepdims=True) + jnp.exp2(sink - m)
        else:
            p = jnp.exp2(s)
            den = jnp.sum(p, axis=1, keepdims=True) + jnp.exp2(sink)
        outs.append(_dot(p.astype(BF16), vcat) / den)
    for g in range(A_GROUP):
        o_ref[:, g * LANES:(g + 1) * LANES] = jnp.where(
            lane < HEAD_DIM, outs[0][g * blk:(g + 1) * blk], outs[1][g * blk:(g + 1) * blk]).astype(o_ref.dtype)


def _win_attn_bias(ctx_len):
    qi = jnp.arange(WQ)[:, None]
    kj = jnp.arange(WQ + 2 * A_WINDOW)[None, :]
    band = jnp.abs(kj - A_WINDOW - qi) <= A_WINDOW
    variants = [jnp.zeros_like(band), band & (kj >= A_WINDOW), band, band & (kj < A_WINDOW + WQ)]
    win = jnp.stack([jnp.where(v_, 0.0, NEG) for v_ in variants]).astype(F32)
    return jnp.concatenate([win, jnp.zeros((4, WQ, ctx_len), F32)], axis=2)


def _win_attn_call(q, kv, sink, ctx_len):
    bsz, s, qw = q.shape
    kw = kv.shape[-1]
    nq = s // WQ
    cq = ctx_len // WQ
    per = WQ // A_WINDOW
    nblk = s // A_WINDOW
    assert ctx_len % WQ == 0 and nq - cq >= 2
    bias = _win_attn_bias(ctx_len)
    clamp = lambda j: jnp.clip(j, cq * per, nblk - 1)
    variant = lambda i: jnp.where(i < cq, 0, jnp.where(i == cq, 1, jnp.where(i == nq - 1, 3, 2)))
    kv_specs = [pl.BlockSpec((None, A_WINDOW, kw), lambda b, i: (b, clamp(i * per - 1), 0)),
                pl.BlockSpec((None, WQ, kw), lambda b, i: (b, i, 0)),
                pl.BlockSpec((None, A_WINDOW, kw), lambda b, i: (b, clamp((i + 1) * per), 0)),
                pl.BlockSpec((None, ctx_len, kw), lambda b, i: (b, 0, 0))]
    return pl.pallas_call(
        _win_attn_body,
        grid=(bsz, nq),
        in_specs=[pl.BlockSpec(memory_space=pltpu.SMEM),
                  pl.BlockSpec((None,) + bias.shape[1:], lambda b, i: (variant(i), 0, 0)),
                  pl.BlockSpec((None, WQ, qw), lambda b, i: (b, i, 0))] + kv_specs,
        out_specs=pl.BlockSpec((None, WQ, qw), lambda b, i: (b, i, 0)),
        out_shape=jax.ShapeDtypeStruct((bsz, s, qw), BF16),
        compiler_params=_cparams("parallel", "arbitrary"),
        name="window_attention",
    )(sink, bias, q, kv, kv, kv, kv)


def _bwd_chunk(j, ctx_chunks, n_chunks):
    return jnp.where(j < ctx_chunks, ctx_chunks - 1 - j, n_chunks + ctx_chunks - 1 - j)


def _scan_rows(dirs):
    return [(bb, d, tuple(r.at[bb] for r in refs)) for bb in range(SCAN_ROWS) for d, refs in enumerate(dirs)]


def _tri_masks(d):
    row = lax.broadcasted_iota(jnp.int32, (CHUNK, CHUNK), 0)
    col = lax.broadcasted_iota(jnp.int32, (CHUNK, CHUNK), 1)
    keep = (col <= row) if d == 0 else (col >= row)
    return keep, jnp.where(keep, 1.0, 0.0).astype(BF16)


def _mlstm_body(qkf_ref, vf_ref, gf_ref, qkb_ref, vb_ref, gb_ref, bias_ref, hf_ref, hb_ref, c_scr, m_scr):
    L = CHUNK
    dk = HEAD_DIM
    dv = LANES

    @pl.when(pl.program_id(1) == 0)
    def _():
        c_scr[...] = jnp.zeros_like(c_scr)
        m_scr[...] = jnp.zeros_like(m_scr)

    lane = lax.broadcasted_iota(jnp.int32, (L, LANES), 1)
    srow = lax.broadcasted_iota(jnp.int32, (LANES, 2 * dv), 0)
    ones_v = jnp.ones((L, dv), BF16)
    dirs = ((qkf_ref, vf_ref, gf_ref, hf_ref), (qkb_ref, vb_ref, gb_ref, hb_ref))
    for bb, d, (qk_ref, v_ref, g_ref, o_ref) in _scan_rows(dirs):
        keep, tri = _tri_masks(d)
        gates = g_ref[...] + bias_ref[...]
        gates_t = gates.T[0:16, :]
        b_col = _cumsum_cols(tri, _log_sigmoid(gates))
        b_row = _cumsum_rows(_log_sigmoid(gates_t), tri)
        k_pairs = [qk_ref[:, B_HEADS * dk + p * LANES:B_HEADS * dk + (p + 1) * LANES] for p in range(B_HEADS // 2)]
        kt_pairs = [kp.astype(F32).T.astype(BF16) for kp in k_pairs]
        for h in range(B_HEADS):
            ci = d * B_HEADS + h
            cf = 2 * B_HEADS + ci
            r = bb * 2 * B_HEADS + ci
            li_c = jnp.broadcast_to(gates[:, ci:ci + 1], (L, LANES))
            li_r = gates_t[ci:ci + 1, :]
            bc = jnp.broadcast_to(b_col[:, cf:cf + 1], (L, LANES))
            br = b_row[cf:cf + 1, :]
            b_last = bc[L - 1:L, :] if d == 0 else bc[0:1, :]
            m_prev = m_scr[r:r + 1, :]
            pair = (h // 2) * LANES
            half = (lane >= (h % 2) * dk) & (lane < (h % 2 + 1) * dk)
            q = jnp.where(half, qk_ref[:, pair:pair + LANES] * (dk ** -0.5), 0.0).astype(BF16)
            k = k_pairs[h // 2].astype(BF16)
            v = v_ref[:, h * dv:(h + 1) * dv]
            dm = jnp.where(keep, bc - br + li_r, NEG)
            g_in = bc + m_prev
            m_t = jnp.maximum(jnp.broadcast_to(jnp.max(dm, axis=1, keepdims=True), (L, LANES)), g_in)
            sm = (_dot_nt(q, k) * jnp.exp(dm - m_t)).astype(BF16)
            vaug = jnp.concatenate([v, ones_v], axis=1)
            c_prev = c_scr[r]
            intra = _dot(sm, vaug)
            inter = _dot(q, c_prev.astype(BF16))
            a_in = jnp.exp(g_in - m_t)
            den = jnp.maximum(jnp.abs(intra[:, dv:] + a_in * inter[:, dv:]), jnp.exp(-m_t))
            o_ref[:, h * dv:(h + 1) * dv] = ((intra[:, :dv] + a_in * inter[:, :dv]) / den).astype(o_ref.dtype)
            w_c = b_last - bc + li_c
            m_loc = jnp.max(w_c, axis=0, keepdims=True)
            e = jnp.exp(w_c - m_loc)
            ev = jnp.concatenate([e * v.astype(F32), e], axis=1).astype(BF16)
            c_loc = _dot(kt_pairs[h // 2], ev)
            own = (srow >= (h % 2) * dk) & (srow < (h % 2 + 1) * dk)
            m_new = jnp.maximum(b_last + m_prev, m_loc)
            keep_old = jnp.exp(b_last + m_prev - m_new)
            take_new = jnp.exp(m_loc - m_new)
            c_scr[r] = (jnp.concatenate([keep_old, keep_old], axis=1) * c_prev
                        + jnp.where(own, jnp.concatenate([take_new, take_new], axis=1) * c_loc, 0.0))
            m_scr[r:r + 1, :] = m_new


def _scan_specs(width, ctx_chunks, n_chunks):
    fwd = pl.BlockSpec((SCAN_ROWS, CHUNK, width), lambda b, j: (b, j, 0))
    bwd = pl.BlockSpec((SCAN_ROWS, CHUNK, width), lambda b, j: (b, _bwd_chunk(j, ctx_chunks, n_chunks), 0))
    return fwd, bwd


def _mlstm_call(qk, v, gates, bias, ctx_len):
    bsz, s, w = v.shape
    nc = s // CHUNK
    cc = ctx_len // CHUNK
    qf, qb = _scan_specs(qk.shape[-1], cc, nc)
    vf, vb = _scan_specs(w, cc, nc)
    gf, gb = _scan_specs(LANES, cc, nc)
    return pl.pallas_call(
        _mlstm_body,
        grid=(bsz // SCAN_ROWS, nc),
        in_specs=[qf, vf, gf, qb, vb, gb, pl.BlockSpec((1, LANES), lambda b, j: (0, 0))],
        out_specs=[vf, vb],
        out_shape=[jax.ShapeDtypeStruct((bsz, s, w), BF16)] * 2,
        scratch_shapes=[pltpu.VMEM((SCAN_ROWS * 2 * B_HEADS, LANES, 2 * LANES), F32),
                        pltpu.VMEM((SCAN_ROWS * 2 * B_HEADS, LANES), F32)],
        compiler_params=_cparams("parallel", "arbitrary"),
        name="mlstm_scan",
    )(qk, v, gates, qk, v, gates, bias)


def _ssd_body(xf_ref, dtf_ref, xb_ref, dtb_ref, dtbias_ref, nega_ref, skip_ref, yf_ref, yb_ref, s_scr):
    L = CHUNK
    hd = HEAD_DIM
    inner = 4 * D_GROUPS * hd
    hpg = 4

    @pl.when(pl.program_id(1) == 0)
    def _():
        s_scr[...] = jnp.zeros_like(s_scr)

    lane = lax.broadcasted_iota(jnp.int32, (L, LANES), 1)
    lo = lane < hd
    lo1 = lo[0:1, :]
    dirs = ((xf_ref, dtf_ref, yf_ref), (xb_ref, dtb_ref, yb_ref))
    for bb, d, (x_ref, dt_ref, y_ref) in _scan_rows(dirs):
        keep, tri = _tri_masks(d)
        dt = _softplus(dt_ref[...] + dtbias_ref[...])
        a = dt * nega_ref[...]
        dt_t = dt.T[0:16, :]
        a_t = a.T[0:16, :]
        ac_col = _cumsum_cols(tri, a)
        ac_row = _cumsum_rows(a_t, tri)
        for g in range(D_GROUPS):
            bg_f = x_ref[:, inner + g * LANES:inner + (g + 1) * LANES]
            bg = bg_f.astype(BF16)
            bg_t = bg_f.T.astype(BF16)
            cg = x_ref[:, inner + (D_GROUPS + g) * LANES:inner + (D_GROUPS + g + 1) * LANES].astype(BF16)
            cb = _dot_nt(cg, bg)
            for pr in range(2):
                c0 = g * hpg * hd + pr * LANES
                xp = x_ref[:, c0:c0 + LANES]
                xpb = xp.astype(BF16)
                ys, eas, wsts, als = [], [], [], []
                for hh in range(2):
                    col = d * D_GROUPS * hpg + g * hpg + pr * 2 + hh
                    acc = jnp.broadcast_to(ac_col[:, col:col + 1], (L, LANES))
                    acr = ac_row[col:col + 1, :]
                    seg = jnp.where(keep, acc - acr, NEG)
                    mix = (cb * jnp.exp(seg) * dt_t[col:col + 1, :]).astype(BF16)
                    ys.append(_dot(mix, xpb))
                    a_last = acc[L - 1:L, :] if d == 0 else acc[0:1, :]
                    eas.append(jnp.exp(acc))
                    wsts.append(jnp.exp(a_last - acc) * jnp.broadcast_to(dt[:, col:col + 1], (L, LANES)))
                    als.append(jnp.exp(a_last))
                sidx = (bb * 2 + d) * 2 * D_GROUPS + g * 2 + pr
                st = s_scr[sidx]
                y = jnp.where(lo, ys[0], ys[1]) + _dot(cg, st.astype(BF16)) * jnp.where(lo, eas[0], eas[1])
                if d == 0:
                    y = y + skip_ref[:, c0:c0 + LANES] * xp
                y_ref[:, c0:c0 + LANES] = y.astype(y_ref.dtype)
                xw = (xp * jnp.where(lo, wsts[0], wsts[1])).astype(BF16)
                s_scr[sidx] = jnp.where(lo1, als[0], als[1]) * st + _dot(bg_t, xw)


def _ssd_call(xbc, dt, dt_bias, neg_a, skip, ctx_len):
    bsz, s, w = xbc.shape
    nc = s // CHUNK
    cc = ctx_len // CHUNK
    inner = skip.shape[-1]
    xf, xb = _scan_specs(w, cc, nc)
    df, db = _scan_specs(LANES, cc, nc)
    yf, yb = _scan_specs(inner, cc, nc)
    vec = lambda n: pl.BlockSpec((1, n), lambda b, j: (0, 0))
    return pl.pallas_call(
        _ssd_body,
        grid=(bsz // SCAN_ROWS, nc),
        in_specs=[xf, df, xb, db, vec(LANES), vec(LANES), vec(inner)],
        out_specs=[yf, yb],
        out_shape=[jax.ShapeDtypeStruct((bsz, s, inner), BF16)] * 2,
        scratch_shapes=[pltpu.VMEM((SCAN_ROWS * 4 * D_GROUPS, LANES, LANES), F32)],
        compiler_params=_cparams("parallel", "arbitrary"),
        name="ssd_scan",
    )(xbc, dt, xbc, dt, dt_bias, neg_a, skip)


def _diff_attn_body(lam_ref, q_ref, k_ref, v_ref, o_ref, kn_scr, *, lam_init):
    nheads = q_ref.shape[1] // LANES
    max_norm2 = _max_half_norm2()

    @pl.when(pl.program_id(1) == 0)
    def _():
        for h in range(nheads):
            kn_scr[h:h + 1, :] = max_norm2(k_ref[:, h * LANES:(h + 1) * LANES])

    bound2 = max_norm2(q_ref[:, 0:LANES]) * kn_scr[0:1, :]
    for h in range(1, nheads):
        bound2 = jnp.maximum(bound2, max_norm2(q_ref[:, h * LANES:(h + 1) * LANES]) * kn_scr[h:h + 1, :])
    worst = jnp.max(bound2, axis=1, keepdims=True)[0, 0] * NORM_SLACK
    small = worst < SAFE_LOGIT * SAFE_LOGIT

    @pl.when(small)
    def _():
        _diff_attn_heads(lam_ref, q_ref, k_ref, v_ref, o_ref, lam_init, stabilise=False)

    @pl.when(jnp.logical_not(small))
    def _():
        _diff_attn_heads(lam_ref, q_ref, k_ref, v_ref, o_ref, lam_init, stabilise=True)


def _diff_attn_heads(lam_ref, q_ref, k_ref, v_ref, o_ref, lam_init, stabilise):
    lv = lam_ref[...]
    lam = (jnp.exp(jnp.sum(lv[0:1] * lv[1:2], axis=1, keepdims=True))
           - jnp.exp(jnp.sum(lv[2:3] * lv[3:4], axis=1, keepdims=True)) + lam_init)
    tq = q_ref.shape[0]
    lane = lax.broadcasted_iota(jnp.int32, (tq, LANES), 1)
    nheads = q_ref.shape[1] // LANES

    def logits(h):
        q = q_ref[:, h * LANES:(h + 1) * LANES]
        k = k_ref[:, h * LANES:(h + 1) * LANES]
        return [_dot_nt(jnp.where((lane >= m * HEAD_DIM) & (lane < (m + 1) * HEAD_DIM), q, jnp.zeros((), BF16)), k)
                for m in range(2)]

    s_next = logits(0)
    for h in range(nheads):
        s_cur = s_next
        if h + 1 < nheads:
            s_next = logits(h + 1)
        ps, ls = [], []
        for s in s_cur:
            p = jnp.exp2(s - jnp.max(s, axis=1, keepdims=True)) if stabilise else jnp.exp2(s)
            ps.append(p)
            ls.append(jnp.sum(p, axis=1, keepdims=True))
        a = (ps[0] - (lam * ls[0] / ls[1]) * ps[1]).astype(BF16)
        o_ref[:, h * LANES:(h + 1) * LANES] = _dot(a, v_ref[:, h * LANES:(h + 1) * LANES]) / ls[0]


def _diff_attn_call(q, k, v, lam_vecs, lam_init, ctx_len):
    bsz, s, w = q.shape
    t = s - ctx_len
    off = ctx_len // TQ
    return pl.pallas_call(
        functools.partial(_diff_attn_body, lam_init=lam_init),
        grid=(bsz, t // TQ),
        in_specs=[pl.BlockSpec(lam_vecs.shape, lambda b, i: (0, 0)),
                  pl.BlockSpec((None, TQ, w), lambda b, i: (b, i + off, 0)),
                  pl.BlockSpec((None, s, w), lambda b, i: (b, 0, 0)),
                  pl.BlockSpec((None, s, w), lambda b, i: (b, 0, 0))],
        out_specs=pl.BlockSpec((None, TQ, w), lambda b, i: (b, i, 0)),
        out_shape=jax.ShapeDtypeStruct((bsz, t, w), F32),
        scratch_shapes=[pltpu.VMEM((HALO, LANES), F32)],
        compiler_params=_cparams("parallel", "arbitrary"),
        name="diff_attention",
    )(lam_vecs, q, k, v)


def _group_rms(x, width):
    return jnp.concatenate([_rms(x[:, a:a + width]) for a in range(0, x.shape[1], width)], axis=1)


def _outproj_even_body(ya_ref, hf_ref, hb_ref, og_ref, ng_ref, w_ref, g_ref, out_ref):
    half = ya_ref.shape[1]
    f32 = lambda r: r[...].astype(F32)
    hn = _group_rms(f32(hf_ref) + f32(hb_ref), LANES) * ng_ref[...] * _sigmoid(f32(og_ref))
    y = _dot(ya_ref[...], w_ref[0:half, :]) + _dot(hn.astype(BF16), w_ref[half:, :])
    out_ref[...] = (_rms(y) * g_ref[...]).astype(out_ref.dtype)


def _outproj_odd_body(oa_ref, yf_ref, yb_ref, z_ref, cg_ref, dg_ref, w_ref, g_ref, out_ref, *, out_scale):
    half = oa_ref.shape[1]
    on = _group_rms(oa_ref[...], LANES) * cg_ref[...] * out_scale
    f32 = lambda r: r[...].astype(F32)
    yz = (f32(yf_ref) + f32(yb_ref)) * _silu(f32(z_ref))
    sn = _group_rms(yz, half // D_GROUPS) * dg_ref[...]
    y = _dot(on.astype(BF16), w_ref[0:half, :]) + _dot(sn.astype(BF16), w_ref[half:, :])
    out_ref[...] = (_rms(y) * g_ref[...]).astype(out_ref.dtype)


def _outproj_call(body, acts, act_offs, vecs, w, g, n_t):
    bsz = acts[0].shape[0]
    d = w.shape[1]

    def act_spec(arr, off):
        return pl.BlockSpec((None, TM, arr.shape[-1]), lambda b, t: (b, t + off, 0))

    in_specs = ([act_spec(a_, o_) for a_, o_ in zip(acts, act_offs)]
                + [pl.BlockSpec(v_.shape, lambda b, t: (0, 0)) for v_ in vecs]
                + [pl.BlockSpec(w.shape, lambda b, t: (0, 0)),
                   pl.BlockSpec((1, d), lambda b, t: (0, 0))])
    return pl.pallas_call(
        body, grid=(bsz, n_t), in_specs=in_specs,
        out_specs=pl.BlockSpec((None, TM, d), lambda b, t: (b, t, 0)),
        out_shape=jax.ShapeDtypeStruct((bsz, n_t * TM, d), BF16),
        compiler_params=_cparams("parallel", "arbitrary"),
        name="out_proj",
    )(*acts, *vecs, w, g)


def _ffn_body(*refs, n_src, ctx_tiles, t_off, nt, seg_starts):
    mains, prevs, nexts = refs[:n_src], refs[n_src:2 * n_src], refs[2 * n_src:3 * n_src]
    (ym_ref, yp_ref, yn_ref, gate1_ref, g2_ref, sh_ref, sc_ref, wg_ref, wu_ref, cw_ref, wd_ref, g3_ref, gate_ref,
     out_ref, gscr, uscr) = refs[3 * n_src:]
    t = pl.program_id(1)
    prev_ok, nxt_ok = _halo_flags(t, nt, seg_starts)

    def resid(h_refs, y):
        return _load_rows(h_refs, ctx_tiles, t_off) + gate1_ref[...] * y

    def normmod(h):
        return (_rms(h) * g2_ref[...]) * (1.0 + sc_ref[...]) + sh_ref[...]

    hm = resid(mains, ym_ref[...].astype(F32))
    um = normmod(hm)
    up = jnp.where(prev_ok, normmod(resid(prevs, yp_ref[...].astype(F32)[Y_HALO - HALO:, :])), 0.0)
    un = jnp.where(nxt_ok, normmod(resid(nexts, yn_ref[...].astype(F32)[:HALO, :])), 0.0)
    ucat = jnp.concatenate([up, um, un], axis=0).astype(BF16)
    umb = um.astype(BF16)

    nchunk = wg_ref.shape[1] // FFN_CHUNK
    cols = lambda c: slice(c * FFN_CHUNK, (c + 1) * FFN_CHUNK)
    nrow = TM + 2 * HALO

    def front(c):
        gscr[c % 2] = _dot(ucat, wg_ref[:, cols(c)])
        uscr[c % 2] = _dot(umb, wu_ref[:, cols(c)])

    front(0)
    y = None
    for c in range(nchunk):
        if c + 1 < nchunk:
            front(c + 1)
        g = gscr[c % 2]
        gc = (pltpu.roll(g, 1, 0)[HALO:HALO + TM] * cw_ref[0:1, cols(c)] + g[HALO:HALO + TM] * cw_ref[1:2, cols(c)]
              + pltpu.roll(g, nrow - 1, 0)[HALO:HALO + TM] * cw_ref[2:3, cols(c)])
        hid = (_silu(gc) * uscr[c % 2]).astype(BF16)
        if c % 2 == 0 and c + 1 < nchunk:
            held = hid
            continue
        if c % 2 == 1:
            part = _dot(jnp.concatenate([held, hid], axis=1), wd_ref[(c - 1) * FFN_CHUNK:(c + 1) * FFN_CHUNK, :])
        else:
            part = _dot(hid, wd_ref[cols(c), :])
        y = part if y is None else y + part
    out_ref[...] = hm + gate_ref[...] * (_rms(y) * g3_ref[...])


def _ffn_call(hs, yn, gate1, g2, shift, scale, wg, wu, cw, wd, g3, gate2, ctx_tiles, t_off, seg_starts):
    bsz, rows, d = yn.shape
    n_t = rows // TM
    per = TM // Y_HALO
    last = rows // Y_HALO - 1
    const = lambda arr: pl.BlockSpec(arr.shape, lambda b, t: (0, 0))
    vec = pl.BlockSpec((1, d), lambda b, t: (0, 0))
    mod = pl.BlockSpec((None, None, 1, d), _seg_map(ctx_tiles, t_off))
    y_specs = [pl.BlockSpec((None, TM, d), lambda b, t: (b, t, 0)),
               pl.BlockSpec((None, Y_HALO, d), lambda b, t: (b, jnp.maximum(t * per - 1, 0), 0)),
               pl.BlockSpec((None, Y_HALO, d), lambda b, t: (b, jnp.minimum((t + 1) * per, last), 0))]
    body = functools.partial(_ffn_body, n_src=len(hs), ctx_tiles=ctx_tiles, t_off=t_off, nt=n_t, seg_starts=seg_starts)
    return pl.pallas_call(
        body, grid=(bsz, n_t),
        in_specs=(_row_specs(hs, ctx_tiles, t_off) + _halo_row_specs(hs, ctx_tiles, t_off) + y_specs
                  + [mod, vec, mod, mod, const(wg), const(wu), const(cw), const(wd), vec, mod]),
        out_specs=pl.BlockSpec((None, TM, d), lambda b, t: (b, t, 0)),
        out_shape=jax.ShapeDtypeStruct((bsz, rows, d), F32),
        scratch_shapes=[pltpu.VMEM((2, TM + 2 * HALO, FFN_CHUNK), F32), pltpu.VMEM((2, TM, FFN_CHUNK), F32)],
        compiler_params=_cparams("parallel", "arbitrary"),
        name="conv_ffn",
    )(*hs, *hs, *hs, yn, yn, yn, gate1, g2, shift, scale, wg, wu, cw, wd, g3, gate2)


def _rope_tables(ctx_len, t):
    pos = jnp.arange(t)
    row = (pos // GRID_W).astype(F32)
    col = (pos % GRID_W).astype(F32)
    nf = HEAD_DIM // 4
    inv = ROPE_BASE ** (-jnp.arange(nf, dtype=F32) / nf)
    ar = row[:, None] * inv
    ac = col[:, None] * inv
    cos = jnp.concatenate([jnp.cos(ar), jnp.cos(ar), jnp.cos(ac), jnp.cos(ac)], axis=1)
    sin = jnp.concatenate([-jnp.sin(ar), jnp.sin(ar), -jnp.sin(ac), jnp.sin(ac)], axis=1)
    cos = jnp.concatenate([jnp.ones((ctx_len, HEAD_DIM), F32), cos], axis=0)
    sin = jnp.concatenate([jnp.zeros((ctx_len, HEAD_DIM), F32), sin], axis=0)
    return jnp.tile(cos, (1, LANES // HEAD_DIM)), jnp.tile(sin, (1, LANES // HEAD_DIM))


def _pad_cols(w, n):
    return jnp.pad(w, ((0, 0), (0, n - w.shape[1])))


def _gqa_perm():
    idx = [(h * A_GROUP + g) * HEAD_DIM + dd
           for g in range(A_GROUP) for h in range(A_KV_HEADS) for dd in range(HEAD_DIM)]
    return jnp.array(idx, dtype=jnp.int32)


def _ffn_weights(w_gate, w_up, conv, w_down):
    return w_gate.astype(BF16), w_up.astype(BF16), conv, w_down.astype(BF16)


def kernel(x, c, ctx, c_ctx, mod_w, mod_b, norm_g, ffn_w_gate, ffn_w_up, ffn_conv, ffn_w_down, ev_w_in, ev_w_out, a_sink, b_conv, b_gate_b, b_norm_g, od_w_in, od_w_out, c_lambda, c_norm_g, d_conv, d_conv_b, d_dt_bias, d_a_log, d_skip, d_norm_g):
    bsz, t, d = x.shape
    ctx_len = ctx.shape[1]
    depth = mod_w.shape[0]
    s = ctx_len + t
    half = d // 2
    assert ctx_len % TM == 0 and t % TM == 0 and t % GRID_W == 0 and ffn_w_gate.shape[-1] % FFN_CHUNK == 0
    assert bsz % SCAN_ROWS == 0
    n_t = s // TM
    ctx_tiles = ctx_len // TM

    mod_rows = -(-(bsz + 1) // HALO) * HALO
    cc = jnp.concatenate([c, c_ctx[None, :], jnp.zeros((mod_rows - bsz - 1, d), F32)], axis=0)
    mods = _mod_call(cc, mod_w, mod_b).reshape(depth, mod_rows, 6, d)

    def layer_mods(l):
        lat = mods[l, :bsz]
        cx = jnp.broadcast_to(mods[l, bsz][None], (bsz, 6, d))
        m = jnp.stack([cx, lat], axis=1)
        return [m[:, :, i][:, :, None, :] for i in range(6)]

    cos, sin = _rope_tables(ctx_len, t)
    hs = (ctx, x)

    for l in range(depth):
        last = l == depth - 1
        j = l // 2
        m = layer_mods(l)
        g = norm_g[l]
        wg, wu, cw, wd = _ffn_weights(ffn_w_gate[l], ffn_w_up[l], ffn_conv[l], ffn_w_down[l])
        if l % 2 == 0:
            w_in = ev_w_in[j]
            perm = _gqa_perm()
            akv = A_KV_HEADS * HEAD_DIM
            o1 = half + 2 * akv
            w_cat = jnp.concatenate([w_in[:, :half][:, perm], w_in[:, half:o1 + 3 * half],
                                     _pad_cols(w_in[:, o1 + 3 * half:], LANES)], axis=1).astype(BF16)
            groups = ((0, half, half, HEAD_DIM ** -0.5 * LOG2E), (half, 2 * akv, akv, 1.0),
                      (o1, half, 0, 1.0), (o1 + half, half, 0, 1.0), (o1 + 2 * half, half, 0, 1.0),
                      (o1 + 3 * half, LANES, 0, 1.0))
            q, kv, qkc, mv, mo, gates = _inproj_call(
                hs, g[0:1], m[0], m[1], cos, sin, w_cat, groups, (BF16, BF16, BF16, BF16, BF16, F32), ctx_tiles,
                2, b_conv[j], jnp.zeros((1, half), F32))
            ya = _win_attn_call(q, kv, a_sink[j], ctx_len)
            hf, hb = _mlstm_call(qkc, mv, gates, _pad_cols(b_gate_b[j][None, :], LANES), ctx_len)
            w_out = ev_w_out[j]
            w_out = jnp.concatenate([w_out[:half][perm], w_out[half:]], axis=0).astype(BF16)
            acts, vecs, body = (ya, hf, hb, mo), (b_norm_g[j][None, :],), _outproj_even_body
        else:
            w_in = od_w_in[j]
            xbc_w = 2 * half
            w_cat = jnp.concatenate([w_in[:, :4 * half + xbc_w],
                                     _pad_cols(w_in[:, 4 * half + xbc_w:], LANES)], axis=1).astype(BF16)
            groups = ((0, half, half, HEAD_DIM ** -0.5 * LOG2E), (half, half, half, 1.0), (2 * half, half, 0, 1.0),
                      (3 * half, half, 0, 1.0), (4 * half, xbc_w, 0, 1.0), (4 * half + xbc_w, LANES, 0, 1.0))
            q, k, v, z, xbcc, dt = _inproj_call(
                hs, g[0:1], m[0], m[1], cos, sin, w_cat, groups, (BF16, BF16, BF16, BF16, F32, F32), ctx_tiles,
                4, d_conv[j], d_conv_b[j][None, :])
            lam_init = 0.8 - 0.6 * math.exp(-LAM_DEPTH_RATE * l)
            oa = _diff_attn_call(q, k, v, c_lambda[j], lam_init, ctx_len)
            dt_bias = _pad_cols(d_dt_bias[j].reshape(1, -1), LANES)
            neg_a = _pad_cols(-jnp.exp(d_a_log[j].astype(F32)).reshape(1, -1), LANES)
            skip = jnp.repeat(d_skip[j].astype(F32), HEAD_DIM)[None, :]
            yf, yb = _ssd_call(xbcc, dt, dt_bias, neg_a, skip, ctx_len)
            w_out = od_w_out[j].astype(BF16)
            acts, vecs = (oa, yf, yb, z), (c_norm_g[j][None, :], d_norm_g[j][None, :])
            body = functools.partial(_outproj_odd_body, out_scale=1.0 - lam_init)
        t_off = ctx_tiles if last else 0
        if l % 2 == 0:
            act_offs = (t_off,) * 4
        else:
            assert last, "differential attention is only computed for latent queries"
            act_offs = (0,) + (t_off,) * 3
        yn = _outproj_call(body, acts, act_offs, vecs, w_out, g[1:2], n_t - t_off)
        seg_starts = (0,) if last else (0, ctx_tiles)
        hs = (_ffn_call(hs, yn, m[2], g[2:3], m[3], m[4], wg, wu, cw, wd, g[3:4], m[5], ctx_tiles, t_off,
                        seg_starts),)
    return hs[0]
```
